```python
import math
import jax
import jax.numpy as jnp
from jax import lax
import numpy as np

D_MODEL = 1024
BATCH = 16
SEQ = 2048
DEPTH = 2

GRID_W = 64
CTX_LEN = 256
EPS = 1e-6
ROPE_BASE = 10000.0
QBLOCK = 128
N_MOD = 6

MLA_HEADS = 4
MLA_Q_RANK = 256
MLA_KV_RANK = 128
MLA_NOPE = 64
MLA_ROPE = 32
MLA_V = 64
MLA_SCALE = (MLA_NOPE + MLA_ROPE) ** -0.5

SWA_HEADS = 4
SWA_KV_HEADS = 2
SWA_HEAD_DIM = 64
SWA_WINDOW = 128
SWA_SCALE = SWA_HEAD_DIM ** -0.5

DIFF_HEADS = 4
DIFF_QK = 32
DIFF_V = 64
DIFF_SCALE = DIFF_QK ** -0.5

POOL_WINDOWS = (2, 4, 8, 16)
POOL_GROUP = 64

MIX_A = MLA_HEADS * MLA_V
MIX_B = SWA_HEADS * SWA_HEAD_DIM
MIX_C = DIFF_HEADS * DIFF_V
MIX_D = len(POOL_WINDOWS) * POOL_GROUP
D_MIX = MIX_A + MIX_B + MIX_C + MIX_D

IN_SIZES = (MLA_Q_RANK, MLA_KV_RANK, MLA_ROPE,
            SWA_HEADS * SWA_HEAD_DIM, SWA_KV_HEADS * SWA_HEAD_DIM, SWA_KV_HEADS * SWA_HEAD_DIM,
            2 * DIFF_HEADS * DIFF_QK, 2 * DIFF_HEADS * DIFF_QK, DIFF_HEADS * DIFF_V,
            MIX_D)
D_IN = sum(IN_SIZES)

N_EXPERT_GROUPS = 4
EXPERTS_PER_GROUP = 4
N_EXPERTS = N_EXPERT_GROUPS * EXPERTS_PER_GROUP
TOP_K_IN_GROUP = 2
D_EXPERT = 512

kernel_name = "hybrid_headgroup_dit_hmoe"


def rms_norm(x, g):
    xf = x.astype(jnp.float32)
    y = xf * lax.rsqrt(jnp.mean(xf * xf, axis=-1, keepdims=True) + EPS)
    return (y * g.astype(jnp.float32)).astype(x.dtype)


def modulate(h, shift, scale):
    return h * (1 + scale) + shift


def axial_rope_table(length, dim):
    rows = length // GRID_W
    row = jnp.repeat(jnp.arange(rows), GRID_W).astype(jnp.float32)
    col = jnp.tile(jnp.arange(GRID_W), rows).astype(jnp.float32)
    n = dim // 4
    inv = ROPE_BASE ** (-jnp.arange(n, dtype=jnp.float32) / n)
    ang = jnp.concatenate([row[:, None] * inv, col[:, None] * inv], axis=-1)
    return jnp.cos(ang), jnp.sin(ang)


def apply_rope(x, rope):
    cos, sin = rope
    xf = x.astype(jnp.float32)
    h = x.shape[-1] // 2
    x1, x2 = xf[..., :h], xf[..., h:]
    cs, sn = cos[None, :, None, :], sin[None, :, None, :]
    return jnp.concatenate([x1 * cs - x2 * sn, x1 * sn + x2 * cs], axis=-1).astype(x.dtype)


def split_in(z):
    points = np.cumsum(IN_SIZES)[:-1].tolist()
    return jnp.split(z, points, axis=-1)


def mixer_keys_values(z, kv_gain, w_ukv, ropes):
    B, L, _ = z.shape
    _, ckv, kr, _, bk, bv, _, dk, dv, _ = split_in(z)
    kva = (rms_norm(ckv, kv_gain) @ w_ukv).reshape(B, L, MLA_HEADS, MLA_NOPE + MLA_V)
    k_nope, va = kva[..., :MLA_NOPE], kva[..., MLA_NOPE:]
    k_rope = kr.reshape(B, L, 1, MLA_ROPE)
    bk = bk.reshape(B, L, SWA_KV_HEADS, SWA_HEAD_DIM)
    bv = bv.reshape(B, L, SWA_KV_HEADS, SWA_HEAD_DIM)
    dk = dk.reshape(B, L, 2 * DIFF_HEADS, DIFF_QK)
    dv = dv.reshape(B, L, DIFF_HEADS, DIFF_V)
    if ropes is not None:
        rope_a, rope_b, rope_c = ropes
        k_rope = apply_rope(k_rope, rope_a)
        bk = apply_rope(bk, rope_b)
        dk = apply_rope(dk, rope_c)
    ka = jnp.concatenate([k_nope, jnp.broadcast_to(k_rope, (B, L, MLA_HEADS, MLA_ROPE))], axis=-1)
    return (ka, va, bk, bv, dk, dv)


def mixer_queries(z, q_gain, w_uq, ropes):
    B, L, _ = z.shape
    cq, _, _, bq, _, _, dq, _, _, u = split_in(z)
    qa = (rms_norm(cq, q_gain) @ w_uq).reshape(B, L, MLA_HEADS, MLA_NOPE + MLA_ROPE)
    q_nope, q_rope = qa[..., :MLA_NOPE], qa[..., MLA_NOPE:]
    bq = bq.reshape(B, L, SWA_HEADS, SWA_HEAD_DIM)
    dq = dq.reshape(B, L, 2 * DIFF_HEADS, DIFF_QK)
    if ropes is not None:
        rope_a, rope_b, rope_c = ropes
        q_rope = apply_rope(q_rope, rope_a)
        bq = apply_rope(bq, rope_b)
        dq = apply_rope(dq, rope_c)
    qa = jnp.concatenate([q_nope, q_rope], axis=-1)
    return (qa, bq, dq, u)


def sweep_query_blocks(fn, q):
    B, L = q.shape[:2]
    nb = L // QBLOCK
    qb = jnp.moveaxis(q.reshape((B, nb, QBLOCK) + q.shape[2:]), 1, 0)
    out = lax.map(fn, qb)
    return jnp.moveaxis(out, 0, 1).reshape((B, L) + out.shape[3:])


def softmax_attend(q, k, v, scale):
    s = jnp.einsum('bqhd,bkhd->bhqk', q, k).astype(jnp.float32) * scale
    p = jax.nn.softmax(s, axis=-1)
    return jnp.einsum('bhqk,bkhd->bqhd', p.astype(v.dtype), v)


def diff_attend(q, k, v, lam):
    B, Q = q.shape[:2]
    M = k.shape[1]
    s = jnp.einsum('bqhd,bkhd->bhqk', q, k).astype(jnp.float32) * DIFF_SCALE
    p = jax.nn.softmax(s, axis=-1).reshape(B, DIFF_HEADS, 2, Q, M)
    a = p[:, :, 0] - lam * p[:, :, 1]
    return jnp.einsum('bhqk,bkhd->bqhd', a.astype(v.dtype), v)


def gqa_sink_dense(q, k, v, sink):
    B, Q, H, d = q.shape
    G = H // SWA_KV_HEADS
    qg = q.reshape(B, Q, SWA_KV_HEADS, G, d)
    s = jnp.einsum('bqkgd,bmkd->bkgqm', qg, k).astype(jnp.float32) * SWA_SCALE
    sk = jnp.broadcast_to(sink.astype(jnp.float32).reshape(1, SWA_KV_HEADS, G, 1, 1), s.shape[:-1] + (1,))
    p = jax.nn.softmax(jnp.concatenate([s, sk], axis=-1), axis=-1)[..., :-1]
    o = jnp.einsum('bkgqm,bmkd->bqkgd', p.astype(v.dtype), v)
    return o.reshape(B, Q, H, d)


def window_gqa_sink(q, k, v, kc, vc, sink):
    B, L, H, d = q.shape
    G = H // SWA_KV_HEADS
    nb = L // QBLOCK
    J = 3 * QBLOCK
    C = kc.shape[1]

    def band(a):
        ap = jnp.pad(a, ((0, 0), (QBLOCK, QBLOCK), (0, 0), (0, 0))).reshape(B, nb + 2, QBLOCK, SWA_KV_HEADS, d)
        return jnp.concatenate([ap[:, :-2], ap[:, 1:-1], ap[:, 2:]], axis=2)

    kb, vb = band(k), band(v)
    qg = q.reshape(B, nb, QBLOCK, SWA_KV_HEADS, G, d)
    s_loc = jnp.einsum('bnqkgd,bnjkd->bnkgqj', qg, kb).astype(jnp.float32) * SWA_SCALE
    qpos = jnp.arange(L).reshape(nb, QBLOCK)
    kpos = (jnp.arange(nb) * QBLOCK - QBLOCK)[:, None] + jnp.arange(J)[None, :]
    rel = kpos[:, None, :] - qpos[:, :, None]
    valid = (jnp.abs(rel) <= SWA_WINDOW) & (kpos[:, None, :] >= 0) & (kpos[:, None, :] < L)
    s_loc = jnp.where(valid[None, :, None, None], s_loc, -jnp.inf)
    s_ctx = jnp.einsum('bnqkgd,bckd->bnkgqc', qg, kc).astype(jnp.float32) * SWA_SCALE
    sk = jnp.broadcast_to(sink.astype(jnp.float32).reshape(1, 1, SWA_KV_HEADS, G, 1, 1), s_loc.shape[:-1] + (1,))
    p = jax.nn.softmax(jnp.concatenate([s_loc, s_ctx, sk], axis=-1), axis=-1)
    p_loc, p_ctx = p[..., :J], p[..., J:J + C]
    o = (jnp.einsum('bnkgqj,bnjkd->bnqkgd', p_loc.astype(v.dtype), vb)
         + jnp.einsum('bnkgqc,bckd->bnqkgd', p_ctx.astype(vc.dtype), vc))
    return o.reshape(B, L, H, d)


def multiscale_pool(u, pool_w, pool_scale):
    B, L, _ = u.shape
    ug = u.reshape(B, L, len(POOL_WINDOWS), POOL_GROUP).astype(jnp.float32)
    t = jnp.arange(L)
    diffs = []
    for gi, w in enumerate(POOL_WINDOWS):
        xg = ug[:, :, gi]
        half = w // 2
        xp = jnp.pad(xg, ((0, 0), (half, w - half), (0, 0)))
        cs = jnp.concatenate([jnp.zeros((B, 1, POOL_GROUP), jnp.float32), jnp.cumsum(xp, axis=1)], axis=1)
        win_sum = cs[:, w:w + L] - cs[:, :L]
        cnt = (jnp.minimum(t - half + w, L) - jnp.maximum(t - half, 0)).astype(jnp.float32)
        diffs.append(win_sum / cnt[None, :, None] - xg)
    dg = jnp.stack(diffs, axis=2)
    y = jnp.einsum('blgc,gcd->blgd', dg, pool_w.astype(jnp.float32)).reshape(B, L, MIX_D)
    return (y * pool_scale.astype(jnp.float32)).astype(u.dtype)


def latent_mix(qx, kvx, kvc, sink, lam, lam_init, subln_g, pool_w, pool_scale):
    qa, bq, dq, u = qx
    ka, va, bk, bv, dk, dv = kvx
    kac, vac, bkc, bvc, dkc, dvc = kvc
    B, L = u.shape[:2]
    ka_all = jnp.concatenate([ka, kac], axis=1)
    va_all = jnp.concatenate([va, vac], axis=1)
    oa = sweep_query_blocks(lambda q: softmax_attend(q, ka_all, va_all, MLA_SCALE), qa)
    ob = window_gqa_sink(bq, bk, bv, bkc, bvc, sink)
    dk_all = jnp.concatenate([dk, dkc], axis=1)
    dv_all = jnp.concatenate([dv, dvc], axis=1)
    oc = sweep_query_blocks(lambda q: diff_attend(q, dk_all, dv_all, lam), dq)
    oc = rms_norm(oc, subln_g) * (1.0 - lam_init)
    od = multiscale_pool(u, pool_w, pool_scale)
    return jnp.concatenate([oa.reshape(B, L, MIX_A), ob.reshape(B, L, MIX_B),
                            oc.reshape(B, L, MIX_C), od], axis=-1)


def context_mix(qc, kvc, sink, lam, lam_init, subln_g, pool_w, pool_scale):
    qa, bq, dq, u = qc
    kac, vac, bkc, bvc, dkc, dvc = kvc
    B, C = u.shape[:2]
    oa = softmax_attend(qa, kac, vac, MLA_SCALE)
    ob = gqa_sink_dense(bq, bkc, bvc, sink)
    oc = rms_norm(diff_attend(dq, dkc, dvc, lam), subln_g) * (1.0 - lam_init)
    od = multiscale_pool(u, pool_w, pool_scale)
    return jnp.concatenate([oa.reshape(B, C, MIX_A), ob.reshape(B, C, MIX_B),
                            oc.reshape(B, C, MIX_C), od], axis=-1)


def hier_moe(h, gw, gb, ew, eb, w_gu, w_down):
    B, L, D = h.shape
    tok = h.reshape(-1, D)
    g_logits = (tok @ gw + gb).astype(jnp.float32)
    g_prob = jax.nn.softmax(g_logits, axis=-1)
    g_sel = jnp.argmax(g_logits, axis=-1)
    p_g = jnp.take_along_axis(g_prob, g_sel[:, None], axis=-1)
    e_logits = (tok @ ew + eb).astype(jnp.float32).reshape(-1, N_EXPERT_GROUPS, EXPERTS_PER_GROUP)
    e_in = jnp.take_along_axis(e_logits, g_sel[:, None, None], axis=1)[:, 0]
    top_v, top_i = lax.top_k(e_in, TOP_K_IN_GROUP)
    w_sel = jax.nn.softmax(top_v, axis=-1) * p_g
    e_idx = g_sel[:, None] * EXPERTS_PER_GROUP + top_i
    combine = jnp.sum(jax.nn.one_hot(e_idx, N_EXPERTS, dtype=jnp.float32) * w_sel[..., None], axis=1)
    y = jnp.zeros((tok.shape[0], D), jnp.float32)
    for e in range(N_EXPERTS):
        gu = tok @ w_gu[e]
        act = jax.nn.silu(gu[:, :D_EXPERT]) * gu[:, D_EXPERT:]
        y = y + combine[:, e:e + 1] * (act @ w_down[e]).astype(jnp.float32)
    return y.reshape(B, L, D).astype(h.dtype)


def setup_inputs(seed: int = 0) -> dict:
    key = jax.random.key(seed)
    ks = jax.random.split(key, 24)
    f32 = jnp.float32

    def nrm(k, shape, scale):
        return jax.random.normal(k, shape, f32) * scale

    return {
        "x": nrm(ks[0], (BATCH, SEQ, D_MODEL), 1.0),
        "c": nrm(ks[1], (BATCH, D_MODEL), 1.0),
        "ctx": nrm(ks[2], (BATCH, CTX_LEN, D_MODEL), 1.0),
        "c_ctx": nrm(ks[3], (D_MODEL,), 1.0),
        "ada_w": nrm(ks[4], (DEPTH, D_MODEL, N_MOD * D_MODEL), 0.5 * D_MODEL ** -0.5),
        "ada_b": nrm(ks[5], (DEPTH, N_MOD * D_MODEL), 0.02),
        "norm_g": 1.0 + nrm(ks[6], (DEPTH, 4, D_MODEL), 0.05),
        "w_in": nrm(ks[7], (DEPTH, D_MODEL, D_IN), D_MODEL ** -0.5),
        "mla_q_gain": 1.0 + nrm(ks[8], (DEPTH, MLA_Q_RANK), 0.05),
        "mla_kv_gain": 1.0 + nrm(ks[9], (DEPTH, MLA_KV_RANK), 0.05),
        "mla_w_uq": nrm(ks[10], (DEPTH, MLA_Q_RANK, MLA_HEADS * (MLA_NOPE + MLA_ROPE)), MLA_Q_RANK ** -0.5),
        "mla_w_ukv": nrm(ks[11], (DEPTH, MLA_KV_RANK, MLA_HEADS * (MLA_NOPE + MLA_V)), MLA_KV_RANK ** -0.5),
        "swa_sink": nrm(ks[12], (DEPTH, SWA_HEADS), 0.5),
        "diff_lambda": nrm(ks[13], (DEPTH, 4, DIFF_QK), 0.1),
        "diff_subln": 1.0 + nrm(ks[14], (DEPTH, DIFF_V), 0.05),
        "pool_w": nrm(ks[15], (DEPTH, len(POOL_WINDOWS), POOL_GROUP, POOL_GROUP), POOL_GROUP ** -0.5),
        "pool_scale": 1.0 + nrm(ks[16], (DEPTH, MIX_D), 0.1),
        "w_out": nrm(ks[17], (DEPTH, D_MIX, D_MODEL), D_MIX ** -0.5),
        "router_gw": nrm(ks[18], (DEPTH, D_MODEL, N_EXPERT_GROUPS), D_MODEL ** -0.5),
        "router_gb": nrm(ks[19], (DEPTH, N_EXPERT_GROUPS), 0.01),
        "router_ew": nrm(ks[20], (DEPTH, D_MODEL, N_EXPERTS), D_MODEL ** -0.5),
        "router_eb": nrm(ks[21], (DEPTH, N_EXPERTS), 0.01),
        "exp_w_gu": nrm(ks[22], (DEPTH, N_EXPERTS, D_MODEL, 2 * D_EXPERT), D_MODEL ** -0.5),
        "exp_w_down": nrm(ks[23], (DEPTH, N_EXPERTS, D_EXPERT, D_MODEL), D_EXPERT ** -0.5),
    }


def reference(x, c, ctx, c_ctx, ada_w, ada_b, norm_g, w_in, mla_q_gain, mla_kv_gain, mla_w_uq, mla_w_ukv,
              swa_sink, diff_lambda, diff_subln, pool_w, pool_scale, w_out, router_gw, router_gb,
              router_ew, router_eb, exp_w_gu, exp_w_down):
    B, L, D = x.shape
    ropes = (axial_rope_table(L, MLA_ROPE), axial_rope_table(L, SWA_HEAD_DIM), axial_rope_table(L, DIFF_QK))
    silu_c = jax.nn.silu(c)
    silu_cc = jax.nn.silu(c_ctx)
    h_ctx = ctx
    for layer in range(DEPTH):
        update_ctx = layer < DEPTH - 1
        mx = jnp.split((silu_c @ ada_w[layer] + ada_b[layer])[:, None, :], N_MOD, axis=-1)
        mc = jnp.split((silu_cc @ ada_w[layer] + ada_b[layer])[None, None, :], N_MOD, axis=-1)
        lam_init = 0.8 - 0.6 * math.exp(-0.3 * layer)
        dl = diff_lambda[layer].astype(jnp.float32)
        lam = jnp.exp(jnp.sum(dl[0] * dl[1])) - jnp.exp(jnp.sum(dl[2] * dl[3])) + lam_init

        hx = modulate(rms_norm(x, norm_g[layer, 0]), mx[0], mx[1])
        hc = modulate(rms_norm(h_ctx, norm_g[layer, 0]), mc[0], mc[1])
        zx = hx @ w_in[layer]
        zc = hc @ w_in[layer]
        kvx = mixer_keys_values(zx, mla_kv_gain[layer], mla_w_ukv[layer], ropes)
        kvc = mixer_keys_values(zc, mla_kv_gain[layer], mla_w_ukv[layer], None)
        qx = mixer_queries(zx, mla_q_gain[layer], mla_w_uq[layer], ropes)
        ox = latent_mix(qx, kvx, kvc, swa_sink[layer], lam, lam_init, diff_subln[layer],
                        pool_w[layer], pool_scale[layer]) @ w_out[layer]
        x = x + mx[2] * rms_norm(ox, norm_g[layer, 1])
        if update_ctx:
            qc = mixer_queries(zc, mla_q_gain[layer], mla_w_uq[layer], None)
            oc = context_mix(qc, kvc, swa_sink[layer], lam, lam_init, diff_subln[layer],
                             pool_w[layer], pool_scale[layer]) @ w_out[layer]
            h_ctx = h_ctx + mc[2] * rms_norm(oc, norm_g[layer, 1])

        fx = hier_moe(modulate(rms_norm(x, norm_g[layer, 2]), mx[3], mx[4]), router_gw[layer], router_gb[layer],
                      router_ew[layer], router_eb[layer], exp_w_gu[layer], exp_w_down[layer])
        x = x + mx[5] * rms_norm(fx, norm_g[layer, 3])
        if update_ctx:
            fc = hier_moe(modulate(rms_norm(h_ctx, norm_g[layer, 2]), mc[3], mc[4]), router_gw[layer],
                          router_gb[layer], router_ew[layer], router_eb[layer], exp_w_gu[layer], exp_w_down[layer])
            h_ctx = h_ctx + mc[5] * rms_norm(fc, norm_g[layer, 3])
    return x
```

```python
import functools
import math

import jax
import jax.numpy as jnp
from jax import lax
from jax.experimental import pallas as pl
from jax.experimental.pallas import tpu as pltpu

F32 = jnp.float32
BF = jnp.bfloat16

D_MODEL = 1024
DEPTH = 2
GRID_W = 64
EPS = 1e-6
ROPE_BASE = 10000.0
N_MOD = 6

MLA_HEADS = 4
MLA_Q_RANK = 256
MLA_KV_RANK = 128
MLA_NOPE = 64
MLA_ROPE = 32
MLA_V = 64
MLA_QK = MLA_NOPE + MLA_ROPE
MLA_SCALE = MLA_QK ** -0.5

SWA_HEADS = 4
SWA_KV_HEADS = 2
SWA_HEAD_DIM = 64
SWA_WINDOW = 128
SWA_SCALE = SWA_HEAD_DIM ** -0.5

DIFF_HEADS = 4
DIFF_QK = 32
DIFF_V = 64
DIFF_SCALE = DIFF_QK ** -0.5

POOL_WINDOWS = (2, 4, 8, 16)
POOL_GROUP = 64
POOL_HALO = 8
MIX_D = len(POOL_WINDOWS) * POOL_GROUP

N_EXPERT_GROUPS = 4
EXPERTS_PER_GROUP = 4
N_EXPERTS = 16
D_EXPERT = 512

LOG2E = math.log2(math.e)
NEG = -1e30
LANES = 128
MOD_ROWS = 24
VMEM_LIMIT = 48 * 1024 * 1024

R_CQ, R_BQ, R_DQ, R_DV, R_U, R_CKV, R_BK, R_BV, R_END = 0, 256, 512, 768, 1024, 1280, 1408, 1536, 1664
T_CKV, T_KR, T_DK, T_END = 0, 128, 160, 416
P_QA, P_QB, P_KB, P_QC, P_END = 0, 384, 640, 768, 1024


def _dot(a, b):
    return jnp.dot(a, b, preferred_element_type=F32)


def _dot_nt(a, b):
    return lax.dot_general(a, b, (((1,), (1,)), ((), ())), preferred_element_type=F32)


def _rms(x, g):
    return x * lax.rsqrt(jnp.mean(x * x, axis=-1, keepdims=True) + EPS) * g


def _params(sem):
    return pltpu.CompilerParams(dimension_semantics=sem, vmem_limit_bytes=VMEM_LIMIT)


def _adaln_kernel(c_ref, w_ref, b_ref, o_ref):
    c = c_ref[...]
    s = (c * jax.nn.sigmoid(c)).astype(BF)
    o_ref[...] = _dot(s, w_ref[...].astype(BF)) + b_ref[...]


def _adaln(cc, ada_w, ada_b):
    return pl.pallas_call(
        _adaln_kernel,
        out_shape=jax.ShapeDtypeStruct((DEPTH, N_MOD, MOD_ROWS, D_MODEL), F32),
        grid=(DEPTH, N_MOD),
        in_specs=[
            pl.BlockSpec((MOD_ROWS, D_MODEL), lambda l, j: (0, 0)),
            pl.BlockSpec((None, D_MODEL, D_MODEL), lambda l, j: (l, 0, j)),
            pl.BlockSpec((None, None, 1, D_MODEL), lambda l, j: (l, j, 0, 0)),
        ],
        out_specs=pl.BlockSpec((None, None, MOD_ROWS, D_MODEL), lambda l, j: (l, j, 0, 0)),
        compiler_params=_params(("arbitrary", "arbitrary")),
        name="adaln",
    )(cc, ada_w, ada_b.reshape(DEPTH, N_MOD, 1, D_MODEL))


def _rope_rows(xf, c, s, m1, half):
    w = xf.shape[-1]
    up = pltpu.roll(xf, w - half, axis=1)
    dn = pltpu.roll(xf, half, axis=1)
    return xf * c + jnp.where(m1 > 0.5, up, dn) * s


def _rope_cols(xt, c, s):
    h = xt.shape[0] // 2
    x1, x2 = xt[:h], xt[h:]
    return jnp.concatenate([x1 * c - x2 * s, x1 * s + x2 * c], axis=0)


def _inproj_kernel(x_ref, mod_ref, ng_ref, wr_ref, wkt_ref, qg_ref, kvgr_ref, kvgc_ref,
                   wuq_ref, wukt_ref, wuv_ref, tc_ref, ts_ref, m_ref, ct_ref, st_ref,
                   qa_ref, qb_ref, qc_ref, u_ref, va_ref, vb_ref, vc_ref, kb_ref, kta_ref, ktc_ref):
    x = x_ref[...]
    mod = mod_ref[...]
    hx = _rms(x, ng_ref[0:1, :]) * (1.0 + mod[1:2, :]) + mod[0:1, :]
    hb = hx.astype(BF)
    zr = _dot(hb, wr_ref[...])
    zk = _dot_nt(wkt_ref[...], hb)

    tc = tc_ref[...]
    ts = ts_ref[...]
    m1 = m_ref[...]

    cqn = _rms(zr[:, R_CQ:R_CQ + MLA_Q_RANK], qg_ref[...]).astype(BF)
    qa = _dot(cqn, wuq_ref[...])
    qa_ref[...] = _rope_rows(qa, tc[:, P_QA:P_QB], ts[:, P_QA:P_QB], m1[:, P_QA:P_QB],
                             MLA_ROPE // 2).astype(BF)
    qb_ref[...] = _rope_rows(zr[:, R_BQ:R_DQ], tc[:, P_QB:P_KB], ts[:, P_QB:P_KB], m1[:, P_QB:P_KB],
                             SWA_HEAD_DIM // 2).astype(BF)
    qc_ref[...] = _rope_rows(zr[:, R_DQ:R_DV], tc[:, P_QC:P_END], ts[:, P_QC:P_END], m1[:, P_QC:P_END],
                             DIFF_QK // 2).astype(BF)
    kb_ref[...] = _rope_rows(zr[:, R_BK:R_BV], tc[:, P_KB:P_QC], ts[:, P_KB:P_QC], m1[:, P_KB:P_QC],
                             SWA_HEAD_DIM // 2).astype(BF)
    u_ref[...] = zr[:, R_U:R_CKV]
    vb_ref[...] = zr[:, R_BV:R_END].astype(BF)
    vc_ref[...] = zr[:, R_DV:R_U].astype(BF)

    ckvn = _rms(zr[:, R_CKV:R_BK], kvgr_ref[...]).astype(BF)
    va_ref[...] = _dot(ckvn, wuv_ref[...]).astype(BF)

    ckvt = zk[T_CKV:T_KR]
    ckvtn = ckvt * lax.rsqrt(jnp.mean(ckvt * ckvt, axis=0, keepdims=True) + EPS) * kvgc_ref[...]
    knt = _dot(wukt_ref[...], ckvtn.astype(BF))
    ct = ct_ref[...]
    st = st_ref[...]
    krt = _rope_cols(zk[T_KR:T_DK], ct, st)
    pieces = []
    for h in range(MLA_HEADS):
        pieces.append(knt[h * MLA_NOPE:(h + 1) * MLA_NOPE])
        pieces.append(krt)
    kta_ref[...] = jnp.concatenate(pieces, axis=0).astype(BF)

    dk = zk[T_DK:T_END]
    ktc_ref[...] = jnp.concatenate(
        [_rope_cols(dk[j * DIFF_QK:(j + 1) * DIFF_QK], ct, st) for j in range(2 * DIFF_HEADS)],
        axis=0).astype(BF)


def _inproj(layer, x, mod_l, mod_row, tm, pw, tabs):
    B, L, _ = x.shape
    nt = L // tm
    tab_c, tab_s, msk, cos_t, sin_t = tabs
    row = (lambda b: b) if mod_row is None else (lambda b: mod_row)

    def wspec(shape):
        nd = len(shape)
        return pl.BlockSpec((None,) + shape, lambda i, b: (layer,) + (0,) * nd)

    tok = lambda w: pl.BlockSpec((None, tm, w), lambda i, b: (b, i, 0))
    tokt = lambda w: pl.BlockSpec((None, w, tm), lambda i, b: (b, 0, i))
    in_specs = [
        tok(D_MODEL),
        pl.BlockSpec((None, N_MOD, D_MODEL), lambda i, b: (row(b), 0, 0)),
        wspec((4, D_MODEL)),
        wspec((D_MODEL, R_END)),
        wspec((T_END, D_MODEL)),
        wspec((1, MLA_Q_RANK)),
        wspec((1, MLA_KV_RANK)),
        wspec((MLA_KV_RANK, 1)),
        wspec((MLA_Q_RANK, MLA_HEADS * MLA_QK)),
        wspec((MLA_HEADS * MLA_NOPE, MLA_KV_RANK)),
        wspec((MLA_KV_RANK, MLA_HEADS * MLA_V)),
        pl.BlockSpec((tm, P_END), lambda i, b: (i, 0)),
        pl.BlockSpec((tm, P_END), lambda i, b: (i, 0)),
        pl.BlockSpec((1, P_END), lambda i, b: (0, 0)),
        pl.BlockSpec((MLA_ROPE // 2, tm), lambda i, b: (0, i)),
        pl.BlockSpec((MLA_ROPE // 2, tm), lambda i, b: (0, i)),
    ]
    widths = [(MLA_HEADS * MLA_QK, BF), (256, BF), (256, BF), (MIX_D, F32),
              (256, BF), (128, BF), (256, BF), (128, BF)]
    out_shape = [jax.ShapeDtypeStruct((B, L, w), dt) for w, dt in widths]
    out_specs = [tok(w) for w, _ in widths]
    out_shape += [jax.ShapeDtypeStruct((B, MLA_HEADS * MLA_QK, L), BF),
                  jax.ShapeDtypeStruct((B, 2 * DIFF_HEADS * DIFF_QK, L), BF)]
    out_specs += [tokt(MLA_HEADS * MLA_QK), tokt(2 * DIFF_HEADS * DIFF_QK)]
    return pl.pallas_call(
        _inproj_kernel,
        out_shape=out_shape,
        grid=(nt, B),
        in_specs=in_specs,
        out_specs=out_specs,
        compiler_params=_params(("arbitrary", "arbitrary")),
        name="inproj",
    )(x, mod_l, pw["norm_g"], pw["w_r"], pw["w_kt"], pw["q_gain"], pw["kv_gain_r"], pw["kv_gain_c"],
      pw["w_uq"], pw["w_ukt"], pw["w_uv"], tab_c, tab_s, msk, cos_t, sin_t)


def _softmax_pv(q, segs, extra=None):
    ss = []
    m = None
    for kt, _, mask in segs:
        s = _dot(q, kt)
        if mask is not None:
            s = jnp.where(mask, s, NEG)
        ss.append(s)
        sm = jnp.max(s, axis=-1, keepdims=True)
        m = sm if m is None else jnp.maximum(m, sm)
    if extra is not None:
        m = jnp.maximum(m, extra)
    l = None
    r = None
    for s, (_, v, _) in zip(ss, segs):
        e = jnp.exp2(s - m)
        ls = jnp.sum(e, axis=-1, keepdims=True)
        rs = _dot(e.astype(BF), v)
        l = ls if l is None else l + ls
        r = rs if r is None else r + rs
    if extra is not None:
        l = l + jnp.exp2(extra - m)
    return r / l


def _mla_kernel(*refs, nseg):
    q_ref = refs[0]
    kt_refs = refs[1:1 + nseg]
    v_refs = refs[1 + nseg:1 + 2 * nseg]
    o_ref = refs[1 + 2 * nseg]
    q = q_ref[...]
    lane = lax.broadcasted_iota(jnp.int32, (1, LANES), 1)
    blocks = []
    for p in range(MLA_HEADS // 2):
        pair = []
        for h in (2 * p, 2 * p + 1):
            qh = q[:, h * MLA_QK:(h + 1) * MLA_QK]
            segs = [(kt[h * MLA_QK:(h + 1) * MLA_QK, :], v[:, p * LANES:(p + 1) * LANES], None)
                    for kt, v in zip(kt_refs, v_refs)]
            pair.append(_softmax_pv(qh, segs))
        blocks.append(jnp.where(lane < MLA_V, pair[0], pair[1]))
    o_ref[...] = jnp.concatenate(blocks, axis=-1).astype(BF)


def _diff_kernel(*refs, nseg, lam_init):
    q_ref = refs[0]
    kt_refs = refs[1:1 + nseg]
    v_refs = refs[1 + nseg:1 + 2 * nseg]
    dl_ref, g_ref, o_ref = refs[1 + 2 * nseg:]
    q = q_ref[...]
    dl = dl_ref[...]
    lam = (jnp.exp(jnp.sum(dl[0:1] * dl[1:2], axis=-1, keepdims=True))
           - jnp.exp(jnp.sum(dl[2:3] * dl[3:4], axis=-1, keepdims=True)) + lam_init)
    lane = lax.broadcasted_iota(jnp.int32, (1, LANES), 1)
    lo = lane < DIFF_V
    blocks = []
    for p in range(DIFF_HEADS // 2):
        pair = []
        for h in (2 * p, 2 * p + 1):
            comp = []
            for c in range(2):
                j = 2 * h + c
                qj = q[:, j * DIFF_QK:(j + 1) * DIFF_QK]
                segs = [(kt[j * DIFF_QK:(j + 1) * DIFF_QK, :], v[:, p * LANES:(p + 1) * LANES], None)
                        for kt, v in zip(kt_refs, v_refs)]
                comp.append(_softmax_pv(qj, segs))
            pair.append(comp[0] - lam * comp[1])
        o = jnp.where(lo, pair[0], pair[1])
        o2 = o * o
        ms_lo = jnp.sum(jnp.where(lo, o2, 0.0), axis=-1, keepdims=True) * (1.0 / DIFF_V)
        ms_hi = jnp.sum(jnp.where(lo, 0.0, o2), axis=-1, keepdims=True) * (1.0 / DIFF_V)
        ms = jnp.where(lo, ms_lo, ms_hi)
        blocks.append(o * lax.rsqrt(ms + EPS) * g_ref[...] * (1.0 - lam_init))
    o_ref[...] = jnp.concatenate(blocks, axis=-1).astype(BF)


def _dense_attn(kind, layer, q, kts, vs, tq, extra_in=(), lam_init=0.0):
    B, L, wq = q.shape
    nseg = len(kts)
    in_specs = [pl.BlockSpec((None, tq, wq), lambda b, i: (b, i, 0))]
    for kt in kts:
        in_specs.append(pl.BlockSpec((None,) + kt.shape[1:], lambda b, i: (b, 0, 0)))
    for v in vs:
        in_specs.append(pl.BlockSpec((None,) + v.shape[1:], lambda b, i: (b, 0, 0)))
    if kind == "mla":
        body = functools.partial(_mla_kernel, nseg=nseg)
    else:
        body = functools.partial(_diff_kernel, nseg=nseg, lam_init=lam_init)
        dl, g = extra_in
        in_specs.append(pl.BlockSpec((None, 4, DIFF_QK), lambda b, i: (layer, 0, 0)))
        in_specs.append(pl.BlockSpec((None, 1, LANES), lambda b, i: (layer, 0, 0)))
    return pl.pallas_call(
        body,
        out_shape=jax.ShapeDtypeStruct((B, L, 256), BF),
        grid=(B, L // tq),
        in_specs=in_specs,
        out_specs=pl.BlockSpec((None, tq, 256), lambda b, i: (b, i, 0)),
        compiler_params=_params(("arbitrary", "arbitrary")),
        name=kind,
    )(q, *kts, *vs, *extra_in)


def _swa_kernel(*refs, layer, tq, seq_len, local):
    if local:
        sink_ref, q_ref, kl_ref, vl_ref, kc_ref, vc_ref, o_ref = refs
    else:
        sink_ref, q_ref, kc_ref, vc_ref, o_ref = refs
    i = pl.program_id(1)
    q = q_ref[...]
    lane = lax.broadcasted_iota(jnp.int32, (1, LANES), 1)
    rows = lax.broadcasted_iota(jnp.int32, (2 * tq, 1), 0)
    top = rows < tq
    if local:
        win = tq + 2 * SWA_WINDOW
        q0 = i * tq
        ws = pl.multiple_of(jnp.clip(q0 - SWA_WINDOW, 0, seq_len - win), LANES)
        kwin = kl_ref[pl.ds(ws, win), :]
        vwin = vl_ref[pl.ds(ws, win), :]
        qpos = q0 + jnp.where(top, rows, rows - tq)
        kpos = ws + lax.broadcasted_iota(jnp.int32, (1, win), 1)
        band = jnp.abs(kpos - qpos) <= SWA_WINDOW
    blocks = []
    for k in range(SWA_KV_HEADS):
        q2 = jnp.concatenate([q[:, (2 * k) * SWA_HEAD_DIM:(2 * k + 1) * SWA_HEAD_DIM],
                              q[:, (2 * k + 1) * SWA_HEAD_DIM:(2 * k + 2) * SWA_HEAD_DIM]], axis=0)
        sl = slice(k * SWA_HEAD_DIM, (k + 1) * SWA_HEAD_DIM)
        sk = jnp.where(top, sink_ref[layer, 2 * k], sink_ref[layer, 2 * k + 1]) * LOG2E
        ss = []
        if local:
            ss.append((jnp.where(band, _dot_nt(q2, kwin[:, sl]), NEG), vwin))
        ss.append((_dot_nt(q2, kc_ref[:, sl]), vc_ref[...]))
        m = sk
        for s, _ in ss:
            m = jnp.maximum(m, jnp.max(s, axis=-1, keepdims=True))
        l = jnp.exp2(sk - m)
        r = None
        for s, v in ss:
            e = jnp.exp2(s - m)
            l = l + jnp.sum(e, axis=-1, keepdims=True)
            rs = _dot(e.astype(BF), v)
            r = rs if r is None else r + rs
        r = r / l
        r_top, r_bot = r[:tq], r[tq:]
        if k == 0:
            r_bot = pltpu.roll(r_bot, SWA_HEAD_DIM, axis=1)
        else:
            r_top = pltpu.roll(r_top, SWA_HEAD_DIM, axis=1)
        blocks.append(jnp.where(lane < SWA_HEAD_DIM, r_top, r_bot))
    o_ref[...] = jnp.concatenate(blocks, axis=-1).astype(BF)


def _swa(layer, sink, q, k_loc, v_loc, k_ctx, v_ctx, tq):
    B, L, _ = q.shape
    C = k_ctx.shape[1]
    local = k_loc is not None
    full = lambda n: pl.BlockSpec((None, n, 128), lambda b, i: (b, 0, 0))
    in_specs = [pl.BlockSpec(memory_space=pltpu.SMEM),
                pl.BlockSpec((None, tq, 256), lambda b, i: (b, i, 0))]
    args = [sink, q]
    if local:
        in_specs += [full(L), full(L)]
        args += [k_loc, v_loc]
    in_specs += [full(C), full(C)]
    args += [k_ctx, v_ctx]
    return pl.pallas_call(
        functools.partial(_swa_kernel, layer=layer, tq=tq, seq_len=L, local=local),
        out_shape=jax.ShapeDtypeStruct((B, L, 256), BF),
        grid=(B, L // tq),
        in_specs=in_specs,
        out_specs=pl.BlockSpec((None, tq, 256), lambda b, i: (b, i, 0)),
        compiler_params=_params(("arbitrary", "arbitrary")),
        name="swa",
    )(*args)


def _pool_diffs(u, prev, nxt, i, nt, tm, seq_len):
    n = tm + 2 * POOL_HALO
    ue = jnp.concatenate([jnp.where(i > 0, prev, 0.0), u, jnp.where(i < nt - 1, nxt, 0.0)], axis=0)

    def sh(a, k):
        return pltpu.roll(a, k % n, axis=0)

    s2 = ue + sh(ue, 1)
    s4 = sh(s2, 1) + sh(s2, -1)
    s8 = sh(s4, 2) + sh(s4, -2)
    s16 = sh(s8, 4) + sh(s8, -4)
    lane = lax.broadcasted_iota(jnp.int32, (1, MIX_D), 1)
    t = i * tm + lax.broadcasted_iota(jnp.int32, (tm, 1), 0)
    win = None
    cnt = None
    for gi, (w, s) in enumerate(zip(POOL_WINDOWS, (s2, s4, s8, s16))):
        half = w // 2
        c = (jnp.minimum(t - half + w, seq_len) - jnp.maximum(t - half, 0)).astype(F32)
        sv = s[POOL_HALO:POOL_HALO + tm]
        if win is None:
            win, cnt = sv, c
        else:
            sel = lane >= gi * POOL_GROUP
            win = jnp.where(sel, sv, win)
            cnt = jnp.where(sel, c, cnt)
    return win / cnt - u


def _outproj_kernel(oa_ref, ob_ref, oc_ref, u_ref, up_ref, un_ref, x_ref, mod_ref, ng_ref,
                    wo_ref, pw_ref, ps_ref, rwh_ref, rwl_ref, rb_ref,
                    xo_ref, h_ref, cb_ref, *, tm, seq_len, nt):
    i = pl.program_id(1)
    u = u_ref[...]
    dg = _pool_diffs(u, up_ref[...], un_ref[...], i, nt, tm, seq_len)
    od = _dot(dg.astype(BF), pw_ref[...]) * ps_ref[...]
    mix = jnp.concatenate([oa_ref[...], ob_ref[...], oc_ref[...], od.astype(BF)], axis=-1)
    ox = _dot(mix, wo_ref[...])
    mod = mod_ref[...]
    x = x_ref[...] + mod[2:3, :] * _rms(ox, ng_ref[1:2, :])
    xo_ref[...] = x
    h = _rms(x, ng_ref[2:3, :]) * (1.0 + mod[4:5, :]) + mod[3:4, :]
    hh = h.astype(BF)
    h_ref[...] = hh
    hl = (h - hh.astype(F32)).astype(BF)
    wh = rwh_ref[...]
    logits = _dot(hh, wh) + _dot(hl, wh) + _dot(hh, rwl_ref[...]) + rb_ref[...]

    lane = lax.broadcasted_iota(jnp.int32, (1, LANES), 1)
    lanef = lane.astype(F32)
    is_g = (lane >= N_EXPERTS) & (lane < N_EXPERTS + N_EXPERT_GROUPS)
    gl = jnp.where(is_g, logits, NEG)
    gmax = jnp.max(gl, axis=-1, keepdims=True)
    gsel = jnp.min(jnp.where(gl == gmax, lanef, 1e9), axis=-1, keepdims=True) - N_EXPERTS
    p_g = 1.0 / jnp.sum(jnp.where(is_g, jnp.exp(gl - gmax), 0.0), axis=-1, keepdims=True)
    grp = jnp.right_shift(lane, 2).astype(F32)
    el = jnp.where((lane < N_EXPERTS) & (grp == gsel), logits, NEG)
    v1 = jnp.max(el, axis=-1, keepdims=True)
    i1 = jnp.min(jnp.where(el == v1, lanef, 1e9), axis=-1, keepdims=True)
    el2 = jnp.where(lanef == i1, NEG, el)
    v2 = jnp.max(el2, axis=-1, keepdims=True)
    i2 = jnp.min(jnp.where(el2 == v2, lanef, 1e9), axis=-1, keepdims=True)
    e21 = jnp.exp(v2 - v1)
    w1 = p_g / (1.0 + e21)
    w2 = w1 * e21
    cb_ref[...] = jnp.where(lanef == i1, w1, 0.0) + jnp.where(lanef == i2, w2, 0.0)


def _outproj(layer, oa, ob, oc, u, x, mod_l, mod_row, tm, pw):
    B, L, _ = x.shape
    nt = L // tm
    hb = tm // POOL_HALO
    nhb = L // POOL_HALO
    row = (lambda b: b) if mod_row is None else (lambda b: mod_row)

    def wspec(shape):
        nd = len(shape)
        return pl.BlockSpec((None,) + shape, lambda b, i: (layer,) + (0,) * nd)

    tok = lambda w: pl.BlockSpec((None, tm, w), lambda b, i: (b, i, 0))
    in_specs = [
        tok(256), tok(256), tok(256), tok(MIX_D),
        pl.BlockSpec((None, POOL_HALO, MIX_D), lambda b, i: (b, jnp.maximum(i * hb - 1, 0), 0)),
        pl.BlockSpec((None, POOL_HALO, MIX_D), lambda b, i: (b, jnp.minimum((i + 1) * hb, nhb - 1), 0)),
        tok(D_MODEL),
        pl.BlockSpec((None, N_MOD, D_MODEL), lambda b, i: (row(b), 0, 0)),
        wspec((4, D_MODEL)),
        wspec((D_MODEL, D_MODEL)),
        wspec((MIX_D, MIX_D)),
        wspec((1, MIX_D)),
        wspec((D_MODEL, LANES)),
        wspec((D_MODEL, LANES)),
        wspec((1, LANES)),
    ]
    return pl.pallas_call(
        functools.partial(_outproj_kernel, tm=tm, seq_len=L, nt=nt),
        out_shape=[jax.ShapeDtypeStruct((B, L, D_MODEL), F32),
                   jax.ShapeDtypeStruct((B, L, D_MODEL), BF),
                   jax.ShapeDtypeStruct((B, L, LANES), F32)],
        grid=(B, nt),
        in_specs=in_specs,
        out_specs=[tok(D_MODEL), tok(D_MODEL), tok(LANES)],
        compiler_params=_params(("arbitrary", "arbitrary")),
        name="outproj",
    )(oa, ob, oc, u, u, u, x, mod_l, pw["norm_g"], pw["w_out"], pw["pool_bd"], pw["pool_scale"],
      pw["r_hi"], pw["r_lo"], pw["r_b"])


def _moe_kernel(h_ref, cb_ref, x_ref, mod_ref, ng_ref, wgu_ref, wd_ref, o_ref, acc_ref):
    e = pl.program_id(1)

    @pl.when(e == 0)
    def _():
        acc_ref[...] = jnp.zeros_like(acc_ref)

    gu = _dot(h_ref[...], wgu_ref[...])
    g = gu[:, :D_EXPERT]
    act = (g * jax.nn.sigmoid(g)) * gu[:, D_EXPERT:]
    y = _dot(act.astype(BF), wd_ref[...])
    lane = lax.broadcasted_iota(jnp.int32, (1, LANES), 1)
    w = jnp.sum(jnp.where(lane == e, cb_ref[...], 0.0), axis=-1, keepdims=True)
    acc_ref[...] += w * y

    @pl.when(e == N_EXPERTS - 1)
    def _():
        o_ref[...] = x_ref[...] + mod_ref[5:6, :] * _rms(acc_ref[...], ng_ref[3:4, :])


def _moe(layer, h, cb, x, mod_l, mod_row, tm, pw):
    B, L, _ = x.shape
    T = B * L
    tiles_per_seq = L // tm
    row = (lambda i: i // tiles_per_seq) if mod_row is None else (lambda i: mod_row)
    tok = lambda w: pl.BlockSpec((tm, w), lambda i, e: (i, 0))
    out = pl.pallas_call(
        _moe_kernel,
        out_shape=jax.ShapeDtypeStruct((T, D_MODEL), F32),
        grid=(T // tm, N_EXPERTS),
        in_specs=[
            tok(D_MODEL), tok(LANES), tok(D_MODEL),
            pl.BlockSpec((None, N_MOD, D_MODEL), lambda i, e: (row(i), 0, 0)),
            pl.BlockSpec((None, 4, D_MODEL), lambda i, e: (layer, 0, 0)),
            pl.BlockSpec((None, None, D_MODEL, 2 * D_EXPERT), lambda i, e: (layer, e, 0, 0)),
            pl.BlockSpec((None, None, D_EXPERT, D_MODEL), lambda i, e: (layer, e, 0, 0)),
        ],
        out_specs=tok(D_MODEL),
        scratch_shapes=[pltpu.VMEM((tm, D_MODEL), F32)],
        compiler_params=_params(("arbitrary", "arbitrary")),
        name="moe",
    )(h.reshape(T, D_MODEL), cb.reshape(T, LANES), x.reshape(T, D_MODEL), mod_l,
      pw["norm_g"], pw["w_gu"], pw["w_down"])
    return out.reshape(B, L, D_MODEL)


def _axial(length, dim):
    rows = length // GRID_W
    row = jnp.repeat(jnp.arange(rows), GRID_W).astype(F32)
    col = jnp.tile(jnp.arange(GRID_W), rows).astype(F32)
    n = dim // 4
    inv = ROPE_BASE ** (-jnp.arange(n, dtype=F32) / n)
    ang = jnp.concatenate([row[:, None] * inv, col[:, None] * inv], axis=-1)
    return jnp.cos(ang), jnp.sin(ang)


def _rope_tables(length, rotate):
    if rotate:
        cos_a, sin_a = _axial(length, MLA_ROPE)
        cos_b, sin_b = _axial(length, SWA_HEAD_DIM)
    else:
        cos_a, sin_a = jnp.ones((length, MLA_ROPE // 2), F32), jnp.zeros((length, MLA_ROPE // 2), F32)
        cos_b, sin_b = jnp.ones((length, SWA_HEAD_DIM // 2), F32), jnp.zeros((length, SWA_HEAD_DIM // 2), F32)
    one = jnp.ones((length, MLA_NOPE), F32)
    zero = jnp.zeros((length, MLA_NOPE), F32)
    sa, sb, sc = MLA_SCALE * LOG2E, SWA_SCALE * LOG2E, DIFF_SCALE * LOG2E
    c_qa = jnp.tile(jnp.concatenate([one, cos_a, cos_a], -1), (1, MLA_HEADS)) * sa
    s_qa = jnp.tile(jnp.concatenate([zero, -sin_a, sin_a], -1), (1, MLA_HEADS)) * sa
    c_b = jnp.concatenate([cos_b, cos_b], -1)
    s_b = jnp.concatenate([-sin_b, sin_b], -1)
    c_c = jnp.tile(jnp.concatenate([cos_a, cos_a], -1), (1, 2 * DIFF_HEADS)) * sc
    s_c = jnp.tile(jnp.concatenate([-sin_a, sin_a], -1), (1, 2 * DIFF_HEADS)) * sc
    tab_c = jnp.concatenate([c_qa, jnp.tile(c_b, (1, SWA_HEADS)) * sb, jnp.tile(c_b, (1, SWA_KV_HEADS)), c_c], -1)
    tab_s = jnp.concatenate([s_qa, jnp.tile(s_b, (1, SWA_HEADS)) * sb, jnp.tile(s_b, (1, SWA_KV_HEADS)), s_c], -1)
    m_qa = jnp.tile(jnp.concatenate([jnp.zeros((MLA_NOPE,)), jnp.ones((MLA_ROPE // 2,)),
                                     jnp.zeros((MLA_ROPE // 2,))]), MLA_HEADS)
    m_b = jnp.concatenate([jnp.ones((SWA_HEAD_DIM // 2,)), jnp.zeros((SWA_HEAD_DIM // 2,))])
    m_c = jnp.concatenate([jnp.ones((DIFF_QK // 2,)), jnp.zeros((DIFF_QK // 2,))])
    msk = jnp.concatenate([m_qa, jnp.tile(m_b, SWA_HEADS + SWA_KV_HEADS), jnp.tile(m_c, 2 * DIFF_HEADS)])
    return tab_c, tab_s, msk.astype(F32)[None, :], cos_a.T, sin_a.T


def _prep_weights(norm_g, w_in, mla_q_gain, mla_kv_gain, mla_w_uq, mla_w_ukv, diff_subln, pool_w,
                  pool_scale, w_out, router_gw, router_gb, router_ew, router_eb, exp_w_gu, exp_w_down):
    sizes = (MLA_Q_RANK, MLA_KV_RANK, MLA_ROPE, 256, 128, 128, 256, 256, 256, MIX_D)
    offs = [0]
    for s in sizes:
        offs.append(offs[-1] + s)
    cq, ckv, kr, bq, bk, bv, dq, dk, dv, u = [w_in[:, :, offs[j]:offs[j + 1]] for j in range(len(sizes))]
    w_r = jnp.concatenate([cq, bq, dq, dv, u, ckv, bk, bv], axis=-1).astype(BF)
    w_kt = jnp.swapaxes(jnp.concatenate([ckv, kr, dk], axis=-1), 1, 2).astype(BF)
    ukv = mla_w_ukv.reshape(DEPTH, MLA_KV_RANK, MLA_HEADS, MLA_NOPE + MLA_V)
    w_uk = ukv[..., :MLA_NOPE].reshape(DEPTH, MLA_KV_RANK, MLA_HEADS * MLA_NOPE)
    w_uv = ukv[..., MLA_NOPE:].reshape(DEPTH, MLA_KV_RANK, MLA_HEADS * MLA_V)
    eye = jnp.eye(len(POOL_WINDOWS), dtype=F32)
    pool_bd = jnp.einsum('lgcd,gh->lgchd', pool_w, eye).reshape(DEPTH, MIX_D, MIX_D)
    r_w = jnp.concatenate([router_ew, router_gw], axis=-1)
    r_w = jnp.pad(r_w, ((0, 0), (0, 0), (0, LANES - r_w.shape[-1])))
    r_hi = r_w.astype(BF)
    r_lo = (r_w - r_hi.astype(F32)).astype(BF)
    r_b = jnp.concatenate([router_eb, router_gb], axis=-1)
    r_b = jnp.pad(r_b, ((0, 0), (0, LANES - r_b.shape[-1])))[:, None, :]
    return dict(
        norm_g=norm_g, w_r=w_r, w_kt=w_kt,
        q_gain=mla_q_gain[:, None, :], kv_gain_r=mla_kv_gain[:, None, :], kv_gain_c=mla_kv_gain[:, :, None],
        w_uq=mla_w_uq.astype(BF), w_ukt=jnp.swapaxes(w_uk, 1, 2).astype(BF), w_uv=w_uv.astype(BF),
        subln=jnp.tile(diff_subln, (1, LANES // DIFF_V))[:, None, :],
        pool_bd=pool_bd.astype(BF), pool_scale=pool_scale[:, None, :], w_out=w_out.astype(BF),
        r_hi=r_hi, r_lo=r_lo, r_b=r_b, w_gu=exp_w_gu.astype(BF), w_down=exp_w_down.astype(BF))


def kernel(x, c, ctx, c_ctx, ada_w, ada_b, norm_g, w_in, mla_q_gain, mla_kv_gain, mla_w_uq, mla_w_ukv,
           swa_sink, diff_lambda, diff_subln, pool_w, pool_scale, w_out, router_gw, router_gb,
           router_ew, router_eb, exp_w_gu, exp_w_down):
    B, L, D = x.shape
    C = ctx.shape[1]
    assert D == D_MODEL and B + 1 <= MOD_ROWS and L % 512 == 0 and C % 256 == 0
    pw = _prep_weights(norm_g, w_in, mla_q_gain, mla_kv_gain, mla_w_uq, mla_w_ukv, diff_subln, pool_w,
                       pool_scale, w_out, router_gw, router_gb, router_ew, router_eb, exp_w_gu, exp_w_down)
    cc = jnp.concatenate([c, c_ctx[None, :], jnp.zeros((MOD_ROWS - B - 1, D), F32)], axis=0)
    mod = jnp.swapaxes(_adaln(cc, ada_w, ada_b), 1, 2)
    tabs_x = _rope_tables(L, True)
    tabs_c = _rope_tables(C, False)
    tm_x, tm_c, tq = 512, 256, 256

    h_ctx = ctx
    for layer in range(DEPTH):
        update_ctx = layer < DEPTH - 1
        lam_init = 0.8 - 0.6 * math.exp(-0.3 * layer)
        mod_l = mod[layer]
        qa, qb, qc, u, va, vb, vc, kb, kta, ktc = _inproj(layer, x, mod_l, None, tm_x, pw, tabs_x)
        cqa, cqb, cqc, cu, cva, cvb, cvc, ckb, ckta, cktc = _inproj(layer, h_ctx, mod_l, B, tm_c, pw, tabs_c)
        diff_extra = (diff_lambda, pw["subln"])

        oa = _dense_attn("mla", layer, qa, [kta, ckta], [va, cva], tq)
        ob = _swa(layer, swa_sink, qb, kb, vb, ckb, cvb, tq)
        oc = _dense_attn("diff", layer, qc, [ktc, cktc], [vc, cvc], tq, diff_extra, lam_init)
        x, hx, cbx = _outproj(layer, oa, ob, oc, u, x, mod_l, None, tm_x, pw)
        if update_ctx:
            coa = _dense_attn("mla", layer, cqa, [ckta], [cva], tq)
            cob = _swa(layer, swa_sink, cqb, None, None, ckb, cvb, tq)
            coc = _dense_attn("diff", layer, cqc, [cktc], [cvc], tq, diff_extra, lam_init)
            h_ctx, hc, cbc = _outproj(layer, coa, cob, coc, cu, h_ctx, mod_l, B, tm_c, pw)
        x = _moe(layer, hx, cbx, x, mod_l, None, 1024, pw)
        if update_ctx:
            h_ctx = _moe(layer, hc, cbc, h_ctx, mod_l, B, 256, pw)
    return x
```

```python
import functools
import math

import jax
import jax.numpy as jnp
from jax import lax
from jax.experimental import pallas as pl
from jax.experimental.pallas import tpu as pltpu

F32 = jnp.float32
BF = jnp.bfloat16

D_MODEL = 1024
DEPTH = 2
GRID_W = 64
EPS = 1e-6
ROPE_BASE = 10000.0
N_MOD = 6

MLA_HEADS = 4
MLA_Q_RANK = 256
MLA_KV_RANK = 128
MLA_NOPE = 64
MLA_ROPE = 32
MLA_V = 64
MLA_QK = MLA_NOPE + MLA_ROPE
MLA_SCALE = MLA_QK ** -0.5

SWA_HEADS = 4
SWA_KV_HEADS = 2
SWA_HEAD_DIM = 64
SWA_WINDOW = 128
SWA_SCALE = SWA_HEAD_DIM ** -0.5

DIFF_HEADS = 4
DIFF_QK = 32
DIFF_V = 64
DIFF_SCALE = DIFF_QK ** -0.5

POOL_WINDOWS = (2, 4, 8, 16)
POOL_GROUP = 64
POOL_HALO = 8
MIX_D = len(POOL_WINDOWS) * POOL_GROUP

N_EXPERT_GROUPS = 4
EXPERTS_PER_GROUP = 4
N_EXPERTS = 16
D_EXPERT = 512
PAIR_A = (0, 2, 2, 3, 3, 3)
PAIR_B = (1, 1, 0, 0, 1, 2)
PAIRS_PER_GROUP = len(PAIR_A)
N_BUCKETS = N_EXPERT_GROUPS * PAIRS_PER_GROUP

LOG2E = math.log2(math.e)
NEG = -1e30
LANES = 128
MOD_ROWS = 24
ROW_W = D_MODEL + LANES
MOE_TM = 256
VMEM_LIMIT = 48 * 1024 * 1024

R_CQ, R_BQ, R_DQ, R_DV, R_U, R_CKV, R_BK, R_BV, R_END = 0, 256, 512, 768, 1024, 1280, 1408, 1536, 1664
T_CKV, T_KR, T_DK, T_END = 0, 128, 160, 416
P_QA, P_QB, P_KB, P_QC, P_END = 0, 384, 640, 768, 1024


def _dot(a, b):
    return jnp.dot(a, b, preferred_element_type=F32)


def _dot_nt(a, b):
    return lax.dot_general(a, b, (((1,), (1,)), ((), ())), preferred_element_type=F32)


def _rms(x, g):
    return x * lax.rsqrt(jnp.mean(x * x, axis=-1, keepdims=True) + EPS) * g


def _params(sem):
    return pltpu.CompilerParams(dimension_semantics=sem, vmem_limit_bytes=VMEM_LIMIT)


def _adaln_kernel(c_ref, w_ref, b_ref, o_ref):
    c = c_ref[...]
    s = (c * jax.nn.sigmoid(c)).astype(BF)
    o_ref[...] = _dot(s, w_ref[...].astype(BF)) + b_ref[...]


def _adaln(cc, ada_w, ada_b):
    return pl.pallas_call(
        _adaln_kernel,
        out_shape=jax.ShapeDtypeStruct((DEPTH, N_MOD, MOD_ROWS, D_MODEL), F32),
        grid=(DEPTH, N_MOD),
        in_specs=[
            pl.BlockSpec((MOD_ROWS, D_MODEL), lambda l, j: (0, 0)),
            pl.BlockSpec((None, D_MODEL, D_MODEL), lambda l, j: (l, 0, j)),
            pl.BlockSpec((None, None, 1, D_MODEL), lambda l, j: (l, j, 0, 0)),
        ],
        out_specs=pl.BlockSpec((None, None, MOD_ROWS, D_MODEL), lambda l, j: (l, j, 0, 0)),
        compiler_params=_params(("arbitrary", "arbitrary")),
        name="adaln",
    )(cc, ada_w, ada_b.reshape(DEPTH, N_MOD, 1, D_MODEL))


def _rope_rows(xf, c, s, m1, half):
    w = xf.shape[-1]
    up = pltpu.roll(xf, w - half, axis=1)
    dn = pltpu.roll(xf, half, axis=1)
    return xf * c + jnp.where(m1 > 0.5, up, dn) * s


def _rope_cols(xt, c, s):
    h = xt.shape[0] // 2
    x1, x2 = xt[:h], xt[h:]
    return jnp.concatenate([x1 * c - x2 * s, x1 * s + x2 * c], axis=0)


def _inproj_kernel(x_ref, mod_ref, ng_ref, wr_ref, wkt_ref, qg_ref, kvgr_ref, kvgc_ref,
                   wuq_ref, wukt_ref, wuv_ref, tc_ref, ts_ref, m_ref, ct_ref, st_ref,
                   qa_ref, qb_ref, qc_ref, u_ref, va_ref, vb_ref, vc_ref, kb_ref, kta_ref, ktc_ref):
    x = x_ref[...]
    mod = mod_ref[...]
    hx = _rms(x, ng_ref[0:1, :]) * (1.0 + mod[1:2, :]) + mod[0:1, :]
    hb = hx.astype(BF)
    zr = _dot(hb, wr_ref[...])
    zk = _dot_nt(wkt_ref[...], hb)

    tc = tc_ref[...]
    ts = ts_ref[...]
    m1 = m_ref[...]

    cqn = _rms(zr[:, R_CQ:R_CQ + MLA_Q_RANK], qg_ref[...]).astype(BF)
    qa = _dot(cqn, wuq_ref[...])
    qa_ref[...] = _rope_rows(qa, tc[:, P_QA:P_QB], ts[:, P_QA:P_QB], m1[:, P_QA:P_QB],
                             MLA_ROPE // 2).astype(BF)
    qb_ref[...] = _rope_rows(zr[:, R_BQ:R_DQ], tc[:, P_QB:P_KB], ts[:, P_QB:P_KB], m1[:, P_QB:P_KB],
                             SWA_HEAD_DIM // 2).astype(BF)
    qc_ref[...] = _rope_rows(zr[:, R_DQ:R_DV], tc[:, P_QC:P_END], ts[:, P_QC:P_END], m1[:, P_QC:P_END],
                             DIFF_QK // 2).astype(BF)
    kb_ref[...] = _rope_rows(zr[:, R_BK:R_BV], tc[:, P_KB:P_QC], ts[:, P_KB:P_QC], m1[:, P_KB:P_QC],
                             SWA_HEAD_DIM // 2).astype(BF)
    u_ref[...] = zr[:, R_U:R_CKV]
    vb_ref[...] = zr[:, R_BV:R_END].astype(BF)
    vc_ref[...] = zr[:, R_DV:R_U].astype(BF)

    ckvn = _rms(zr[:, R_CKV:R_BK], kvgr_ref[...]).astype(BF)
    va_ref[...] = _dot(ckvn, wuv_ref[...]).astype(BF)

    ckvt = zk[T_CKV:T_KR]
    ckvtn = ckvt * lax.rsqrt(jnp.mean(ckvt * ckvt, axis=0, keepdims=True) + EPS) * kvgc_ref[...]
    knt = _dot(wukt_ref[...], ckvtn.astype(BF))
    ct = ct_ref[...]
    st = st_ref[...]
    krt = _rope_cols(zk[T_KR:T_DK], ct, st)
    pieces = []
    for h in range(MLA_HEADS):
        pieces.append(knt[h * MLA_NOPE:(h + 1) * MLA_NOPE])
        pieces.append(krt)
    kta_ref[...] = jnp.concatenate(pieces, axis=0).astype(BF)

    dk = zk[T_DK:T_END]
    ktc_ref[...] = jnp.concatenate(
        [_rope_cols(dk[j * DIFF_QK:(j + 1) * DIFF_QK], ct, st) for j in range(2 * DIFF_HEADS)],
        axis=0).astype(BF)


def _inproj(layer, x, mod_l, mod_row, tm, pw, tabs):
    B, L, _ = x.shape
    nt = L // tm
    tab_c, tab_s, msk, cos_t, sin_t = tabs
    row = (lambda b: b) if mod_row is None else (lambda b: mod_row)

    def wspec(shape):
        nd = len(shape)
        return pl.BlockSpec((None,) + shape, lambda i, b: (layer,) + (0,) * nd)

    tok = lambda w: pl.BlockSpec((None, tm, w), lambda i, b: (b, i, 0))
    tokt = lambda w: pl.BlockSpec((None, w, tm), lambda i, b: (b, 0, i))
    in_specs = [
        tok(D_MODEL),
        pl.BlockSpec((None, N_MOD, D_MODEL), lambda i, b: (row(b), 0, 0)),
        wspec((4, D_MODEL)),
        wspec((D_MODEL, R_END)),
        wspec((T_END, D_MODEL)),
        wspec((1, MLA_Q_RANK)),
        wspec((1, MLA_KV_RANK)),
        wspec((MLA_KV_RANK, 1)),
        wspec((MLA_Q_RANK, MLA_HEADS * MLA_QK)),
        wspec((MLA_HEADS * MLA_NOPE, MLA_KV_RANK)),
        wspec((MLA_KV_RANK, MLA_HEADS * MLA_V)),
        pl.BlockSpec((tm, P_END), lambda i, b: (i, 0)),
        pl.BlockSpec((tm, P_END), lambda i, b: (i, 0)),
        pl.BlockSpec((1, P_END), lambda i, b: (0, 0)),
        pl.BlockSpec((MLA_ROPE // 2, tm), lambda i, b: (0, i)),
        pl.BlockSpec((MLA_ROPE // 2, tm), lambda i, b: (0, i)),
    ]
    widths = [(MLA_HEADS * MLA_QK, BF), (256, BF), (256, BF), (MIX_D, F32),
              (256, BF), (128, BF), (256, BF), (128, BF)]
    out_shape = [jax.ShapeDtypeStruct((B, L, w), dt) for w, dt in widths]
    out_specs = [tok(w) for w, _ in widths]
    out_shape += [jax.ShapeDtypeStruct((B, MLA_HEADS * MLA_QK, L), BF),
                  jax.ShapeDtypeStruct((B, 2 * DIFF_HEADS * DIFF_QK, L), BF)]
    out_specs += [tokt(MLA_HEADS * MLA_QK), tokt(2 * DIFF_HEADS * DIFF_QK)]
    return pl.pallas_call(
        _inproj_kernel,
        out_shape=out_shape,
        grid=(nt, B),
        in_specs=in_specs,
        out_specs=out_specs,
        compiler_params=_params(("arbitrary", "arbitrary")),
        name="inproj",
    )(x, mod_l, pw["norm_g"], pw["w_r"], pw["w_kt"], pw["q_gain"], pw["kv_gain_r"], pw["kv_gain_c"],
      pw["w_uq"], pw["w_ukt"], pw["w_uv"], tab_c, tab_s, msk, cos_t, sin_t)


def _softmax_pv(q, segs, extra=None):
    ss = []
    m = None
    for kt, _, mask in segs:
        s = _dot(q, kt)
        if mask is not None:
            s = jnp.where(mask, s, NEG)
        ss.append(s)
        sm = jnp.max(s, axis=-1, keepdims=True)
        m = sm if m is None else jnp.maximum(m, sm)
    if extra is not None:
        m = jnp.maximum(m, extra)
    l = None
    r = None
    for s, (_, v, _) in zip(ss, segs):
        e = jnp.exp2(s - m)
        ls = jnp.sum(e, axis=-1, keepdims=True)
        rs = _dot(e.astype(BF), v)
        l = ls if l is None else l + ls
        r = rs if r is None else r + rs
    if extra is not None:
        l = l + jnp.exp2(extra - m)
    return r / l


def _mla_kernel(*refs, nseg):
    q_ref = refs[0]
    kt_refs = refs[1:1 + nseg]
    v_refs = refs[1 + nseg:1 + 2 * nseg]
    o_ref = refs[1 + 2 * nseg]
    q = q_ref[...]
    lane = lax.broadcasted_iota(jnp.int32, (1, LANES), 1)
    blocks = []
    for p in range(MLA_HEADS // 2):
        pair = []
        for h in (2 * p, 2 * p + 1):
            qh = q[:, h * MLA_QK:(h + 1) * MLA_QK]
            segs = [(kt[h * MLA_QK:(h + 1) * MLA_QK, :], v[:, p * LANES:(p + 1) * LANES], None)
                    for kt, v in zip(kt_refs, v_refs)]
            pair.append(_softmax_pv(qh, segs))
        blocks.append(jnp.where(lane < MLA_V, pair[0], pair[1]))
    o_ref[...] = jnp.concatenate(blocks, axis=-1).astype(BF)


def _diff_kernel(*refs, nseg, lam_init):
    q_ref = refs[0]
    kt_refs = refs[1:1 + nseg]
    v_refs = refs[1 + nseg:1 + 2 * nseg]
    dl_ref, g_ref, o_ref = refs[1 + 2 * nseg:]
    q = q_ref[...]
    dl = dl_ref[...]
    lam = (jnp.exp(jnp.sum(dl[0:1] * dl[1:2], axis=-1, keepdims=True))
           - jnp.exp(jnp.sum(dl[2:3] * dl[3:4], axis=-1, keepdims=True)) + lam_init)
    lane = lax.broadcasted_iota(jnp.int32, (1, LANES), 1)
    lo = lane < DIFF_V
    blocks = []
    for p in range(DIFF_HEADS // 2):
        pair = []
        for h in (2 * p, 2 * p + 1):
            comp = []
            for c in range(2):
                j = 2 * h + c
                qj = q[:, j * DIFF_QK:(j + 1) * DIFF_QK]
                segs = [(kt[j * DIFF_QK:(j + 1) * DIFF_QK, :], v[:, p * LANES:(p + 1) * LANES], None)
                        for kt, v in zip(kt_refs, v_refs)]
                comp.append(_softmax_pv(qj, segs))
            pair.append(comp[0] - lam * comp[1])
        o = jnp.where(lo, pair[0], pair[1])
        o2 = o * o
        ms_lo = jnp.sum(jnp.where(lo, o2, 0.0), axis=-1, keepdims=True) * (1.0 / DIFF_V)
        ms_hi = jnp.sum(jnp.where(lo, 0.0, o2), axis=-1, keepdims=True) * (1.0 / DIFF_V)
        ms = jnp.where(lo, ms_lo, ms_hi)
        blocks.append(o * lax.rsqrt(ms + EPS) * g_ref[...] * (1.0 - lam_init))
    o_ref[...] = jnp.concatenate(blocks, axis=-1).astype(BF)


def _dense_attn(kind, layer, q, kts, vs, tq, extra_in=(), lam_init=0.0):
    B, L, wq = q.shape
    nseg = len(kts)
    in_specs = [pl.BlockSpec((None, tq, wq), lambda b, i: (b, i, 0))]
    for kt in kts:
        in_specs.append(pl.BlockSpec((None,) + kt.shape[1:], lambda b, i: (b, 0, 0)))
    for v in vs:
        in_specs.append(pl.BlockSpec((None,) + v.shape[1:], lambda b, i: (b, 0, 0)))
    if kind == "mla":
        body = functools.partial(_mla_kernel, nseg=nseg)
    else:
        body = functools.partial(_diff_kernel, nseg=nseg, lam_init=lam_init)
        dl, g = extra_in
        in_specs.append(pl.BlockSpec((None, 4, DIFF_QK), lambda b, i: (layer, 0, 0)))
        in_specs.append(pl.BlockSpec((None, 1, LANES), lambda b, i: (layer, 0, 0)))
    return pl.pallas_call(
        body,
        out_shape=jax.ShapeDtypeStruct((B, L, 256), BF),
        grid=(B, L // tq),
        in_specs=in_specs,
        out_specs=pl.BlockSpec((None, tq, 256), lambda b, i: (b, i, 0)),
        compiler_params=_params(("arbitrary", "arbitrary")),
        name=kind,
    )(q, *kts, *vs, *extra_in)


def _swa_kernel(*refs, layer, tq, seq_len, local):
    if local:
        sink_ref, q_ref, kl_ref, vl_ref, kc_ref, vc_ref, o_ref = refs
    else:
        sink_ref, q_ref, kc_ref, vc_ref, o_ref = refs
    i = pl.program_id(1)
    q = q_ref[...]
    lane = lax.broadcasted_iota(jnp.int32, (1, LANES), 1)
    rows = lax.broadcasted_iota(jnp.int32, (2 * tq, 1), 0)
    top = rows < tq
    if local:
        win = tq + 2 * SWA_WINDOW
        q0 = i * tq
        ws = pl.multiple_of(jnp.clip(q0 - SWA_WINDOW, 0, seq_len - win), LANES)
        kwin = kl_ref[pl.ds(ws, win), :]
        vwin = vl_ref[pl.ds(ws, win), :]
        qpos = q0 + jnp.where(top, rows, rows - tq)
        kpos = ws + lax.broadcasted_iota(jnp.int32, (1, win), 1)
        band = jnp.abs(kpos - qpos) <= SWA_WINDOW
    blocks = []
    for k in range(SWA_KV_HEADS):
        q2 = jnp.concatenate([q[:, (2 * k) * SWA_HEAD_DIM:(2 * k + 1) * SWA_HEAD_DIM],
                              q[:, (2 * k + 1) * SWA_HEAD_DIM:(2 * k + 2) * SWA_HEAD_DIM]], axis=0)
        sl = slice(k * SWA_HEAD_DIM, (k + 1) * SWA_HEAD_DIM)
        sk = jnp.where(top, sink_ref[layer, 2 * k], sink_ref[layer, 2 * k + 1]) * LOG2E
        ss = []
        if local:
            ss.append((jnp.where(band, _dot_nt(q2, kwin[:, sl]), NEG), vwin))
        ss.append((_dot_nt(q2, kc_ref[:, sl]), vc_ref[...]))
        m = sk
        for s, _ in ss:
            m = jnp.maximum(m, jnp.max(s, axis=-1, keepdims=True))
        l = jnp.exp2(sk - m)
        r = None
        for s, v in ss:
            e = jnp.exp2(s - m)
            l = l + jnp.sum(e, axis=-1, keepdims=True)
            rs = _dot(e.astype(BF), v)
            r = rs if r is None else r + rs
        r = r / l
        r_top, r_bot = r[:tq], r[tq:]
        if k == 0:
            r_bot = pltpu.roll(r_bot, SWA_HEAD_DIM, axis=1)
        else:
            r_top = pltpu.roll(r_top, SWA_HEAD_DIM, axis=1)
        blocks.append(jnp.where(lane < SWA_HEAD_DIM, r_top, r_bot))
    o_ref[...] = jnp.concatenate(blocks, axis=-1).astype(BF)


def _swa(layer, sink, q, k_loc, v_loc, k_ctx, v_ctx, tq):
    B, L, _ = q.shape
    C = k_ctx.shape[1]
    local = k_loc is not None
    full = lambda n: pl.BlockSpec((None, n, 128), lambda b, i: (b, 0, 0))
    in_specs = [pl.BlockSpec(memory_space=pltpu.SMEM),
                pl.BlockSpec((None, tq, 256), lambda b, i: (b, i, 0))]
    args = [sink, q]
    if local:
        in_specs += [full(L), full(L)]
        args += [k_loc, v_loc]
    in_specs += [full(C), full(C)]
    args += [k_ctx, v_ctx]
    return pl.pallas_call(
        functools.partial(_swa_kernel, layer=layer, tq=tq, seq_len=L, local=local),
        out_shape=jax.ShapeDtypeStruct((B, L, 256), BF),
        grid=(B, L // tq),
        in_specs=in_specs,
        out_specs=pl.BlockSpec((None, tq, 256), lambda b, i: (b, i, 0)),
        compiler_params=_params(("arbitrary", "arbitrary")),
        name="swa",
    )(*args)


def _pool_diffs(u, prev, nxt, i, nt, tm, seq_len):
    n = tm + 2 * POOL_HALO
    ue = jnp.concatenate([jnp.where(i > 0, prev, 0.0), u, jnp.where(i < nt - 1, nxt, 0.0)], axis=0)

    def sh(a, k):
        return pltpu.roll(a, k % n, axis=0)

    s2 = ue + sh(ue, 1)
    s4 = sh(s2, 1) + sh(s2, -1)
    s8 = sh(s4, 2) + sh(s4, -2)
    s16 = sh(s8, 4) + sh(s8, -4)
    lane = lax.broadcasted_iota(jnp.int32, (1, MIX_D), 1)
    t = i * tm + lax.broadcasted_iota(jnp.int32, (tm, 1), 0)
    win = None
    cnt = None
    for gi, (w, s) in enumerate(zip(POOL_WINDOWS, (s2, s4, s8, s16))):
        half = w // 2
        c = (jnp.minimum(t - half + w, seq_len) - jnp.maximum(t - half, 0)).astype(F32)
        sv = s[POOL_HALO:POOL_HALO + tm]
        if win is None:
            win, cnt = sv, c
        else:
            sel = lane >= gi * POOL_GROUP
            win = jnp.where(sel, sv, win)
            cnt = jnp.where(sel, c, cnt)
    return win / cnt - u


def _outproj_kernel(oa_ref, ob_ref, oc_ref, u_ref, up_ref, un_ref, x_ref, mod_ref, ng_ref,
                    wo_ref, pw_ref, ps_ref, rwh_ref, rwl_ref, rb_ref,
                    xo_ref, hr_ref, cnt_ref, carry_ref, *, tm, seq_len, nt):
    i = pl.program_id(1)

    @pl.when((pl.program_id(0) == 0) & (i == 0))
    def _():
        carry_ref[...] = jnp.zeros_like(carry_ref)

    u = u_ref[...]
    dg = _pool_diffs(u, up_ref[...], un_ref[...], i, nt, tm, seq_len)
    od = _dot(dg.astype(BF), pw_ref[...]) * ps_ref[...]
    mix = jnp.concatenate([oa_ref[...], ob_ref[...], oc_ref[...], od.astype(BF)], axis=-1)
    ox = _dot(mix, wo_ref[...])
    mod = mod_ref[...]
    x = x_ref[...] + mod[2:3, :] * _rms(ox, ng_ref[1:2, :])
    xo_ref[...] = x
    h = _rms(x, ng_ref[2:3, :]) * (1.0 + mod[4:5, :]) + mod[3:4, :]
    hr_ref[:, :D_MODEL] = h
    hh = h.astype(BF)
    hl = (h - hh.astype(F32)).astype(BF)
    wh = rwh_ref[...]
    logits = _dot(hh, wh) + _dot(hl, wh) + _dot(hh, rwl_ref[...]) + rb_ref[...]

    lane = lax.broadcasted_iota(jnp.int32, (1, LANES), 1)
    lanef = lane.astype(F32)
    is_g = (lane >= N_EXPERTS) & (lane < N_EXPERTS + N_EXPERT_GROUPS)
    gl = jnp.where(is_g, logits, NEG)
    gmax = jnp.max(gl, axis=-1, keepdims=True)
    gsel = jnp.min(jnp.where(gl == gmax, lanef, 1e9), axis=-1, keepdims=True) - N_EXPERTS
    p_g = 1.0 / jnp.sum(jnp.where(is_g, jnp.exp(gl - gmax), 0.0), axis=-1, keepdims=True)
    grp = jnp.right_shift(lane, 2).astype(F32)
    el = jnp.where((lane < N_EXPERTS) & (grp == gsel), logits, NEG)
    v1 = jnp.max(el, axis=-1, keepdims=True)
    i1 = jnp.min(jnp.where(el == v1, lanef, 1e9), axis=-1, keepdims=True)
    el2 = jnp.where(lanef == i1, NEG, el)
    v2 = jnp.max(el2, axis=-1, keepdims=True)
    i2 = jnp.min(jnp.where(el2 == v2, lanef, 1e9), axis=-1, keepdims=True)
    e21 = jnp.exp(v2 - v1)
    w1 = p_g / (1.0 + e21)
    w2 = w1 * e21

    l1 = i1 - EXPERTS_PER_GROUP * gsel
    l2 = i2 - EXPERTS_PER_GROUP * gsel
    code = jnp.minimum(l1, l2) * EXPERTS_PER_GROUP + jnp.maximum(l1, l2)
    pidx = jnp.full_like(code, float(PAIRS_PER_GROUP - 1))
    slot_a = jnp.full_like(code, float(PAIR_A[-1]))
    for p in range(PAIRS_PER_GROUP - 1):
        lo_e, hi_e = min(PAIR_A[p], PAIR_B[p]), max(PAIR_A[p], PAIR_B[p])
        hit = code == float(lo_e * EXPERTS_PER_GROUP + hi_e)
        pidx = jnp.where(hit, float(p), pidx)
        slot_a = jnp.where(hit, float(PAIR_A[p]), slot_a)
    first_is_a = l1 == slot_a
    w_a = jnp.where(first_is_a, w1, w2)
    w_b = jnp.where(first_is_a, w2, w1)
    bucket = gsel * PAIRS_PER_GROUP + pidx

    onehot = lanef == bucket
    rr = lax.broadcasted_iota(jnp.int32, (tm, tm), 0)
    cc = lax.broadcasted_iota(jnp.int32, (tm, tm), 1)
    tri = jnp.where(rr >= cc, 1.0, 0.0).astype(BF)
    prefix = _dot(tri, jnp.where(onehot, 1.0, 0.0).astype(BF))
    carry = carry_ref[...]
    rank = jnp.sum(jnp.where(onehot, prefix - 1.0 + carry, 0.0), axis=-1, keepdims=True)
    carry = carry + prefix[tm - 1:tm, :]
    carry_ref[...] = carry
    cnt_ref[...] = carry
    hr_ref[:, D_MODEL:] = jnp.where(lane == 0, w_a, jnp.where(lane == 1, w_b, jnp.where(
        lane == 2, bucket, jnp.where(lane == 3, rank, 0.0))))


def _outproj(layer, oa, ob, oc, u, x, mod_l, mod_row, tm, pw):
    B, L, _ = x.shape
    nt = L // tm
    hb = tm // POOL_HALO
    nhb = L // POOL_HALO
    row = (lambda b: b) if mod_row is None else (lambda b: mod_row)

    def wspec(shape):
        nd = len(shape)
        return pl.BlockSpec((None,) + shape, lambda b, i: (layer,) + (0,) * nd)

    tok = lambda w: pl.BlockSpec((None, tm, w), lambda b, i: (b, i, 0))
    in_specs = [
        tok(256), tok(256), tok(256), tok(MIX_D),
        pl.BlockSpec((None, POOL_HALO, MIX_D), lambda b, i: (b, jnp.maximum(i * hb - 1, 0), 0)),
        pl.BlockSpec((None, POOL_HALO, MIX_D), lambda b, i: (b, jnp.minimum((i + 1) * hb, nhb - 1), 0)),
        tok(D_MODEL),
        pl.BlockSpec((None, N_MOD, D_MODEL), lambda b, i: (row(b), 0, 0)),
        wspec((4, D_MODEL)),
        wspec((D_MODEL, D_MODEL)),
        wspec((MIX_D, MIX_D)),
        wspec((1, MIX_D)),
        wspec((D_MODEL, LANES)),
        wspec((D_MODEL, LANES)),
        wspec((1, LANES)),
    ]
    return pl.pallas_call(
        functools.partial(_outproj_kernel, tm=tm, seq_len=L, nt=nt),
        out_shape=[jax.ShapeDtypeStruct((B, L, D_MODEL), F32),
                   jax.ShapeDtypeStruct((B, L, ROW_W), F32),
                   jax.ShapeDtypeStruct((1, LANES), F32)],
        grid=(B, nt),
        in_specs=in_specs,
        out_specs=[tok(D_MODEL), tok(ROW_W), pl.BlockSpec((1, LANES), lambda b, i: (0, 0))],
        scratch_shapes=[pltpu.VMEM((1, LANES), F32)],
        compiler_params=_params(("arbitrary", "arbitrary")),
        name="outproj",
    )(oa, ob, oc, u, u, u, x, mod_l, pw["norm_g"], pw["w_out"], pw["pool_bd"], pw["pool_scale"],
      pw["r_hi"], pw["r_lo"], pw["r_b"])


def _route_tables(hr, counts, n_tiles):
    T = hr.shape[0] * hr.shape[1]
    bucket = hr[:, :, D_MODEL + 2].reshape(T).astype(jnp.int32)
    rank = hr[:, :, D_MODEL + 3].reshape(T).astype(jnp.int32)
    cnt = counts[0, :N_BUCKETS].astype(jnp.int32)
    tiles_b = (cnt + MOE_TM - 1) // MOE_TM
    ends = jnp.cumsum(tiles_b)
    starts = ends - tiles_b
    pos = jnp.take(starts * MOE_TM, bucket) + rank
    n_used = ends[-1]
    tile = jnp.arange(n_tiles, dtype=jnp.int32)
    src = jnp.minimum(tile, n_used - 1)
    tb = jnp.minimum(jnp.searchsorted(ends, src, side='right'), N_BUCKETS - 1).astype(jnp.int32)
    grp, pidx = tb // PAIRS_PER_GROUP, tb % PAIRS_PER_GROUP
    ea = grp * EXPERTS_PER_GROUP + jnp.take(jnp.array(PAIR_A, jnp.int32), pidx)
    eb = grp * EXPERTS_PER_GROUP + jnp.take(jnp.array(PAIR_B, jnp.int32), pidx)
    zlast = jnp.where(tiles_b > 0, (ends - 1) * MOE_TM, -1)
    idle = n_used + jnp.arange(N_BUCKETS, dtype=jnp.int32)
    zidle = jnp.where(idle < n_tiles, idle * MOE_TM, -1)
    zrow = jnp.concatenate([zlast, zidle]).astype(jnp.int32)
    return pos, zrow, src, ea, eb, n_used.reshape(1).astype(jnp.int32)


def _row_copy(src_ref, src_row, dst_ref, dst_row, sem):
    return pltpu.make_async_copy(src_ref.at[pl.ds(src_row, 1)], dst_ref.at[pl.ds(dst_row, 1)], sem)


def _scatter_kernel(pos_ref, zrow_ref, hr_ref, hs_ref, zbuf_ref, sem, zsem, *, tm):
    i = pl.program_id(0)

    @pl.when(i == 0)
    def _():
        zbuf_ref[...] = jnp.zeros_like(zbuf_ref)
        for start in (True, False):
            for b in range(2 * N_BUCKETS):
                @pl.when(zrow_ref[b] >= 0)
                def _():
                    z0 = pl.multiple_of(zrow_ref[b], MOE_TM)
                    cp = pltpu.make_async_copy(zbuf_ref, hs_ref.at[pl.ds(z0, MOE_TM)], zsem)
                    cp.start() if start else cp.wait()

    def issue(r, carry):
        _row_copy(hr_ref, r, hs_ref, pos_ref[i * tm + r], sem).start()
        return carry

    lax.fori_loop(0, tm, issue, 0)

    def drain(r, carry):
        _row_copy(hr_ref, r, hs_ref, pos_ref[i * tm + r], sem).wait()
        return carry

    lax.fori_loop(0, tm, drain, 0)


def _scatter_rows(hr, pos, zrow, n_tiles, tm):
    T = hr.shape[0]
    return pl.pallas_call(
        functools.partial(_scatter_kernel, tm=tm),
        out_shape=jax.ShapeDtypeStruct((n_tiles * MOE_TM, ROW_W), F32),
        grid_spec=pltpu.PrefetchScalarGridSpec(
            num_scalar_prefetch=2,
            grid=(T // tm,),
            in_specs=[pl.BlockSpec((tm, ROW_W), lambda i, pos, zrow: (i, 0))],
            out_specs=pl.BlockSpec(memory_space=pl.ANY),
            scratch_shapes=[pltpu.VMEM((MOE_TM, ROW_W), F32),
                            pltpu.SemaphoreType.DMA(()), pltpu.SemaphoreType.DMA(())]),
        compiler_params=_params(("arbitrary",)),
        name="moe_scatter",
    )(pos, zrow, hr)


def _expert(h, wgu, wd):
    gu = _dot(h, wgu)
    g = gu[:, :D_EXPERT]
    act = (g * jax.nn.sigmoid(g)) * gu[:, D_EXPERT:]
    return _dot(act.astype(BF), wd)


def _moe_kernel(src_ref, ea_ref, eb_ref, nu_ref, hs_ref, wgua_ref, wgub_ref, wda_ref, wdb_ref, o_ref):
    j = pl.program_id(0)

    @pl.when(j < nu_ref[0])
    def _():
        h = hs_ref[:, :D_MODEL].astype(BF)
        w = hs_ref[:, D_MODEL:]
        o_ref[...] = (w[:, 0:1] * _expert(h, wgua_ref[...], wda_ref[...])
                      + w[:, 1:2] * _expert(h, wgub_ref[...], wdb_ref[...]))

    @pl.when(j >= nu_ref[0])
    def _():
        o_ref[...] = jnp.zeros_like(o_ref)


def _moe_sorted(layer, hs, src, ea, eb, n_used, pw):
    n_tiles = hs.shape[0] // MOE_TM
    wgu = lambda sel: pl.BlockSpec((None, None, D_MODEL, 2 * D_EXPERT),
                                   lambda j, src, ea, eb, nu: (layer, (ea, eb)[sel][j], 0, 0))
    wd = lambda sel: pl.BlockSpec((None, None, D_EXPERT, D_MODEL),
                                  lambda j, src, ea, eb, nu: (layer, (ea, eb)[sel][j], 0, 0))
    return pl.pallas_call(
        _moe_kernel,
        out_shape=jax.ShapeDtypeStruct((n_tiles * MOE_TM, D_MODEL), F32),
        grid_spec=pltpu.PrefetchScalarGridSpec(
            num_scalar_prefetch=4,
            grid=(n_tiles,),
            in_specs=[pl.BlockSpec((MOE_TM, ROW_W), lambda j, src, ea, eb, nu: (src[j], 0)),
                      wgu(0), wgu(1), wd(0), wd(1)],
            out_specs=pl.BlockSpec((MOE_TM, D_MODEL), lambda j, src, ea, eb, nu: (j, 0))),
        compiler_params=_params(("arbitrary",)),
        name="moe",
    )(src, ea, eb, n_used, hs, pw["w_gu"], pw["w_gu"], pw["w_down"], pw["w_down"])


def _unsort_kernel(pos_ref, ys_ref, x_ref, mod_ref, ng_ref, o_ref, buf_ref, sem, *, tm):
    i = pl.program_id(0)

    def issue(r, carry):
        _row_copy(ys_ref, pos_ref[i * tm + r], buf_ref, r, sem).start()
        return carry

    lax.fori_loop(0, tm, issue, 0)

    def drain(r, carry):
        _row_copy(ys_ref, pos_ref[i * tm + r], buf_ref, r, sem).wait()
        return carry

    lax.fori_loop(0, tm, drain, 0)
    o_ref[...] = x_ref[...] + mod_ref[5:6, :] * _rms(buf_ref[...], ng_ref[3:4, :])


def _unsort_residual(layer, ys, pos, x, mod_l, mod_row, tm, pw):
    B, L, _ = x.shape
    T = B * L
    tiles_per_seq = L // tm
    row = (lambda i: i // tiles_per_seq) if mod_row is None else (lambda i: mod_row)
    out = pl.pallas_call(
        functools.partial(_unsort_kernel, tm=tm),
        out_shape=jax.ShapeDtypeStruct((T, D_MODEL), F32),
        grid_spec=pltpu.PrefetchScalarGridSpec(
            num_scalar_prefetch=1,
            grid=(T // tm,),
            in_specs=[pl.BlockSpec(memory_space=pl.ANY),
                      pl.BlockSpec((tm, D_MODEL), lambda i, pos: (i, 0)),
                      pl.BlockSpec((None, N_MOD, D_MODEL), lambda i, pos: (row(i), 0, 0)),
                      pl.BlockSpec((None, 4, D_MODEL), lambda i, pos: (layer, 0, 0))],
            out_specs=pl.BlockSpec((tm, D_MODEL), lambda i, pos: (i, 0)),
            scratch_shapes=[pltpu.VMEM((tm, D_MODEL), F32), pltpu.SemaphoreType.DMA(())]),
        compiler_params=_params(("arbitrary",)),
        name="moe_unsort",
    )(pos, ys, x.reshape(T, D_MODEL), mod_l, pw["norm_g"])
    return out.reshape(B, L, D_MODEL)


def _moe(layer, hr, counts, x, mod_l, mod_row, tm, pw):
    B, L, _ = x.shape
    T = B * L
    n_tiles = T // MOE_TM + N_BUCKETS
    pos, zrow, src, ea, eb, n_used = _route_tables(hr, counts, n_tiles)
    hs = _scatter_rows(hr.reshape(T, ROW_W), pos, zrow, n_tiles, tm)
    ys = _moe_sorted(layer, hs, src, ea, eb, n_used, pw)
    return _unsort_residual(layer, ys, pos, x, mod_l, mod_row, tm, pw)


def _axial(length, dim):
    rows = length // GRID_W
    row = jnp.repeat(jnp.arange(rows), GRID_W).astype(F32)
    col = jnp.tile(jnp.arange(GRID_W), rows).astype(F32)
    n = dim // 4
    inv = ROPE_BASE ** (-jnp.arange(n, dtype=F32) / n)
    ang = jnp.concatenate([row[:, None] * inv, col[:, None] * inv], axis=-1)
    return jnp.cos(ang), jnp.sin(ang)


def _rope_tables(length, rotate):
    if rotate:
        cos_a, sin_a = _axial(length, MLA_ROPE)
        cos_b, sin_b = _axial(length, SWA_HEAD_DIM)
    else:
        cos_a, sin_a = jnp.ones((length, MLA_ROPE // 2), F32), jnp.zeros((length, MLA_ROPE // 2), F32)
        cos_b, sin_b = jnp.ones((length, SWA_HEAD_DIM // 2), F32), jnp.zeros((length, SWA_HEAD_DIM // 2), F32)
    one = jnp.ones((length, MLA_NOPE), F32)
    zero = jnp.zeros((length, MLA_NOPE), F32)
    sa, sb, sc = MLA_SCALE * LOG2E, SWA_SCALE * LOG2E, DIFF_SCALE * LOG2E
    c_qa = jnp.tile(jnp.concatenate([one, cos_a, cos_a], -1), (1, MLA_HEADS)) * sa
    s_qa = jnp.tile(jnp.concatenate([zero, -sin_a, sin_a], -1), (1, MLA_HEADS)) * sa
    c_b = jnp.concatenate([cos_b, cos_b], -1)
    s_b = jnp.concatenate([-sin_b, sin_b], -1)
    c_c = jnp.tile(jnp.concatenate([cos_a, cos_a], -1), (1, 2 * DIFF_HEADS)) * sc
    s_c = jnp.tile(jnp.concatenate([-sin_a, sin_a], -1), (1, 2 * DIFF_HEADS)) * sc
    tab_c = jnp.concatenate([c_qa, jnp.tile(c_b, (1, SWA_HEADS)) * sb, jnp.tile(c_b, (1, SWA_KV_HEADS)), c_c], -1)
    tab_s = jnp.concatenate([s_qa, jnp.tile(s_b, (1, SWA_HEADS)) * sb, jnp.tile(s_b, (1, SWA_KV_HEADS)), s_c], -1)
    m_qa = jnp.tile(jnp.concatenate([jnp.zeros((MLA_NOPE,)), jnp.ones((MLA_ROPE // 2,)),
                                     jnp.zeros((MLA_ROPE // 2,))]), MLA_HEADS)
    m_b = jnp.concatenate([jnp.ones((SWA_HEAD_DIM // 2,)), jnp.zeros((SWA_HEAD_DIM // 2,))])
    m_c = jnp.concatenate([jnp.ones((DIFF_QK // 2,)), jnp.zeros((DIFF_QK // 2,))])
    msk = jnp.concatenate([m_qa, jnp.tile(m_b, SWA_HEADS + SWA_KV_HEADS), jnp.tile(m_c, 2 * DIFF_HEADS)])
    return tab_c, tab_s, msk.astype(F32)[None, :], cos_a.T, sin_a.T


def _prep_weights(norm_g, w_in, mla_q_gain, mla_kv_gain, mla_w_uq, mla_w_ukv, diff_subln, pool_w,
                  pool_scale, w_out, router_gw, router_gb, router_ew, router_eb, exp_w_gu, exp_w_down):
    sizes = (MLA_Q_RANK, MLA_KV_RANK, MLA_ROPE, 256, 128, 128, 256, 256, 256, MIX_D)
    offs = [0]
    for s in sizes:
        offs.append(offs[-1] + s)
    cq, ckv, kr, bq, bk, bv, dq, dk, dv, u = [w_in[:, :, offs[j]:offs[j + 1]] for j in range(len(sizes))]
    w_r = jnp.concatenate([cq, bq, dq, dv, u, ckv, bk, bv], axis=-1).astype(BF)
    w_kt = jnp.swapaxes(jnp.concatenate([ckv, kr, dk], axis=-1), 1, 2).astype(BF)
    ukv = mla_w_ukv.reshape(DEPTH, MLA_KV_RANK, MLA_HEADS, MLA_NOPE + MLA_V)
    w_uk = ukv[..., :MLA_NOPE].reshape(DEPTH, MLA_KV_RANK, MLA_HEADS * MLA_NOPE)
    w_uv = ukv[..., MLA_NOPE:].reshape(DEPTH, MLA_KV_RANK, MLA_HEADS * MLA_V)
    eye = jnp.eye(len(POOL_WINDOWS), dtype=F32)
    pool_bd = jnp.einsum('lgcd,gh->lgchd', pool_w, eye).reshape(DEPTH, MIX_D, MIX_D)
    r_w = jnp.concatenate([router_ew, router_gw], axis=-1)
    r_w = jnp.pad(r_w, ((0, 0), (0, 0), (0, LANES - r_w.shape[-1])))
    r_hi = r_w.astype(BF)
    r_lo = (r_w - r_hi.astype(F32)).astype(BF)
    r_b = jnp.concatenate([router_eb, router_gb], axis=-1)
    r_b = jnp.pad(r_b, ((0, 0), (0, LANES - r_b.shape[-1])))[:, None, :]
    return dict(
        norm_g=norm_g, w_r=w_r, w_kt=w_kt,
        q_gain=mla_q_gain[:, None, :], kv_gain_r=mla_kv_gain[:, None, :], kv_gain_c=mla_kv_gain[:, :, None],
        w_uq=mla_w_uq.astype(BF), w_ukt=jnp.swapaxes(w_uk, 1, 2).astype(BF), w_uv=w_uv.astype(BF),
        subln=jnp.tile(diff_subln, (1, LANES // DIFF_V))[:, None, :],
        pool_bd=pool_bd.astype(BF), pool_scale=pool_scale[:, None, :], w_out=w_out.astype(BF),
        r_hi=r_hi, r_lo=r_lo, r_b=r_b, w_gu=exp_w_gu.astype(BF), w_down=exp_w_down.astype(BF))


def kernel(x, c, ctx, c_ctx, ada_w, ada_b, norm_g, w_in, mla_q_gain, mla_kv_gain, mla_w_uq, mla_w_ukv,
           swa_sink, diff_lambda, diff_subln, pool_w, pool_scale, w_out, router_gw, router_gb,
           router_ew, router_eb, exp_w_gu, exp_w_down):
    B, L, D = x.shape
    C = ctx.shape[1]
    assert D == D_MODEL and B + 1 <= MOD_ROWS and L % 512 == 0 and C % 256 == 0
    pw = _prep_weights(norm_g, w_in, mla_q_gain, mla_kv_gain, mla_w_uq, mla_w_ukv, diff_subln, pool_w,
                       pool_scale, w_out, router_gw, router_gb, router_ew, router_eb, exp_w_gu, exp_w_down)
    cc = jnp.concatenate([c, c_ctx[None, :], jnp.zeros((MOD_ROWS - B - 1, D), F32)], axis=0)
    mod = jnp.swapaxes(_adaln(cc, ada_w, ada_b), 1, 2)
    tabs_x = _rope_tables(L, True)
    tabs_c = _rope_tables(C, False)
    tm_x, tm_c, tq = 512, 256, 256

    h_ctx = ctx
    for layer in range(DEPTH):
        update_ctx = layer < DEPTH - 1
        lam_init = 0.8 - 0.6 * math.exp(-0.3 * layer)
        mod_l = mod[layer]
        qa, qb, qc, u, va, vb, vc, kb, kta, ktc = _inproj(layer, x, mod_l, None, tm_x, pw, tabs_x)
        cqa, cqb, cqc, cu, cva, cvb, cvc, ckb, ckta, cktc = _inproj(layer, h_ctx, mod_l, B, tm_c, pw, tabs_c)
        diff_extra = (diff_lambda, pw["subln"])

        oa = _dense_attn("mla", layer, qa, [kta, ckta], [va, cva], tq)
        ob = _swa(layer, swa_sink, qb, kb, vb, ckb, cvb, tq)
        oc = _dense_attn("diff", layer, qc, [ktc, cktc], [vc, cvc], tq, diff_extra, lam_init)
        x, hrx, cntx = _outproj(layer, oa, ob, oc, u, x, mod_l, None, tm_x, pw)
        if update_ctx:
            coa = _dense_attn("mla", layer, cqa, [ckta], [cva], tq)
            cob = _swa(layer, swa_sink, cqb, None, None, ckb, cvb, tq)
            coc = _dense_attn("diff", layer, cqc, [cktc], [cvc], tq, diff_extra, lam_init)
            h_ctx, hrc, cntc = _outproj(layer, coa, cob, coc, cu, h_ctx, mod_l, B, tm_c, pw)
        x = _moe(layer, hrx, cntx, x, mod_l, None, tm_x, pw)
        if update_ctx:
            h_ctx = _moe(layer, hrc, cntc, h_ctx, mod_l, B, tm_c, pw)
    return x
```

```python
import functools
import math

import jax
import jax.numpy as jnp
from jax import lax
from jax.experimental import pallas as pl
from jax.experimental.pallas import tpu as pltpu

F32 = jnp.float32
BF = jnp.bfloat16

D_MODEL = 1024
DEPTH = 2
GRID_W = 64
EPS = 1e-6
ROPE_BASE = 10000.0
N_MOD = 6

MLA_HEADS = 4
MLA_Q_RANK = 256
MLA_KV_RANK = 128
MLA_NOPE = 64
MLA_ROPE = 32
MLA_V = 64
MLA_QK = MLA_NOPE + MLA_ROPE
MLA_SCALE = MLA_QK ** -0.5

SWA_HEADS = 4
SWA_KV_HEADS = 2
SWA_HEAD_DIM = 64
SWA_WINDOW = 128
SWA_SCALE = SWA_HEAD_DIM ** -0.5

DIFF_HEADS = 4
DIFF_QK = 32
DIFF_V = 64
DIFF_SCALE = DIFF_QK ** -0.5

POOL_WINDOWS = (2, 4, 8, 16)
POOL_GROUP = 64
POOL_HALO = 8
MIX_D = len(POOL_WINDOWS) * POOL_GROUP

N_EXPERT_GROUPS = 4
EXPERTS_PER_GROUP = 4
N_EXPERTS = 16
D_EXPERT = 512
PAIR_A = (0, 2, 2, 3, 3, 3)
PAIR_B = (1, 1, 0, 0, 1, 2)
PAIRS_PER_GROUP = len(PAIR_A)
N_BUCKETS = N_EXPERT_GROUPS * PAIRS_PER_GROUP

LOG2E = math.log2(math.e)
NEG = -1e30
LANES = 128
MOD_ROWS = 24
SUBLANES = 8
TOK_TILE = (D_MODEL // LANES, LANES)
MOE_TM = 256
DMA_UNROLL = 8
VMEM_LIMIT = 48 * 1024 * 1024

R_CQ, R_BQ, R_DQ, R_DV, R_U, R_CKV, R_BK, R_BV, R_END = 0, 256, 512, 768, 1024, 1280, 1408, 1536, 1664
T_CKV, T_KR, T_DK, T_END = 0, 128, 160, 416
P_QA, P_QB, P_KB, P_QC, P_END = 0, 384, 640, 768, 1024


def _dot(a, b):
    return jnp.dot(a, b, preferred_element_type=F32)


def _dot_nt(a, b):
    return lax.dot_general(a, b, (((1,), (1,)), ((), ())), preferred_element_type=F32)


def _rms(x, g):
    return x * lax.rsqrt(jnp.mean(x * x, axis=-1, keepdims=True) + EPS) * g


def _params(sem):
    return pltpu.CompilerParams(dimension_semantics=sem, vmem_limit_bytes=VMEM_LIMIT)


def _adaln_kernel(c_ref, w_ref, b_ref, o_ref):
    c = c_ref[...]
    s = (c * jax.nn.sigmoid(c)).astype(BF)
    o_ref[...] = _dot(s, w_ref[...].astype(BF)) + b_ref[...]


def _adaln(cc, ada_w, ada_b):
    return pl.pallas_call(
        _adaln_kernel,
        out_shape=jax.ShapeDtypeStruct((DEPTH, N_MOD, MOD_ROWS, D_MODEL), F32),
        grid=(DEPTH, N_MOD),
        in_specs=[
            pl.BlockSpec((MOD_ROWS, D_MODEL), lambda l, j: (0, 0)),
            pl.BlockSpec((None, D_MODEL, D_MODEL), lambda l, j: (l, 0, j)),
            pl.BlockSpec((None, None, 1, D_MODEL), lambda l, j: (l, j, 0, 0)),
        ],
        out_specs=pl.BlockSpec((None, None, MOD_ROWS, D_MODEL), lambda l, j: (l, j, 0, 0)),
        compiler_params=_params(("arbitrary", "arbitrary")),
        name="adaln",
    )(cc, ada_w, ada_b.reshape(DEPTH, N_MOD, 1, D_MODEL))


def _rope_rows(xf, c, s, m1, half):
    w = xf.shape[-1]
    up = pltpu.roll(xf, w - half, axis=1)
    dn = pltpu.roll(xf, half, axis=1)
    return xf * c + jnp.where(m1 > 0.5, up, dn) * s


def _rope_cols(xt, c, s):
    h = xt.shape[0] // 2
    x1, x2 = xt[:h], xt[h:]
    return jnp.concatenate([x1 * c - x2 * s, x1 * s + x2 * c], axis=0)


def _inproj_kernel(x_ref, mod_ref, ng_ref, wr_ref, wkt_ref, qg_ref, kvgr_ref, kvgc_ref,
                   wuq_ref, wukt_ref, wuv_ref, tc_ref, ts_ref, m_ref, ct_ref, st_ref,
                   qa_ref, qb_ref, qc_ref, u_ref, va_ref, vb_ref, vc_ref, kb_ref, kta_ref, ktc_ref):
    x = x_ref[...]
    mod = mod_ref[...]
    hx = _rms(x, ng_ref[0:1, :]) * (1.0 + mod[1:2, :]) + mod[0:1, :]
    hb = hx.astype(BF)
    zr = _dot(hb, wr_ref[...])
    zk = _dot_nt(wkt_ref[...], hb)

    tc = tc_ref[...]
    ts = ts_ref[...]
    m1 = m_ref[...]

    cqn = _rms(zr[:, R_CQ:R_CQ + MLA_Q_RANK], qg_ref[...]).astype(BF)
    qa = _dot(cqn, wuq_ref[...])
    qa_ref[...] = _rope_rows(qa, tc[:, P_QA:P_QB], ts[:, P_QA:P_QB], m1[:, P_QA:P_QB],
                             MLA_ROPE // 2).astype(BF)
    qb_ref[...] = _rope_rows(zr[:, R_BQ:R_DQ], tc[:, P_QB:P_KB], ts[:, P_QB:P_KB], m1[:, P_QB:P_KB],
                             SWA_HEAD_DIM // 2).astype(BF)
    qc_ref[...] = _rope_rows(zr[:, R_DQ:R_DV], tc[:, P_QC:P_END], ts[:, P_QC:P_END], m1[:, P_QC:P_END],
                             DIFF_QK // 2).astype(BF)
    kb_ref[...] = _rope_rows(zr[:, R_BK:R_BV], tc[:, P_KB:P_QC], ts[:, P_KB:P_QC], m1[:, P_KB:P_QC],
                             SWA_HEAD_DIM // 2).astype(BF)
    u_ref[...] = zr[:, R_U:R_CKV]
    vb_ref[...] = zr[:, R_BV:R_END].astype(BF)
    vc_ref[...] = zr[:, R_DV:R_U].astype(BF)

    ckvn = _rms(zr[:, R_CKV:R_BK], kvgr_ref[...]).astype(BF)
    va_ref[...] = _dot(ckvn, wuv_ref[...]).astype(BF)

    ckvt = zk[T_CKV:T_KR]
    ckvtn = ckvt * lax.rsqrt(jnp.mean(ckvt * ckvt, axis=0, keepdims=True) + EPS) * kvgc_ref[...]
    knt = _dot(wukt_ref[...], ckvtn.astype(BF))
    ct = ct_ref[...]
    st = st_ref[...]
    krt = _rope_cols(zk[T_KR:T_DK], ct, st)
    pieces = []
    for h in range(MLA_HEADS):
        pieces.append(knt[h * MLA_NOPE:(h + 1) * MLA_NOPE])
        pieces.append(krt)
    kta_ref[...] = jnp.concatenate(pieces, axis=0).astype(BF)

    dk = zk[T_DK:T_END]
    ktc_ref[...] = jnp.concatenate(
        [_rope_cols(dk[j * DIFF_QK:(j + 1) * DIFF_QK], ct, st) for j in range(2 * DIFF_HEADS)],
        axis=0).astype(BF)


def _inproj(layer, x, mod_l, mod_row, tm, pw, tabs):
    B, L, _ = x.shape
    nt = L // tm
    tab_c, tab_s, msk, cos_t, sin_t = tabs
    row = (lambda b: b) if mod_row is None else (lambda b: mod_row)

    def wspec(shape):
        nd = len(shape)
        return pl.BlockSpec((None,) + shape, lambda i, b: (layer,) + (0,) * nd)

    tok = lambda w: pl.BlockSpec((None, tm, w), lambda i, b: (b, i, 0))
    tokt = lambda w: pl.BlockSpec((None, w, tm), lambda i, b: (b, 0, i))
    in_specs = [
        tok(D_MODEL),
        pl.BlockSpec((None, N_MOD, D_MODEL), lambda i, b: (row(b), 0, 0)),
        wspec((4, D_MODEL)),
        wspec((D_MODEL, R_END)),
        wspec((T_END, D_MODEL)),
        wspec((1, MLA_Q_RANK)),
        wspec((1, MLA_KV_RANK)),
        wspec((MLA_KV_RANK, 1)),
        wspec((MLA_Q_RANK, MLA_HEADS * MLA_QK)),
        wspec((MLA_HEADS * MLA_NOPE, MLA_KV_RANK)),
        wspec((MLA_KV_RANK, MLA_HEADS * MLA_V)),
        pl.BlockSpec((tm, P_END), lambda i, b: (i, 0)),
        pl.BlockSpec((tm, P_END), lambda i, b: (i, 0)),
        pl.BlockSpec((1, P_END), lambda i, b: (0, 0)),
        pl.BlockSpec((MLA_ROPE // 2, tm), lambda i, b: (0, i)),
        pl.BlockSpec((MLA_ROPE // 2, tm), lambda i, b: (0, i)),
    ]
    widths = [(MLA_HEADS * MLA_QK, BF), (256, BF), (256, BF), (MIX_D, F32),
              (256, BF), (128, BF), (256, BF), (128, BF)]
    out_shape = [jax.ShapeDtypeStruct((B, L, w), dt) for w, dt in widths]
    out_specs = [tok(w) for w, _ in widths]
    out_shape += [jax.ShapeDtypeStruct((B, MLA_HEADS * MLA_QK, L), BF),
                  jax.ShapeDtypeStruct((B, 2 * DIFF_HEADS * DIFF_QK, L), BF)]
    out_specs += [tokt(MLA_HEADS * MLA_QK), tokt(2 * DIFF_HEADS * DIFF_QK)]
    return pl.pallas_call(
        _inproj_kernel,
        out_shape=out_shape,
        grid=(nt, B),
        in_specs=in_specs,
        out_specs=out_specs,
        compiler_params=_params(("arbitrary", "arbitrary")),
        name="inproj",
    )(x, mod_l, pw["norm_g"], pw["w_r"], pw["w_kt"], pw["q_gain"], pw["kv_gain_r"], pw["kv_gain_c"],
      pw["w_uq"], pw["w_ukt"], pw["w_uv"], tab_c, tab_s, msk, cos_t, sin_t)


def _softmax_pv(q, segs, extra=None):
    ss = []
    m = None
    for kt, _, mask in segs:
        s = _dot(q, kt)
        if mask is not None:
            s = jnp.where(mask, s, NEG)
        ss.append(s)
        sm = jnp.max(s, axis=-1, keepdims=True)
        m = sm if m is None else jnp.maximum(m, sm)
    if extra is not None:
        m = jnp.maximum(m, extra)
    l = None
    r = None
    for s, (_, v, _) in zip(ss, segs):
        e = jnp.exp2(s - m)
        ls = jnp.sum(e, axis=-1, keepdims=True)
        rs = _dot(e.astype(BF), v)
        l = ls if l is None else l + ls
        r = rs if r is None else r + rs
    if extra is not None:
        l = l + jnp.exp2(extra - m)
    return r / l


def _mla_kernel(*refs, nseg):
    q_ref = refs[0]
    kt_refs = refs[1:1 + nseg]
    v_refs = refs[1 + nseg:1 + 2 * nseg]
    o_ref = refs[1 + 2 * nseg]
    q = q_ref[...]
    lane = lax.broadcasted_iota(jnp.int32, (1, LANES), 1)
    blocks = []
    for p in range(MLA_HEADS // 2):
        pair = []
        for h in (2 * p, 2 * p + 1):
            qh = q[:, h * MLA_QK:(h + 1) * MLA_QK]
            segs = [(kt[h * MLA_QK:(h + 1) * MLA_QK, :], v[:, p * LANES:(p + 1) * LANES], None)
                    for kt, v in zip(kt_refs, v_refs)]
            pair.append(_softmax_pv(qh, segs))
        blocks.append(jnp.where(lane < MLA_V, pair[0], pair[1]))
    o_ref[...] = jnp.concatenate(blocks, axis=-1).astype(BF)


def _diff_kernel(*refs, nseg, lam_init):
    q_ref = refs[0]
    kt_refs = refs[1:1 + nseg]
    v_refs = refs[1 + nseg:1 + 2 * nseg]
    dl_ref, g_ref, o_ref = refs[1 + 2 * nseg:]
    q = q_ref[...]
    dl = dl_ref[...]
    lam = (jnp.exp(jnp.sum(dl[0:1] * dl[1:2], axis=-1, keepdims=True))
           - jnp.exp(jnp.sum(dl[2:3] * dl[3:4], axis=-1, keepdims=True)) + lam_init)
    lane = lax.broadcasted_iota(jnp.int32, (1, LANES), 1)
    lo = lane < DIFF_V
    blocks = []
    for p in range(DIFF_HEADS // 2):
        pair = []
        for h in (2 * p, 2 * p + 1):
            comp = []
            for c in range(2):
                j = 2 * h + c
                qj = q[:, j * DIFF_QK:(j + 1) * DIFF_QK]
                segs = [(kt[j * DIFF_QK:(j + 1) * DIFF_QK, :], v[:, p * LANES:(p + 1) * LANES], None)
                        for kt, v in zip(kt_refs, v_refs)]
                comp.append(_softmax_pv(qj, segs))
            pair.append(comp[0] - lam * comp[1])
        o = jnp.where(lo, pair[0], pair[1])
        o2 = o * o
        ms_lo = jnp.sum(jnp.where(lo, o2, 0.0), axis=-1, keepdims=True) * (1.0 / DIFF_V)
        ms_hi = jnp.sum(jnp.where(lo, 0.0, o2), axis=-1, keepdims=True) * (1.0 / DIFF_V)
        ms = jnp.where(lo, ms_lo, ms_hi)
        blocks.append(o * lax.rsqrt(ms + EPS) * g_ref[...] * (1.0 - lam_init))
    o_ref[...] = jnp.concatenate(blocks, axis=-1).astype(BF)


def _dense_attn(kind, layer, q, kts, vs, tq, extra_in=(), lam_init=0.0):
    B, L, wq = q.shape
    nseg = len(kts)
    in_specs = [pl.BlockSpec((None, tq, wq), lambda b, i: (b, i, 0))]
    for kt in kts:
        in_specs.append(pl.BlockSpec((None,) + kt.shape[1:], lambda b, i: (b, 0, 0)))
    for v in vs:
        in_specs.append(pl.BlockSpec((None,) + v.shape[1:], lambda b, i: (b, 0, 0)))
    if kind == "mla":
        body = functools.partial(_mla_kernel, nseg=nseg)
    else:
        body = functools.partial(_diff_kernel, nseg=nseg, lam_init=lam_init)
        dl, g = extra_in
        in_specs.append(pl.BlockSpec((None, 4, DIFF_QK), lambda b, i: (layer, 0, 0)))
        in_specs.append(pl.BlockSpec((None, 1, LANES), lambda b, i: (layer, 0, 0)))
    return pl.pallas_call(
        body,
        out_shape=jax.ShapeDtypeStruct((B, L, 256), BF),
        grid=(B, L // tq),
        in_specs=in_specs,
        out_specs=pl.BlockSpec((None, tq, 256), lambda b, i: (b, i, 0)),
        compiler_params=_params(("arbitrary", "arbitrary")),
        name=kind,
    )(q, *kts, *vs, *extra_in)


def _swa_kernel(*refs, layer, tq, seq_len, local):
    if local:
        sink_ref, q_ref, kl_ref, vl_ref, kc_ref, vc_ref, o_ref = refs
    else:
        sink_ref, q_ref, kc_ref, vc_ref, o_ref = refs
    i = pl.program_id(1)
    q = q_ref[...]
    lane = lax.broadcasted_iota(jnp.int32, (1, LANES), 1)
    rows = lax.broadcasted_iota(jnp.int32, (2 * tq, 1), 0)
    top = rows < tq
    if local:
        win = tq + 2 * SWA_WINDOW
        q0 = i * tq
        ws = pl.multiple_of(jnp.clip(q0 - SWA_WINDOW, 0, seq_len - win), LANES)
        kwin = kl_ref[pl.ds(ws, win), :]
        vwin = vl_ref[pl.ds(ws, win), :]
        qpos = q0 + jnp.where(top, rows, rows - tq)
        kpos = ws + lax.broadcasted_iota(jnp.int32, (1, win), 1)
        band = jnp.abs(kpos - qpos) <= SWA_WINDOW
    blocks = []
    for k in range(SWA_KV_HEADS):
        q2 = jnp.concatenate([q[:, (2 * k) * SWA_HEAD_DIM:(2 * k + 1) * SWA_HEAD_DIM],
                              q[:, (2 * k + 1) * SWA_HEAD_DIM:(2 * k + 2) * SWA_HEAD_DIM]], axis=0)
        sl = slice(k * SWA_HEAD_DIM, (k + 1) * SWA_HEAD_DIM)
        sk = jnp.where(top, sink_ref[layer, 2 * k], sink_ref[layer, 2 * k + 1]) * LOG2E
        ss = []
        if local:
            ss.append((jnp.where(band, _dot_nt(q2, kwin[:, sl]), NEG), vwin))
        ss.append((_dot_nt(q2, kc_ref[:, sl]), vc_ref[...]))
        m = sk
        for s, _ in ss:
            m = jnp.maximum(m, jnp.max(s, axis=-1, keepdims=True))
        l = jnp.exp2(sk - m)
        r = None
        for s, v in ss:
            e = jnp.exp2(s - m)
            l = l + jnp.sum(e, axis=-1, keepdims=True)
            rs = _dot(e.astype(BF), v)
            r = rs if r is None else r + rs
        r = r / l
        r_top, r_bot = r[:tq], r[tq:]
        if k == 0:
            r_bot = pltpu.roll(r_bot, SWA_HEAD_DIM, axis=1)
        else:
            r_top = pltpu.roll(r_top, SWA_HEAD_DIM, axis=1)
        blocks.append(jnp.where(lane < SWA_HEAD_DIM, r_top, r_bot))
    o_ref[...] = jnp.concatenate(blocks, axis=-1).astype(BF)


def _swa(layer, sink, q, k_loc, v_loc, k_ctx, v_ctx, tq):
    B, L, _ = q.shape
    C = k_ctx.shape[1]
    local = k_loc is not None
    full = lambda n: pl.BlockSpec((None, n, 128), lambda b, i: (b, 0, 0))
    in_specs = [pl.BlockSpec(memory_space=pltpu.SMEM),
                pl.BlockSpec((None, tq, 256), lambda b, i: (b, i, 0))]
    args = [sink, q]
    if local:
        in_specs += [full(L), full(L)]
        args += [k_loc, v_loc]
    in_specs += [full(C), full(C)]
    args += [k_ctx, v_ctx]
    return pl.pallas_call(
        functools.partial(_swa_kernel, layer=layer, tq=tq, seq_len=L, local=local),
        out_shape=jax.ShapeDtypeStruct((B, L, 256), BF),
        grid=(B, L // tq),
        in_specs=in_specs,
        out_specs=pl.BlockSpec((None, tq, 256), lambda b, i: (b, i, 0)),
        compiler_params=_params(("arbitrary", "arbitrary")),
        name="swa",
    )(*args)


def _pool_diffs(u, prev, nxt, i, nt, tm, seq_len):
    n = tm + 2 * POOL_HALO
    ue = jnp.concatenate([jnp.where(i > 0, prev, 0.0), u, jnp.where(i < nt - 1, nxt, 0.0)], axis=0)

    def sh(a, k):
        return pltpu.roll(a, k % n, axis=0)

    s2 = ue + sh(ue, 1)
    s4 = sh(s2, 1) + sh(s2, -1)
    s8 = sh(s4, 2) + sh(s4, -2)
    s16 = sh(s8, 4) + sh(s8, -4)
    lane = lax.broadcasted_iota(jnp.int32, (1, MIX_D), 1)
    t = i * tm + lax.broadcasted_iota(jnp.int32, (tm, 1), 0)
    win = None
    cnt = None
    for gi, (w, s) in enumerate(zip(POOL_WINDOWS, (s2, s4, s8, s16))):
        half = w // 2
        c = (jnp.minimum(t - half + w, seq_len) - jnp.maximum(t - half, 0)).astype(F32)
        sv = s[POOL_HALO:POOL_HALO + tm]
        if win is None:
            win, cnt = sv, c
        else:
            sel = lane >= gi * POOL_GROUP
            win = jnp.where(sel, sv, win)
            cnt = jnp.where(sel, c, cnt)
    return win / cnt - u


def _outproj_kernel(oa_ref, ob_ref, oc_ref, u_ref, up_ref, un_ref, x_ref, mod_ref, ng_ref,
                    wo_ref, pw_ref, ps_ref, rwh_ref, rwl_ref, rb_ref,
                    xo_ref, hr_ref, rt_ref, cnt_ref, carry_ref, *, tm, seq_len, nt):
    i = pl.program_id(1)

    @pl.when((pl.program_id(0) == 0) & (i == 0))
    def _():
        carry_ref[...] = jnp.zeros_like(carry_ref)

    u = u_ref[...]
    dg = _pool_diffs(u, up_ref[...], un_ref[...], i, nt, tm, seq_len)
    od = _dot(dg.astype(BF), pw_ref[...]) * ps_ref[...]
    mix = jnp.concatenate([oa_ref[...], ob_ref[...], oc_ref[...], od.astype(BF)], axis=-1)
    ox = _dot(mix, wo_ref[...])
    mod = mod_ref[...]
    x = x_ref[...] + mod[2:3, :] * _rms(ox, ng_ref[1:2, :])
    xo_ref[...] = x
    h = _rms(x, ng_ref[2:3, :]) * (1.0 + mod[4:5, :]) + mod[3:4, :]
    hr_ref[...] = h.reshape((tm,) + TOK_TILE)
    hh = h.astype(BF)
    hl = (h - hh.astype(F32)).astype(BF)
    wh = rwh_ref[...]
    logits = _dot(hh, wh) + _dot(hl, wh) + _dot(hh, rwl_ref[...]) + rb_ref[...]

    lane = lax.broadcasted_iota(jnp.int32, (1, LANES), 1)
    lanef = lane.astype(F32)
    is_g = (lane >= N_EXPERTS) & (lane < N_EXPERTS + N_EXPERT_GROUPS)
    gl = jnp.where(is_g, logits, NEG)
    gmax = jnp.max(gl, axis=-1, keepdims=True)
    gsel = jnp.min(jnp.where(gl == gmax, lanef, 1e9), axis=-1, keepdims=True) - N_EXPERTS
    p_g = 1.0 / jnp.sum(jnp.where(is_g, jnp.exp(gl - gmax), 0.0), axis=-1, keepdims=True)
    grp = jnp.right_shift(lane, 2).astype(F32)
    el = jnp.where((lane < N_EXPERTS) & (grp == gsel), logits, NEG)
    v1 = jnp.max(el, axis=-1, keepdims=True)
    i1 = jnp.min(jnp.where(el == v1, lanef, 1e9), axis=-1, keepdims=True)
    el2 = jnp.where(lanef == i1, NEG, el)
    v2 = jnp.max(el2, axis=-1, keepdims=True)
    i2 = jnp.min(jnp.where(el2 == v2, lanef, 1e9), axis=-1, keepdims=True)
    e21 = jnp.exp(v2 - v1)
    w1 = p_g / (1.0 + e21)
    w2 = w1 * e21

    l1 = i1 - EXPERTS_PER_GROUP * gsel
    l2 = i2 - EXPERTS_PER_GROUP * gsel
    code = jnp.minimum(l1, l2) * EXPERTS_PER_GROUP + jnp.maximum(l1, l2)
    pidx = jnp.full_like(code, float(PAIRS_PER_GROUP - 1))
    slot_a = jnp.full_like(code, float(PAIR_A[-1]))
    for p in range(PAIRS_PER_GROUP - 1):
        lo_e, hi_e = min(PAIR_A[p], PAIR_B[p]), max(PAIR_A[p], PAIR_B[p])
        hit = code == float(lo_e * EXPERTS_PER_GROUP + hi_e)
        pidx = jnp.where(hit, float(p), pidx)
        slot_a = jnp.where(hit, float(PAIR_A[p]), slot_a)
    first_is_a = l1 == slot_a
    w_a = jnp.where(first_is_a, w1, w2)
    w_b = jnp.where(first_is_a, w2, w1)
    bucket = gsel * PAIRS_PER_GROUP + pidx

    onehot = lanef == bucket
    rr = lax.broadcasted_iota(jnp.int32, (tm, tm), 0)
    cc = lax.broadcasted_iota(jnp.int32, (tm, tm), 1)
    tri = jnp.where(rr >= cc, 1.0, 0.0).astype(BF)
    prefix = _dot(tri, jnp.where(onehot, 1.0, 0.0).astype(BF))
    carry = carry_ref[...]
    rank = jnp.sum(jnp.where(onehot, prefix - 1.0 + carry, 0.0), axis=-1, keepdims=True)
    carry = carry + prefix[tm - 1:tm, :]
    carry_ref[...] = carry
    cnt_ref[...] = carry
    rt_ref[...] = jnp.where(lane == 0, w_a, jnp.where(lane == 1, w_b, jnp.where(
        lane == 2, bucket, jnp.where(lane == 3, rank, 0.0))))


def _outproj(layer, oa, ob, oc, u, x, mod_l, mod_row, tm, pw):
    B, L, _ = x.shape
    nt = L // tm
    hb = tm // POOL_HALO
    nhb = L // POOL_HALO
    row = (lambda b: b) if mod_row is None else (lambda b: mod_row)

    def wspec(shape):
        nd = len(shape)
        return pl.BlockSpec((None,) + shape, lambda b, i: (layer,) + (0,) * nd)

    tok = lambda w: pl.BlockSpec((None, tm, w), lambda b, i: (b, i, 0))
    in_specs = [
        tok(256), tok(256), tok(256), tok(MIX_D),
        pl.BlockSpec((None, POOL_HALO, MIX_D), lambda b, i: (b, jnp.maximum(i * hb - 1, 0), 0)),
        pl.BlockSpec((None, POOL_HALO, MIX_D), lambda b, i: (b, jnp.minimum((i + 1) * hb, nhb - 1), 0)),
        tok(D_MODEL),
        pl.BlockSpec((None, N_MOD, D_MODEL), lambda b, i: (row(b), 0, 0)),
        wspec((4, D_MODEL)),
        wspec((D_MODEL, D_MODEL)),
        wspec((MIX_D, MIX_D)),
        wspec((1, MIX_D)),
        wspec((D_MODEL, LANES)),
        wspec((D_MODEL, LANES)),
        wspec((1, LANES)),
    ]
    return pl.pallas_call(
        functools.partial(_outproj_kernel, tm=tm, seq_len=L, nt=nt),
        out_shape=[jax.ShapeDtypeStruct((B, L, D_MODEL), F32),
                   jax.ShapeDtypeStruct((B * L,) + TOK_TILE, F32),
                   jax.ShapeDtypeStruct((B, L, LANES), F32),
                   jax.ShapeDtypeStruct((1, LANES), F32)],
        grid=(B, nt),
        in_specs=in_specs,
        out_specs=[tok(D_MODEL),
                   pl.BlockSpec((tm,) + TOK_TILE, lambda b, i: (b * nt + i, 0, 0)),
                   tok(LANES), pl.BlockSpec((1, LANES), lambda b, i: (0, 0))],
        scratch_shapes=[pltpu.VMEM((1, LANES), F32)],
        compiler_params=_params(("arbitrary", "arbitrary")),
        name="outproj",
    )(oa, ob, oc, u, u, u, x, mod_l, pw["norm_g"], pw["w_out"], pw["pool_bd"], pw["pool_scale"],
      pw["r_hi"], pw["r_lo"], pw["r_b"])


def _route_tables(route, counts, n_tiles):
    T = route.shape[0] * route.shape[1]
    bucket = route[:, :, 2].reshape(T).astype(jnp.int32)
    rank = route[:, :, 3].reshape(T).astype(jnp.int32)
    cnt = counts[0, :N_BUCKETS].astype(jnp.int32)
    tiles_b = (cnt + MOE_TM - 1) // MOE_TM
    ends = jnp.cumsum(tiles_b)
    starts = ends - tiles_b
    pos = jnp.take(starts * MOE_TM, bucket) + rank
    n_used = ends[-1]
    tile = jnp.arange(n_tiles, dtype=jnp.int32)
    src = jnp.minimum(tile, n_used - 1)
    tb = jnp.sum((src[:, None] >= ends[None, :]).astype(jnp.int32), axis=1)
    tb = jnp.minimum(tb, N_BUCKETS - 1)
    grp, pidx = tb // PAIRS_PER_GROUP, tb % PAIRS_PER_GROUP
    ea = grp * EXPERTS_PER_GROUP + jnp.take(jnp.array(PAIR_A, jnp.int32), pidx)
    eb = grp * EXPERTS_PER_GROUP + jnp.take(jnp.array(PAIR_B, jnp.int32), pidx)
    zlast = jnp.where(tiles_b > 0, (ends - 1) * MOE_TM, -1)
    idle = n_used + jnp.arange(N_BUCKETS, dtype=jnp.int32)
    zidle = jnp.where(idle < n_tiles, idle * MOE_TM, -1)
    zrow = jnp.concatenate([zlast, zidle]).astype(jnp.int32)
    return pos, zrow, src, ea, eb, n_used.reshape(1).astype(jnp.int32)


def _row_copy(src_ref, src_row, dst_ref, dst_row, sem):
    return pltpu.make_async_copy(src_ref.at[src_row], dst_ref.at[dst_row], sem)


def _scatter_kernel(pos_ref, zrow_ref, hr_ref, hs_ref, zbuf_ref, sem, zsem, *, tm):
    i = pl.program_id(0)

    @pl.when(i == 0)
    def _():
        zbuf_ref[...] = jnp.zeros_like(zbuf_ref)
        for start in (True, False):
            for b in range(2 * N_BUCKETS):
                @pl.when(zrow_ref[b] >= 0)
                def _():
                    z0 = pl.multiple_of(zrow_ref[b], MOE_TM)
                    cp = pltpu.make_async_copy(zbuf_ref, hs_ref.at[pl.ds(z0, MOE_TM)], zsem)
                    cp.start() if start else cp.wait()

    def issue(r, carry):
        _row_copy(hr_ref, r, hs_ref, pos_ref[i * tm + r], sem).start()
        return carry

    lax.fori_loop(0, tm, issue, 0, unroll=DMA_UNROLL)

    def drain(r, carry):
        _row_copy(hr_ref, r, hs_ref, pos_ref[i * tm + r], sem).wait()
        return carry

    lax.fori_loop(0, tm, drain, 0, unroll=DMA_UNROLL)


def _scatter_rows(hr, pos, zrow, n_tiles, tm):
    T = hr.shape[0]
    return pl.pallas_call(
        functools.partial(_scatter_kernel, tm=tm),
        out_shape=jax.ShapeDtypeStruct((n_tiles * MOE_TM,) + TOK_TILE, F32),
        grid_spec=pltpu.PrefetchScalarGridSpec(
            num_scalar_prefetch=2,
            grid=(T // tm,),
            in_specs=[pl.BlockSpec((tm,) + TOK_TILE, lambda i, pos, zrow: (i, 0, 0))],
            out_specs=pl.BlockSpec(memory_space=pl.ANY),
            scratch_shapes=[pltpu.VMEM((MOE_TM,) + TOK_TILE, F32),
                            pltpu.SemaphoreType.DMA(()), pltpu.SemaphoreType.DMA(())]),
        compiler_params=_params(("arbitrary",)),
        name="moe_scatter",
    )(pos, zrow, hr)


def _expert(h, wgu, wd):
    gu = _dot(h, wgu)
    g = gu[:, :D_EXPERT]
    act = (g * jax.nn.sigmoid(g)) * gu[:, D_EXPERT:]
    return _dot(act.astype(BF), wd)


def _moe_kernel(src_ref, ea_ref, eb_ref, nu_ref, hs_ref, wgua_ref, wgub_ref, wda_ref, wdb_ref, o_ref):
    j = pl.program_id(0)

    @pl.when(j < nu_ref[0])
    def _():
        h = hs_ref[...].reshape(MOE_TM, D_MODEL).astype(BF)
        o_ref[:, :SUBLANES, :] = _expert(h, wgua_ref[...], wda_ref[...]).reshape((MOE_TM,) + TOK_TILE)
        o_ref[:, SUBLANES:, :] = _expert(h, wgub_ref[...], wdb_ref[...]).reshape((MOE_TM,) + TOK_TILE)

    @pl.when(j >= nu_ref[0])
    def _():
        o_ref[...] = jnp.zeros_like(o_ref)


def _moe_sorted(layer, hs, src, ea, eb, n_used, pw):
    n_tiles = hs.shape[0] // MOE_TM
    wgu = lambda sel: pl.BlockSpec((None, None, D_MODEL, 2 * D_EXPERT),
                                   lambda j, src, ea, eb, nu: (layer, (ea, eb)[sel][j], 0, 0))
    wd = lambda sel: pl.BlockSpec((None, None, D_EXPERT, D_MODEL),
                                  lambda j, src, ea, eb, nu: (layer, (ea, eb)[sel][j], 0, 0))
    return pl.pallas_call(
        _moe_kernel,
        out_shape=jax.ShapeDtypeStruct((n_tiles * MOE_TM, 2 * SUBLANES, LANES), F32),
        grid_spec=pltpu.PrefetchScalarGridSpec(
            num_scalar_prefetch=4,
            grid=(n_tiles,),
            in_specs=[pl.BlockSpec((MOE_TM,) + TOK_TILE, lambda j, src, ea, eb, nu: (src[j], 0, 0)),
                      wgu(0), wgu(1), wd(0), wd(1)],
            out_specs=pl.BlockSpec((MOE_TM, 2 * SUBLANES, LANES), lambda j, src, ea, eb, nu: (j, 0, 0))),
        compiler_params=_params(("arbitrary",)),
        name="moe",
    )(src, ea, eb, n_used, hs, pw["w_gu"], pw["w_gu"], pw["w_down"], pw["w_down"])


def _unsort_kernel(pos_ref, ys_ref, rt_ref, x_ref, mod_ref, ng_ref, o_ref, buf_ref, sem, *, tm):
    i = pl.program_id(0)

    def issue(r, carry):
        _row_copy(ys_ref, pos_ref[i * tm + r], buf_ref, r, sem).start()
        return carry

    lax.fori_loop(0, tm, issue, 0, unroll=DMA_UNROLL)

    def drain(r, carry):
        _row_copy(ys_ref, pos_ref[i * tm + r], buf_ref, r, sem).wait()
        return carry

    lax.fori_loop(0, tm, drain, 0, unroll=DMA_UNROLL)
    rt = rt_ref[...]
    fx = (rt[:, 0:1] * buf_ref[:, :SUBLANES, :].reshape(tm, D_MODEL)
          + rt[:, 1:2] * buf_ref[:, SUBLANES:, :].reshape(tm, D_MODEL))
    o_ref[...] = x_ref[...] + mod_ref[5:6, :] * _rms(fx, ng_ref[3:4, :])


def _unsort_residual(layer, ys, pos, route, x, mod_l, mod_row, tm, pw):
    B, L, _ = x.shape
    T = B * L
    tiles_per_seq = L // tm
    row = (lambda i: i // tiles_per_seq) if mod_row is None else (lambda i: mod_row)
    out = pl.pallas_call(
        functools.partial(_unsort_kernel, tm=tm),
        out_shape=jax.ShapeDtypeStruct((T, D_MODEL), F32),
        grid_spec=pltpu.PrefetchScalarGridSpec(
            num_scalar_prefetch=1,
            grid=(T // tm,),
            in_specs=[pl.BlockSpec(memory_space=pl.ANY),
                      pl.BlockSpec((tm, LANES), lambda i, pos: (i, 0)),
                      pl.BlockSpec((tm, D_MODEL), lambda i, pos: (i, 0)),
                      pl.BlockSpec((None, N_MOD, D_MODEL), lambda i, pos: (row(i), 0, 0)),
                      pl.BlockSpec((None, 4, D_MODEL), lambda i, pos: (layer, 0, 0))],
            out_specs=pl.BlockSpec((tm, D_MODEL), lambda i, pos: (i, 0)),
            scratch_shapes=[pltpu.VMEM((tm, 2 * SUBLANES, LANES), F32), pltpu.SemaphoreType.DMA(())]),
        compiler_params=_params(("arbitrary",)),
        name="moe_unsort",
    )(pos, ys, route.reshape(T, LANES), x.reshape(T, D_MODEL), mod_l, pw["norm_g"])
    return out.reshape(B, L, D_MODEL)


def _moe(layer, hr, route, counts, x, mod_l, mod_row, tm, pw):
    B, L, _ = x.shape
    T = B * L
    n_tiles = T // MOE_TM + N_BUCKETS
    pos, zrow, src, ea, eb, n_used = _route_tables(route, counts, n_tiles)
    hs = _scatter_rows(hr, pos, zrow, n_tiles, tm)
    ys = _moe_sorted(layer, hs, src, ea, eb, n_used, pw)
    return _unsort_residual(layer, ys, pos, route, x, mod_l, mod_row, tm, pw)


def _axial(length, dim):
    rows = length // GRID_W
    row = jnp.repeat(jnp.arange(rows), GRID_W).astype(F32)
    col = jnp.tile(jnp.arange(GRID_W), rows).astype(F32)
    n = dim // 4
    inv = ROPE_BASE ** (-jnp.arange(n, dtype=F32) / n)
    ang = jnp.concatenate([row[:, None] * inv, col[:, None] * inv], axis=-1)
    return jnp.cos(ang), jnp.sin(ang)


def _rope_tables(length, rotate):
    if rotate:
        cos_a, sin_a = _axial(length, MLA_ROPE)
        cos_b, sin_b = _axial(length, SWA_HEAD_DIM)
    else:
        cos_a, sin_a = jnp.ones((length, MLA_ROPE // 2), F32), jnp.zeros((length, MLA_ROPE // 2), F32)
        cos_b, sin_b = jnp.ones((length, SWA_HEAD_DIM // 2), F32), jnp.zeros((length, SWA_HEAD_DIM // 2), F32)
    one = jnp.ones((length, MLA_NOPE), F32)
    zero = jnp.zeros((length, MLA_NOPE), F32)
    sa, sb, sc = MLA_SCALE * LOG2E, SWA_SCALE * LOG2E, DIFF_SCALE * LOG2E
    c_qa = jnp.tile(jnp.concatenate([one, cos_a, cos_a], -1), (1, MLA_HEADS)) * sa
    s_qa = jnp.tile(jnp.concatenate([zero, -sin_a, sin_a], -1), (1, MLA_HEADS)) * sa
    c_b = jnp.concatenate([cos_b, cos_b], -1)
    s_b = jnp.concatenate([-sin_b, sin_b], -1)
    c_c = jnp.tile(jnp.concatenate([cos_a, cos_a], -1), (1, 2 * DIFF_HEADS)) * sc
    s_c = jnp.tile(jnp.concatenate([-sin_a, sin_a], -1), (1, 2 * DIFF_HEADS)) * sc
    tab_c = jnp.concatenate([c_qa, jnp.tile(c_b, (1, SWA_HEADS)) * sb, jnp.tile(c_b, (1, SWA_KV_HEADS)), c_c], -1)
    tab_s = jnp.concatenate([s_qa, jnp.tile(s_b, (1, SWA_HEADS)) * sb, jnp.tile(s_b, (1, SWA_KV_HEADS)), s_c], -1)
    m_qa = jnp.tile(jnp.concatenate([jnp.zeros((MLA_NOPE,)), jnp.ones((MLA_ROPE // 2,)),
                                     jnp.zeros((MLA_ROPE // 2,))]), MLA_HEADS)
    m_b = jnp.concatenate([jnp.ones((SWA_HEAD_DIM // 2,)), jnp.zeros((SWA_HEAD_DIM // 2,))])
    m_c = jnp.concatenate([jnp.ones((DIFF_QK // 2,)), jnp.zeros((DIFF_QK // 2,))])
    msk = jnp.concatenate([m_qa, jnp.tile(m_b, SWA_HEADS + SWA_KV_HEADS), jnp.tile(m_c, 2 * DIFF_HEADS)])
    return tab_c, tab_s, msk.astype(F32)[None, :], cos_a.T, sin_a.T


def _prep_weights(norm_g, w_in, mla_q_gain, mla_kv_gain, mla_w_uq, mla_w_ukv, diff_subln, pool_w,
                  pool_scale, w_out, router_gw, router_gb, router_ew, router_eb, exp_w_gu, exp_w_down):
    sizes = (MLA_Q_RANK, MLA_KV_RANK, MLA_ROPE, 256, 128, 128, 256, 256, 256, MIX_D)
    offs = [0]
    for s in sizes:
        offs.append(offs[-1] + s)
    cq, ckv, kr, bq, bk, bv, dq, dk, dv, u = [w_in[:, :, offs[j]:offs[j + 1]] for j in range(len(sizes))]
    w_r = jnp.concatenate([cq, bq, dq, dv, u, ckv, bk, bv], axis=-1).astype(BF)
    w_kt = jnp.swapaxes(jnp.concatenate([ckv, kr, dk], axis=-1), 1, 2).astype(BF)
    ukv = mla_w_ukv.reshape(DEPTH, MLA_KV_RANK, MLA_HEADS, MLA_NOPE + MLA_V)
    w_uk = ukv[..., :MLA_NOPE].reshape(DEPTH, MLA_KV_RANK, MLA_HEADS * MLA_NOPE)
    w_uv = ukv[..., MLA_NOPE:].reshape(DEPTH, MLA_KV_RANK, MLA_HEADS * MLA_V)
    eye = jnp.eye(len(POOL_WINDOWS), dtype=F32)
    pool_bd = jnp.einsum('lgcd,gh->lgchd', pool_w, eye).reshape(DEPTH, MIX_D, MIX_D)
    r_w = jnp.concatenate([router_ew, router_gw], axis=-1)
    r_w = jnp.pad(r_w, ((0, 0), (0, 0), (0, LANES - r_w.shape[-1])))
    r_hi = r_w.astype(BF)
    r_lo = (r_w - r_hi.astype(F32)).astype(BF)
    r_b = jnp.concatenate([router_eb, router_gb], axis=-1)
    r_b = jnp.pad(r_b, ((0, 0), (0, LANES - r_b.shape[-1])))[:, None, :]
    return dict(
        norm_g=norm_g, w_r=w_r, w_kt=w_kt,
        q_gain=mla_q_gain[:, None, :], kv_gain_r=mla_kv_gain[:, None, :], kv_gain_c=mla_kv_gain[:, :, None],
        w_uq=mla_w_uq.astype(BF), w_ukt=jnp.swapaxes(w_uk, 1, 2).astype(BF), w_uv=w_uv.astype(BF),
        subln=jnp.tile(diff_subln, (1, LANES // DIFF_V))[:, None, :],
        pool_bd=pool_bd.astype(BF), pool_scale=pool_scale[:, None, :], w_out=w_out.astype(BF),
        r_hi=r_hi, r_lo=r_lo, r_b=r_b, w_gu=exp_w_gu.astype(BF), w_down=exp_w_down.astype(BF))


def kernel(x, c, ctx, c_ctx, ada_w, ada_b, norm_g, w_in, mla_q_gain, mla_kv_gain, mla_w_uq, mla_w_ukv,
           swa_sink, diff_lambda, diff_subln, pool_w, pool_scale, w_out, router_gw, router_gb,
           router_ew, router_eb, exp_w_gu, exp_w_down):
    B, L, D = x.shape
    C = ctx.shape[1]
    assert D == D_MODEL and B + 1 <= MOD_ROWS and L % 512 == 0 and C % 256 == 0
    pw = _prep_weights(norm_g, w_in, mla_q_gain, mla_kv_gain, mla_w_uq, mla_w_ukv, diff_subln, pool_w,
                       pool_scale, w_out, router_gw, router_gb, router_ew, router_eb, exp_w_gu, exp_w_down)
    cc = jnp.concatenate([c, c_ctx[None, :], jnp.zeros((MOD_ROWS - B - 1, D), F32)], axis=0)
    mod = jnp.swapaxes(_adaln(cc, ada_w, ada_b), 1, 2)
    tabs_x = _rope_tables(L, True)
    tabs_c = _rope_tables(C, False)
    tm_x, tm_c, tq = 512, 256, 256

    h_ctx = ctx
    for layer in range(DEPTH):
        update_ctx = layer < DEPTH - 1
        lam_init = 0.8 - 0.6 * math.exp(-0.3 * layer)
        mod_l = mod[layer]
        qa, qb, qc, u, va, vb, vc, kb, kta, ktc = _inproj(layer, x, mod_l, None, tm_x, pw, tabs_x)
        cqa, cqb, cqc, cu, cva, cvb, cvc, ckb, ckta, cktc = _inproj(layer, h_ctx, mod_l, B, tm_c, pw, tabs_c)
        diff_extra = (diff_lambda, pw["subln"])

        oa = _dense_attn("mla", layer, qa, [kta, ckta], [va, cva], tq)
        ob = _swa(layer, swa_sink, qb, kb, vb, ckb, cvb, tq)
        oc = _dense_attn("diff", layer, qc, [ktc, cktc], [vc, cvc], tq, diff_extra, lam_init)
        x, hrx, rtx, cntx = _outproj(layer, oa, ob, oc, u, x, mod_l, None, tm_x, pw)
        if update_ctx:
            coa = _dense_attn("mla", layer, cqa, [ckta], [cva], tq)
            cob = _swa(layer, swa_sink, cqb, None, None, ckb, cvb, tq)
            coc = _dense_attn("diff", layer, cqc, [cktc], [cvc], tq, diff_extra, lam_init)
            h_ctx, hrc, rtc, cntc = _outproj(layer, coa, cob, coc, cu, h_ctx, mod_l, B, tm_c, pw)
        x = _moe(layer, hrx, rtx, cntx, x, mod_l, None, tm_x, pw)
        if update_ctx:
            h_ctx = _moe(layer, hrc, rtc, cntc, h_ctx, mod_l, B, tm_c, pw)
    return x
```

```python
import functools
import math

import jax
import jax.numpy as jnp
from jax import lax
from jax.experimental import pallas as pl
from jax.experimental.pallas import tpu as pltpu

F32 = jnp.float32
BF = jnp.bfloat16

D_MODEL = 1024
DEPTH = 2
GRID_W = 64
EPS = 1e-6
ROPE_BASE = 10000.0
N_MOD = 6

MLA_HEADS = 4
MLA_Q_RANK = 256
MLA_KV_RANK = 128
MLA_NOPE = 64
MLA_ROPE = 32
MLA_V = 64
MLA_QK = MLA_NOPE + MLA_ROPE
MLA_SCALE = MLA_QK ** -0.5

SWA_HEADS = 4
SWA_KV_HEADS = 2
SWA_HEAD_DIM = 64
SWA_WINDOW = 128
SWA_SCALE = SWA_HEAD_DIM ** -0.5

DIFF_HEADS = 4
DIFF_QK = 32
DIFF_V = 64
DIFF_SCALE = DIFF_QK ** -0.5

POOL_WINDOWS = (2, 4, 8, 16)
POOL_GROUP = 64
POOL_HALO = 8
MIX_D = len(POOL_WINDOWS) * POOL_GROUP

N_EXPERT_GROUPS = 4
EXPERTS_PER_GROUP = 4
N_EXPERTS = 16
D_EXPERT = 512
PAIR_A = (0, 2, 2, 3, 3, 3)
PAIR_B = (1, 1, 0, 0, 1, 2)
PAIRS_PER_GROUP = len(PAIR_A)
N_BUCKETS = N_EXPERT_GROUPS * PAIRS_PER_GROUP

LOG2E = math.log2(math.e)
NEG = -1e30
LANES = 128
MOD_ROWS = 24
SUBLANES = 8
TOK_TILE = (D_MODEL // LANES, LANES)
MOE_TM = 256
DMA_UNROLL = 8
KEY_CHUNK = 256
VMEM_LIMIT = 48 * 1024 * 1024

R_CQ, R_BQ, R_DQ, R_DV, R_U, R_CKV, R_BK, R_BV, R_END = 0, 256, 512, 768, 1024, 1280, 1408, 1536, 1664
T_CKV, T_KR, T_DK, T_END = 0, 128, 160, 416
P_QA, P_QB, P_KB, P_QC, P_END = 0, 384, 640, 768, 1024


def _dot(a, b):
    return jnp.dot(a, b, preferred_element_type=F32)


def _dot_nt(a, b):
    return lax.dot_general(a, b, (((1,), (1,)), ((), ())), preferred_element_type=F32)


def _rms(x, g):
    return x * lax.rsqrt(jnp.mean(x * x, axis=-1, keepdims=True) + EPS) * g


def _params(sem):
    return pltpu.CompilerParams(dimension_semantics=sem, vmem_limit_bytes=VMEM_LIMIT)


def _adaln_kernel(c_ref, w_ref, b_ref, o_ref):
    c = c_ref[...]
    s = (c * jax.nn.sigmoid(c)).astype(BF)
    o_ref[...] = _dot(s, w_ref[...].astype(BF)) + b_ref[...]


def _adaln(cc, ada_w, ada_b):
    return pl.pallas_call(
        _adaln_kernel,
        out_shape=jax.ShapeDtypeStruct((DEPTH, N_MOD, MOD_ROWS, D_MODEL), F32),
        grid=(DEPTH, N_MOD),
        in_specs=[
            pl.BlockSpec((MOD_ROWS, D_MODEL), lambda l, j: (0, 0)),
            pl.BlockSpec((None, D_MODEL, D_MODEL), lambda l, j: (l, 0, j)),
            pl.BlockSpec((None, None, 1, D_MODEL), lambda l, j: (l, j, 0, 0)),
        ],
        out_specs=pl.BlockSpec((None, None, MOD_ROWS, D_MODEL), lambda l, j: (l, j, 0, 0)),
        compiler_params=_params(("arbitrary", "arbitrary")),
        name="adaln",
    )(cc, ada_w, ada_b.reshape(DEPTH, N_MOD, 1, D_MODEL))


def _rope_rows(xf, c, s, m1, half):
    w = xf.shape[-1]
    up = pltpu.roll(xf, w - half, axis=1)
    dn = pltpu.roll(xf, half, axis=1)
    return xf * c + jnp.where(m1 > 0.5, up, dn) * s


def _rope_cols(xt, c, s):
    h = xt.shape[0] // 2
    x1, x2 = xt[:h], xt[h:]
    return jnp.concatenate([x1 * c - x2 * s, x1 * s + x2 * c], axis=0)


def _inproj_kernel(x_ref, mod_ref, ng_ref, wr_ref, wkt_ref, qg_ref, kvgr_ref, kvgc_ref,
                   wuq_ref, wukt_ref, wuv_ref, tc_ref, ts_ref, m_ref, ct_ref, st_ref,
                   qa_ref, qb_ref, qc_ref, u_ref, va_ref, vb_ref, vc_ref, kb_ref, kta_ref, ktc_ref):
    x = x_ref[...]
    mod = mod_ref[...]
    hx = _rms(x, ng_ref[0:1, :]) * (1.0 + mod[1:2, :]) + mod[0:1, :]
    hb = hx.astype(BF)
    zr = _dot(hb, wr_ref[...])
    zk = _dot_nt(wkt_ref[...], hb)

    tc = tc_ref[...]
    ts = ts_ref[...]
    m1 = m_ref[...]

    cqn = _rms(zr[:, R_CQ:R_CQ + MLA_Q_RANK], qg_ref[...]).astype(BF)
    qa = _dot(cqn, wuq_ref[...])
    qa_ref[...] = _rope_rows(qa, tc[:, P_QA:P_QB], ts[:, P_QA:P_QB], m1[:, P_QA:P_QB],
                             MLA_ROPE // 2).astype(BF)
    qb_ref[...] = _rope_rows(zr[:, R_BQ:R_DQ], tc[:, P_QB:P_KB], ts[:, P_QB:P_KB], m1[:, P_QB:P_KB],
                             SWA_HEAD_DIM // 2).astype(BF)
    qc_ref[...] = _rope_rows(zr[:, R_DQ:R_DV], tc[:, P_QC:P_END], ts[:, P_QC:P_END], m1[:, P_QC:P_END],
                             DIFF_QK // 2).astype(BF)
    kb_ref[...] = _rope_rows(zr[:, R_BK:R_BV], tc[:, P_KB:P_QC], ts[:, P_KB:P_QC], m1[:, P_KB:P_QC],
                             SWA_HEAD_DIM // 2).astype(BF)
    u_ref[...] = zr[:, R_U:R_CKV]
    vb_ref[...] = zr[:, R_BV:R_END].astype(BF)
    vc_ref[...] = zr[:, R_DV:R_U].astype(BF)

    ckvn = _rms(zr[:, R_CKV:R_BK], kvgr_ref[...]).astype(BF)
    va_ref[...] = _dot(ckvn, wuv_ref[...]).astype(BF)

    ckvt = zk[T_CKV:T_KR]
    ckvtn = ckvt * lax.rsqrt(jnp.mean(ckvt * ckvt, axis=0, keepdims=True) + EPS) * kvgc_ref[...]
    knt = _dot(wukt_ref[...], ckvtn.astype(BF))
    ct = ct_ref[...]
    st = st_ref[...]
    krt = _rope_cols(zk[T_KR:T_DK], ct, st)
    pieces = []
    for h in range(MLA_HEADS):
        pieces.append(knt[h * MLA_NOPE:(h + 1) * MLA_NOPE])
        pieces.append(krt)
    kta_ref[...] = jnp.concatenate(pieces, axis=0).astype(BF)

    dk = zk[T_DK:T_END]
    ktc_ref[...] = jnp.concatenate(
        [_rope_cols(dk[j * DIFF_QK:(j + 1) * DIFF_QK], ct, st) for j in range(2 * DIFF_HEADS)],
        axis=0).astype(BF)


def _inproj(layer, x, mod_l, mod_row, tm, pw, tabs):
    B, L, _ = x.shape
    nt = L // tm
    tab_c, tab_s, msk, cos_t, sin_t = tabs
    row = (lambda b: b) if mod_row is None else (lambda b: mod_row)

    def wspec(shape):
        nd = len(shape)
        return pl.BlockSpec((None,) + shape, lambda i, b: (layer,) + (0,) * nd)

    tok = lambda w: pl.BlockSpec((None, tm, w), lambda i, b: (b, i, 0))
    tokt = lambda w: pl.BlockSpec((None, w, tm), lambda i, b: (b, 0, i))
    in_specs = [
        tok(D_MODEL),
        pl.BlockSpec((None, N_MOD, D_MODEL), lambda i, b: (row(b), 0, 0)),
        wspec((4, D_MODEL)),
        wspec((D_MODEL, R_END)),
        wspec((T_END, D_MODEL)),
        wspec((1, MLA_Q_RANK)),
        wspec((1, MLA_KV_RANK)),
        wspec((MLA_KV_RANK, 1)),
        wspec((MLA_Q_RANK, MLA_HEADS * MLA_QK)),
        wspec((MLA_HEADS * MLA_NOPE, MLA_KV_RANK)),
        wspec((MLA_KV_RANK, MLA_HEADS * MLA_V)),
        pl.BlockSpec((tm, P_END), lambda i, b: (i, 0)),
        pl.BlockSpec((tm, P_END), lambda i, b: (i, 0)),
        pl.BlockSpec((1, P_END), lambda i, b: (0, 0)),
        pl.BlockSpec((MLA_ROPE // 2, tm), lambda i, b: (0, i)),
        pl.BlockSpec((MLA_ROPE // 2, tm), lambda i, b: (0, i)),
    ]
    widths = [(MLA_HEADS * MLA_QK, BF), (256, BF), (256, BF), (MIX_D, F32),
              (256, BF), (128, BF), (256, BF), (128, BF)]
    out_shape = [jax.ShapeDtypeStruct((B, L, w), dt) for w, dt in widths]
    out_specs = [tok(w) for w, _ in widths]
    out_shape += [jax.ShapeDtypeStruct((B, MLA_HEADS * MLA_QK, L), BF),
                  jax.ShapeDtypeStruct((B, 2 * DIFF_HEADS * DIFF_QK, L), BF)]
    out_specs += [tokt(MLA_HEADS * MLA_QK), tokt(2 * DIFF_HEADS * DIFF_QK)]
    return pl.pallas_call(
        _inproj_kernel,
        out_shape=out_shape,
        grid=(nt, B),
        in_specs=in_specs,
        out_specs=out_specs,
        compiler_params=_params(("arbitrary", "arbitrary")),
        name="inproj",
    )(x, mod_l, pw["norm_g"], pw["w_r"], pw["w_kt"], pw["q_gain"], pw["kv_gain_r"], pw["kv_gain_c"],
      pw["w_uq"], pw["w_ukt"], pw["w_uv"], tab_c, tab_s, msk, cos_t, sin_t)


def _softmax_pv(ss, vs, extra=None):
    mc = None
    for s in ss:
        for c in range(0, s.shape[1], KEY_CHUNK):
            blk = s[:, c:c + KEY_CHUNK]
            mc = blk if mc is None else jnp.maximum(mc, blk)
    m = jnp.max(mc, axis=-1, keepdims=True)
    if extra is not None:
        m = jnp.maximum(m, extra)
    lacc = None
    r = None
    for s, v in zip(ss, vs):
        for c in range(0, s.shape[1], KEY_CHUNK):
            e = jnp.exp2(s[:, c:c + KEY_CHUNK] - m)
            rs = _dot(e.astype(BF), v[c:c + KEY_CHUNK, :])
            lacc = e if lacc is None else lacc + e
            r = rs if r is None else r + rs
    l = jnp.sum(lacc, axis=-1, keepdims=True)
    if extra is not None:
        l = l + jnp.exp2(extra - m)
    return r / l


def _mla_kernel(*refs, nseg):
    q_ref = refs[0]
    kt_refs = refs[1:1 + nseg]
    v_refs = refs[1 + nseg:1 + 2 * nseg]
    o_ref = refs[1 + 2 * nseg]
    q = q_ref[...]
    lane = lax.broadcasted_iota(jnp.int32, (1, LANES), 1)
    blocks = []
    for p in range(MLA_HEADS // 2):
        pair = []
        for h in (2 * p, 2 * p + 1):
            sl = slice(h * MLA_QK, (h + 1) * MLA_QK)
            pair.append(_softmax_pv([_dot(q[:, sl], kt[sl, :]) for kt in kt_refs],
                                    [v[:, p * LANES:(p + 1) * LANES] for v in v_refs]))
        blocks.append(jnp.where(lane < MLA_V, pair[0], pair[1]))
    o_ref[...] = jnp.concatenate(blocks, axis=-1).astype(BF)


def _diff_kernel(*refs, nseg, lam_init):
    q_ref = refs[0]
    kt_refs = refs[1:1 + nseg]
    v_refs = refs[1 + nseg:1 + 2 * nseg]
    dl_ref, g_ref, o_ref = refs[1 + 2 * nseg:]
    q = q_ref[...]
    dl = dl_ref[...]
    lam = (jnp.exp(jnp.sum(dl[0:1] * dl[1:2], axis=-1, keepdims=True))
           - jnp.exp(jnp.sum(dl[2:3] * dl[3:4], axis=-1, keepdims=True)) + lam_init)
    lane = lax.broadcasted_iota(jnp.int32, (1, LANES), 1)
    lo = lane < DIFF_V
    blocks = []
    for p in range(DIFF_HEADS // 2):
        pair = []
        for h in (2 * p, 2 * p + 1):
            vh = [v[:, p * LANES:(p + 1) * LANES] for v in v_refs]
            comp = []
            for c in range(2):
                sl = slice((2 * h + c) * DIFF_QK, (2 * h + c + 1) * DIFF_QK)
                comp.append(_softmax_pv([_dot(q[:, sl], kt[sl, :]) for kt in kt_refs], vh))
            pair.append(comp[0] - lam * comp[1])
        o = jnp.where(lo, pair[0], pair[1])
        o2 = o * o
        ms_lo = jnp.sum(jnp.where(lo, o2, 0.0), axis=-1, keepdims=True) * (1.0 / DIFF_V)
        ms_hi = jnp.sum(jnp.where(lo, 0.0, o2), axis=-1, keepdims=True) * (1.0 / DIFF_V)
        ms = jnp.where(lo, ms_lo, ms_hi)
        blocks.append(o * lax.rsqrt(ms + EPS) * g_ref[...] * (1.0 - lam_init))
    o_ref[...] = jnp.concatenate(blocks, axis=-1).astype(BF)


def _dense_attn(kind, layer, q, kts, vs, tq, extra_in=(), lam_init=0.0):
    B, L, wq = q.shape
    nseg = len(kts)
    in_specs = [pl.BlockSpec((None, tq, wq), lambda b, i: (b, i, 0))]
    for kt in kts:
        in_specs.append(pl.BlockSpec((None,) + kt.shape[1:], lambda b, i: (b, 0, 0)))
    for v in vs:
        in_specs.append(pl.BlockSpec((None,) + v.shape[1:], lambda b, i: (b, 0, 0)))
    if kind == "mla":
        body = functools.partial(_mla_kernel, nseg=nseg)
    else:
        body = functools.partial(_diff_kernel, nseg=nseg, lam_init=lam_init)
        dl, g = extra_in
        in_specs.append(pl.BlockSpec((None, 4, DIFF_QK), lambda b, i: (layer, 0, 0)))
        in_specs.append(pl.BlockSpec((None, 1, LANES), lambda b, i: (layer, 0, 0)))
    return pl.pallas_call(
        body,
        out_shape=jax.ShapeDtypeStruct((B, L, 256), BF),
        grid=(B, L // tq),
        in_specs=in_specs,
        out_specs=pl.BlockSpec((None, tq, 256), lambda b, i: (b, i, 0)),
        compiler_params=_params(("arbitrary", "arbitrary")),
        name=kind,
    )(q, *kts, *vs, *extra_in)


def _swa_kernel(*refs, layer, tq, seq_len, local):
    if local:
        sink_ref, q_ref, kl_ref, vl_ref, kc_ref, vc_ref, o_ref = refs
    else:
        sink_ref, q_ref, kc_ref, vc_ref, o_ref = refs
    i = pl.program_id(1)
    q = q_ref[...]
    lane = lax.broadcasted_iota(jnp.int32, (1, LANES), 1)
    rows = lax.broadcasted_iota(jnp.int32, (2 * tq, 1), 0)
    top = rows < tq
    if local:
        win = tq + 2 * SWA_WINDOW
        q0 = i * tq
        ws = pl.multiple_of(jnp.clip(q0 - SWA_WINDOW, 0, seq_len - win), LANES)
        kwin = kl_ref[pl.ds(ws, win), :]
        vwin = vl_ref[pl.ds(ws, win), :]
        qpos = q0 + jnp.where(top, rows, rows - tq)
        kpos = ws + lax.broadcasted_iota(jnp.int32, (1, win), 1)
        band = jnp.abs(kpos - qpos) <= SWA_WINDOW
    blocks = []
    for k in range(SWA_KV_HEADS):
        q2 = jnp.concatenate([q[:, (2 * k) * SWA_HEAD_DIM:(2 * k + 1) * SWA_HEAD_DIM],
                              q[:, (2 * k + 1) * SWA_HEAD_DIM:(2 * k + 2) * SWA_HEAD_DIM]], axis=0)
        sl = slice(k * SWA_HEAD_DIM, (k + 1) * SWA_HEAD_DIM)
        sk = jnp.where(top, sink_ref[layer, 2 * k], sink_ref[layer, 2 * k + 1]) * LOG2E
        ss, vs = [_dot_nt(q2, kc_ref[:, sl])], [vc_ref[...]]
        if local:
            ss.append(jnp.where(band, _dot_nt(q2, kwin[:, sl]), NEG))
            vs.append(vwin)
        r = _softmax_pv(ss, vs, extra=sk)
        r_top, r_bot = r[:tq], r[tq:]
        if k == 0:
            r_bot = pltpu.roll(r_bot, SWA_HEAD_DIM, axis=1)
        else:
            r_top = pltpu.roll(r_top, SWA_HEAD_DIM, axis=1)
        blocks.append(jnp.where(lane < SWA_HEAD_DIM, r_top, r_bot))
    o_ref[...] = jnp.concatenate(blocks, axis=-1).astype(BF)


def _swa(layer, sink, q, k_loc, v_loc, k_ctx, v_ctx, tq):
    B, L, _ = q.shape
    C = k_ctx.shape[1]
    local = k_loc is not None
    full = lambda n: pl.BlockSpec((None, n, 128), lambda b, i: (b, 0, 0))
    in_specs = [pl.BlockSpec(memory_space=pltpu.SMEM),
                pl.BlockSpec((None, tq, 256), lambda b, i: (b, i, 0))]
    args = [sink, q]
    if local:
        in_specs += [full(L), full(L)]
        args += [k_loc, v_loc]
    in_specs += [full(C), full(C)]
    args += [k_ctx, v_ctx]
    return pl.pallas_call(
        functools.partial(_swa_kernel, layer=layer, tq=tq, seq_len=L, local=local),
        out_shape=jax.ShapeDtypeStruct((B, L, 256), BF),
        grid=(B, L // tq),
        in_specs=in_specs,
        out_specs=pl.BlockSpec((None, tq, 256), lambda b, i: (b, i, 0)),
        compiler_params=_params(("arbitrary", "arbitrary")),
        name="swa",
    )(*args)


def _pool_diffs(u, prev, nxt, i, nt, tm, seq_len):
    n = tm + 2 * POOL_HALO
    ue = jnp.concatenate([jnp.where(i > 0, prev, 0.0), u, jnp.where(i < nt - 1, nxt, 0.0)], axis=0)

    def sh(a, k):
        return pltpu.roll(a, k % n, axis=0)

    s2 = ue + sh(ue, 1)
    s4 = sh(s2, 1) + sh(s2, -1)
    s8 = sh(s4, 2) + sh(s4, -2)
    s16 = sh(s8, 4) + sh(s8, -4)
    lane = lax.broadcasted_iota(jnp.int32, (1, MIX_D), 1)
    t = i * tm + lax.broadcasted_iota(jnp.int32, (tm, 1), 0)
    win = None
    cnt = None
    for gi, (w, s) in enumerate(zip(POOL_WINDOWS, (s2, s4, s8, s16))):
        half = w // 2
        c = (jnp.minimum(t - half + w, seq_len) - jnp.maximum(t - half, 0)).astype(F32)
        sv = s[POOL_HALO:POOL_HALO + tm]
        if win is None:
            win, cnt = sv, c
        else:
            sel = lane >= gi * POOL_GROUP
            win = jnp.where(sel, sv, win)
            cnt = jnp.where(sel, c, cnt)
    return win / cnt - u


def _outproj_kernel(oa_ref, ob_ref, oc_ref, u_ref, up_ref, un_ref, x_ref, mod_ref, ng_ref,
                    wo_ref, pw_ref, ps_ref, rwh_ref, rwl_ref, rb_ref,
                    xo_ref, hr_ref, rt_ref, cnt_ref, carry_ref, *, tm, seq_len, nt):
    i = pl.program_id(1)

    @pl.when((pl.program_id(0) == 0) & (i == 0))
    def _():
        carry_ref[...] = jnp.zeros_like(carry_ref)

    u = u_ref[...]
    dg = _pool_diffs(u, up_ref[...], un_ref[...], i, nt, tm, seq_len)
    od = _dot(dg.astype(BF), pw_ref[...]) * ps_ref[...]
    mix = jnp.concatenate([oa_ref[...], ob_ref[...], oc_ref[...], od.astype(BF)], axis=-1)
    ox = _dot(mix, wo_ref[...])
    mod = mod_ref[...]
    x = x_ref[...] + mod[2:3, :] * _rms(ox, ng_ref[1:2, :])
    xo_ref[...] = x
    h = _rms(x, ng_ref[2:3, :]) * (1.0 + mod[4:5, :]) + mod[3:4, :]
    hr_ref[...] = h.reshape((tm,) + TOK_TILE)
    hh = h.astype(BF)
    hl = (h - hh.astype(F32)).astype(BF)
    wh = rwh_ref[...]
    logits = _dot(hh, wh) + _dot(hl, wh) + _dot(hh, rwl_ref[...]) + rb_ref[...]

    lane = lax.broadcasted_iota(jnp.int32, (1, LANES), 1)
    lanef = lane.astype(F32)
    is_g = (lane >= N_EXPERTS) & (lane < N_EXPERTS + N_EXPERT_GROUPS)
    gl = jnp.where(is_g, logits, NEG)
    gmax = jnp.max(gl, axis=-1, keepdims=True)
    gsel = jnp.min(jnp.where(gl == gmax, lanef, 1e9), axis=-1, keepdims=True) - N_EXPERTS
    p_g = 1.0 / jnp.sum(jnp.where(is_g, jnp.exp(gl - gmax), 0.0), axis=-1, keepdims=True)
    grp = jnp.right_shift(lane, 2).astype(F32)
    el = jnp.where((lane < N_EXPERTS) & (grp == gsel), logits, NEG)
    v1 = jnp.max(el, axis=-1, keepdims=True)
    i1 = jnp.min(jnp.where(el == v1, lanef, 1e9), axis=-1, keepdims=True)
    el2 = jnp.where(lanef == i1, NEG, el)
    v2 = jnp.max(el2, axis=-1, keepdims=True)
    i2 = jnp.min(jnp.where(el2 == v2, lanef, 1e9), axis=-1, keepdims=True)
    e21 = jnp.exp(v2 - v1)
    w1 = p_g / (1.0 + e21)
    w2 = w1 * e21

    l1 = i1 - EXPERTS_PER_GROUP * gsel
    l2 = i2 - EXPERTS_PER_GROUP * gsel
    code = jnp.minimum(l1, l2) * EXPERTS_PER_GROUP + jnp.maximum(l1, l2)
    pidx = jnp.full_like(code, float(PAIRS_PER_GROUP - 1))
    slot_a = jnp.full_like(code, float(PAIR_A[-1]))
    for p in range(PAIRS_PER_GROUP - 1):
        lo_e, hi_e = min(PAIR_A[p], PAIR_B[p]), max(PAIR_A[p], PAIR_B[p])
        hit = code == float(lo_e * EXPERTS_PER_GROUP + hi_e)
        pidx = jnp.where(hit, float(p), pidx)
        slot_a = jnp.where(hit, float(PAIR_A[p]), slot_a)
    first_is_a = l1 == slot_a
    w_a = jnp.where(first_is_a, w1, w2)
    w_b = jnp.where(first_is_a, w2, w1)
    bucket = gsel * PAIRS_PER_GROUP + pidx

    onehot = lanef == bucket
    rr = lax.broadcasted_iota(jnp.int32, (tm, tm), 0)
    cc = lax.broadcasted_iota(jnp.int32, (tm, tm), 1)
    tri = jnp.where(rr >= cc, 1.0, 0.0).astype(BF)
    prefix = _dot(tri, jnp.where(onehot, 1.0, 0.0).astype(BF))
    carry = carry_ref[...]
    rank = jnp.sum(jnp.where(onehot, prefix - 1.0 + carry, 0.0), axis=-1, keepdims=True)
    carry = carry + prefix[tm - 1:tm, :]
    carry_ref[...] = carry
    cnt_ref[...] = carry
    rt_ref[...] = jnp.where(lane == 0, w_a, jnp.where(lane == 1, w_b, jnp.where(
        lane == 2, bucket, jnp.where(lane == 3, rank, 0.0))))


def _outproj(layer, oa, ob, oc, u, x, mod_l, mod_row, tm, pw):
    B, L, _ = x.shape
    nt = L // tm
    hb = tm // POOL_HALO
    nhb = L // POOL_HALO
    row = (lambda b: b) if mod_row is None else (lambda b: mod_row)

    def wspec(shape):
        nd = len(shape)
        return pl.BlockSpec((None,) + shape, lambda b, i: (layer,) + (0,) * nd)

    tok = lambda w: pl.BlockSpec((None, tm, w), lambda b, i: (b, i, 0))
    in_specs = [
        tok(256), tok(256), tok(256), tok(MIX_D),
        pl.BlockSpec((None, POOL_HALO, MIX_D), lambda b, i: (b, jnp.maximum(i * hb - 1, 0), 0)),
        pl.BlockSpec((None, POOL_HALO, MIX_D), lambda b, i: (b, jnp.minimum((i + 1) * hb, nhb - 1), 0)),
        tok(D_MODEL),
        pl.BlockSpec((None, N_MOD, D_MODEL), lambda b, i: (row(b), 0, 0)),
        wspec((4, D_MODEL)),
        wspec((D_MODEL, D_MODEL)),
        wspec((MIX_D, MIX_D)),
        wspec((1, MIX_D)),
        wspec((D_MODEL, LANES)),
        wspec((D_MODEL, LANES)),
        wspec((1, LANES)),
    ]
    return pl.pallas_call(
        functools.partial(_outproj_kernel, tm=tm, seq_len=L, nt=nt),
        out_shape=[jax.ShapeDtypeStruct((B, L, D_MODEL), F32),
                   jax.ShapeDtypeStruct((B * L,) + TOK_TILE, F32),
                   jax.ShapeDtypeStruct((B, L, LANES), F32),
                   jax.ShapeDtypeStruct((1, LANES), F32)],
        grid=(B, nt),
        in_specs=in_specs,
        out_specs=[tok(D_MODEL),
                   pl.BlockSpec((tm,) + TOK_TILE, lambda b, i: (b * nt + i, 0, 0)),
                   tok(LANES), pl.BlockSpec((1, LANES), lambda b, i: (0, 0))],
        scratch_shapes=[pltpu.VMEM((1, LANES), F32)],
        compiler_params=_params(("arbitrary", "arbitrary")),
        name="outproj",
    )(oa, ob, oc, u, u, u, x, mod_l, pw["norm_g"], pw["w_out"], pw["pool_bd"], pw["pool_scale"],
      pw["r_hi"], pw["r_lo"], pw["r_b"])


def _route_tables(route, counts, n_tiles):
    T = route.shape[0] * route.shape[1]
    bucket = route[:, :, 2].reshape(T).astype(jnp.int32)
    rank = route[:, :, 3].reshape(T).astype(jnp.int32)
    cnt = counts[0, :N_BUCKETS].astype(jnp.int32)
    tiles_b = (cnt + MOE_TM - 1) // MOE_TM
    ends = jnp.cumsum(tiles_b)
    starts = ends - tiles_b
    pos = jnp.take(starts * MOE_TM, bucket) + rank
    n_used = ends[-1]
    tile = jnp.arange(n_tiles, dtype=jnp.int32)
    src = jnp.minimum(tile, n_used - 1)
    tb = jnp.sum((src[:, None] >= ends[None, :]).astype(jnp.int32), axis=1)
    tb = jnp.minimum(tb, N_BUCKETS - 1)
    grp, pidx = tb // PAIRS_PER_GROUP, tb % PAIRS_PER_GROUP
    ea = grp * EXPERTS_PER_GROUP + jnp.take(jnp.array(PAIR_A, jnp.int32), pidx)
    eb = grp * EXPERTS_PER_GROUP + jnp.take(jnp.array(PAIR_B, jnp.int32), pidx)
    zlast = jnp.where(tiles_b > 0, (ends - 1) * MOE_TM, -1)
    idle = n_used + jnp.arange(N_BUCKETS, dtype=jnp.int32)
    zidle = jnp.where(idle < n_tiles, idle * MOE_TM, -1)
    zrow = jnp.concatenate([zlast, zidle]).astype(jnp.int32)
    return pos, zrow, src, ea, eb, n_used.reshape(1).astype(jnp.int32)


def _row_copy(src_ref, src_row, dst_ref, dst_row, sem):
    return pltpu.make_async_copy(src_ref.at[src_row], dst_ref.at[dst_row], sem)


def _scatter_kernel(pos_ref, zrow_ref, hr_ref, rt_ref, hs_ref, ws_ref, zbuf_ref, zw_ref, sem, zsem, *, tm):
    i = pl.program_id(0)

    @pl.when(i == 0)
    def _():
        zbuf_ref[...] = jnp.zeros_like(zbuf_ref)
        zw_ref[...] = jnp.zeros_like(zw_ref)
        for start in (True, False):
            for b in range(2 * N_BUCKETS):
                @pl.when(zrow_ref[b] >= 0)
                def _():
                    z0 = pl.multiple_of(zrow_ref[b], MOE_TM)
                    for cp in (pltpu.make_async_copy(zbuf_ref, hs_ref.at[pl.ds(z0, MOE_TM)], zsem),
                               pltpu.make_async_copy(zw_ref, ws_ref.at[pl.ds(z0, MOE_TM)], zsem)):
                        cp.start() if start else cp.wait()

    def copies(r):
        p = pos_ref[i * tm + r]
        return (_row_copy(hr_ref, r, hs_ref, p, sem),
                pltpu.make_async_copy(rt_ref.at[pl.ds(r, 1)], ws_ref.at[pl.ds(p, 1)], sem))

    _row_dma_loop(tm, copies)


def _row_dma_loop(n, copies):
    def issue(g, carry):
        for k in range(DMA_UNROLL):
            for cp in copies(g * DMA_UNROLL + k):
                cp.start(priority=k % 2)
        return carry

    lax.fori_loop(0, n // DMA_UNROLL, issue, 0)

    def drain(g, carry):
        for k in range(DMA_UNROLL):
            for cp in copies(g * DMA_UNROLL + k):
                cp.wait()
        return carry

    lax.fori_loop(0, n // DMA_UNROLL, drain, 0)


def _scatter_rows(hr, route, pos, zrow, n_tiles, tm):
    T = hr.shape[0]
    return pl.pallas_call(
        functools.partial(_scatter_kernel, tm=tm),
        out_shape=[jax.ShapeDtypeStruct((n_tiles * MOE_TM,) + TOK_TILE, F32),
                   jax.ShapeDtypeStruct((n_tiles * MOE_TM, LANES), F32)],
        grid_spec=pltpu.PrefetchScalarGridSpec(
            num_scalar_prefetch=2,
            grid=(T // tm,),
            in_specs=[pl.BlockSpec((tm,) + TOK_TILE, lambda i, pos, zrow: (i, 0, 0)),
                      pl.BlockSpec((tm, LANES), lambda i, pos, zrow: (i, 0))],
            out_specs=[pl.BlockSpec(memory_space=pl.ANY), pl.BlockSpec(memory_space=pl.ANY)],
            scratch_shapes=[pltpu.VMEM((MOE_TM,) + TOK_TILE, F32), pltpu.VMEM((MOE_TM, LANES), F32),
                            pltpu.SemaphoreType.DMA(()), pltpu.SemaphoreType.DMA(())]),
        compiler_params=_params(("arbitrary",)),
        name="moe_scatter",
    )(pos, zrow, hr, route)


def _expert(h, wgu, wd):
    gu = _dot(h, wgu)
    g = gu[:, :D_EXPERT]
    act = (g * jax.nn.sigmoid(g)) * gu[:, D_EXPERT:]
    return _dot(act.astype(BF), wd)


def _moe_kernel(src_ref, ea_ref, eb_ref, nu_ref, hs_ref, ws_ref, wgua_ref, wgub_ref, wda_ref, wdb_ref, o_ref):
    j = pl.program_id(0)

    @pl.when(j < nu_ref[0])
    def _():
        h = hs_ref[...].reshape(MOE_TM, D_MODEL).astype(BF)
        w = ws_ref[...]
        y = (w[:, 0:1] * _expert(h, wgua_ref[...], wda_ref[...])
             + w[:, 1:2] * _expert(h, wgub_ref[...], wdb_ref[...]))
        o_ref[...] = y.reshape((MOE_TM,) + TOK_TILE)

    @pl.when(j >= nu_ref[0])
    def _():
        o_ref[...] = jnp.zeros_like(o_ref)


def _moe_sorted(layer, hs, ws, src, ea, eb, n_used, pw):
    n_tiles = hs.shape[0] // MOE_TM
    wgu = lambda sel: pl.BlockSpec((None, None, D_MODEL, 2 * D_EXPERT),
                                   lambda j, src, ea, eb, nu: (layer, (ea, eb)[sel][j], 0, 0))
    wd = lambda sel: pl.BlockSpec((None, None, D_EXPERT, D_MODEL),
                                  lambda j, src, ea, eb, nu: (layer, (ea, eb)[sel][j], 0, 0))
    return pl.pallas_call(
        _moe_kernel,
        out_shape=jax.ShapeDtypeStruct((n_tiles * MOE_TM,) + TOK_TILE, F32),
        grid_spec=pltpu.PrefetchScalarGridSpec(
            num_scalar_prefetch=4,
            grid=(n_tiles,),
            in_specs=[pl.BlockSpec((MOE_TM,) + TOK_TILE, lambda j, src, ea, eb, nu: (src[j], 0, 0)),
                      pl.BlockSpec((MOE_TM, LANES), lambda j, src, ea, eb, nu: (src[j], 0)),
                      wgu(0), wgu(1), wd(0), wd(1)],
            out_specs=pl.BlockSpec((MOE_TM,) + TOK_TILE, lambda j, src, ea, eb, nu: (j, 0, 0))),
        compiler_params=_params(("arbitrary",)),
        name="moe",
    )(src, ea, eb, n_used, hs, ws, pw["w_gu"], pw["w_gu"], pw["w_down"], pw["w_down"])


def _unsort_kernel(pos_ref, ys_ref, x_ref, mod_ref, ng_ref, o_ref, buf_ref, sem, *, tm):
    i = pl.program_id(0)
    _row_dma_loop(tm, lambda r: (_row_copy(ys_ref, pos_ref[i * tm + r], buf_ref, r, sem),))
    fx = buf_ref[...].reshape(tm, D_MODEL)
    o_ref[...] = x_ref[...] + mod_ref[5:6, :] * _rms(fx, ng_ref[3:4, :])


def _unsort_residual(layer, ys, pos, x, mod_l, mod_row, tm, pw):
    B, L, _ = x.shape
    T = B * L
    tiles_per_seq = L // tm
    row = (lambda i: i // tiles_per_seq) if mod_row is None else (lambda i: mod_row)
    out = pl.pallas_call(
        functools.partial(_unsort_kernel, tm=tm),
        out_shape=jax.ShapeDtypeStruct((T, D_MODEL), F32),
        grid_spec=pltpu.PrefetchScalarGridSpec(
            num_scalar_prefetch=1,
            grid=(T // tm,),
            in_specs=[pl.BlockSpec(memory_space=pl.ANY),
                      pl.BlockSpec((tm, D_MODEL), lambda i, pos: (i, 0)),
                      pl.BlockSpec((None, N_MOD, D_MODEL), lambda i, pos: (row(i), 0, 0)),
                      pl.BlockSpec((None, 4, D_MODEL), lambda i, pos: (layer, 0, 0))],
            out_specs=pl.BlockSpec((tm, D_MODEL), lambda i, pos: (i, 0)),
            scratch_shapes=[pltpu.VMEM((tm,) + TOK_TILE, F32), pltpu.SemaphoreType.DMA(())]),
        compiler_params=_params(("arbitrary",)),
        name="moe_unsort",
    )(pos, ys, x.reshape(T, D_MODEL), mod_l, pw["norm_g"])
    return out.reshape(B, L, D_MODEL)


def _moe(layer, hr, route, counts, x, mod_l, mod_row, tm, pw):
    B, L, _ = x.shape
    T = B * L
    n_tiles = T // MOE_TM + N_BUCKETS
    pos, zrow, src, ea, eb, n_used = _route_tables(route, counts, n_tiles)
    hs, ws = _scatter_rows(hr, route.reshape(T, LANES), pos, zrow, n_tiles, tm)
    ys = _moe_sorted(layer, hs, ws, src, ea, eb, n_used, pw)
    return _unsort_residual(layer, ys, pos, x, mod_l, mod_row, tm, pw)


def _axial(length, dim):
    rows = length // GRID_W
    row = jnp.repeat(jnp.arange(rows), GRID_W).astype(F32)
    col = jnp.tile(jnp.arange(GRID_W), rows).astype(F32)
    n = dim // 4
    inv = ROPE_BASE ** (-jnp.arange(n, dtype=F32) / n)
    ang = jnp.concatenate([row[:, None] * inv, col[:, None] * inv], axis=-1)
    return jnp.cos(ang), jnp.sin(ang)


def _rope_tables(length, rotate):
    if rotate:
        cos_a, sin_a = _axial(length, MLA_ROPE)
        cos_b, sin_b = _axial(length, SWA_HEAD_DIM)
    else:
        cos_a, sin_a = jnp.ones((length, MLA_ROPE // 2), F32), jnp.zeros((length, MLA_ROPE // 2), F32)
        cos_b, sin_b = jnp.ones((length, SWA_HEAD_DIM // 2), F32), jnp.zeros((length, SWA_HEAD_DIM // 2), F32)
    one = jnp.ones((length, MLA_NOPE), F32)
    zero = jnp.zeros((length, MLA_NOPE), F32)
    sa, sb, sc = MLA_SCALE * LOG2E, SWA_SCALE * LOG2E, DIFF_SCALE * LOG2E
    c_qa = jnp.tile(jnp.concatenate([one, cos_a, cos_a], -1), (1, MLA_HEADS)) * sa
    s_qa = jnp.tile(jnp.concatenate([zero, -sin_a, sin_a], -1), (1, MLA_HEADS)) * sa
    c_b = jnp.concatenate([cos_b, cos_b], -1)
    s_b = jnp.concatenate([-sin_b, sin_b], -1)
    c_c = jnp.tile(jnp.concatenate([cos_a, cos_a], -1), (1, 2 * DIFF_HEADS)) * sc
    s_c = jnp.tile(jnp.concatenate([-sin_a, sin_a], -1), (1, 2 * DIFF_HEADS)) * sc
    tab_c = jnp.concatenate([c_qa, jnp.tile(c_b, (1, SWA_HEADS)) * sb, jnp.tile(c_b, (1, SWA_KV_HEADS)), c_c], -1)
    tab_s = jnp.concatenate([s_qa, jnp.tile(s_b, (1, SWA_HEADS)) * sb, jnp.tile(s_b, (1, SWA_KV_HEADS)), s_c], -1)
    m_qa = jnp.tile(jnp.concatenate([jnp.zeros((MLA_NOPE,)), jnp.ones((MLA_ROPE // 2,)),
                                     jnp.zeros((MLA_ROPE // 2,))]), MLA_HEADS)
    m_b = jnp.concatenate([jnp.ones((SWA_HEAD_DIM // 2,)), jnp.zeros((SWA_HEAD_DIM // 2,))])
    m_c = jnp.concatenate([jnp.ones((DIFF_QK // 2,)), jnp.zeros((DIFF_QK // 2,))])
    msk = jnp.concatenate([m_qa, jnp.tile(m_b, SWA_HEADS + SWA_KV_HEADS), jnp.tile(m_c, 2 * DIFF_HEADS)])
    return tab_c, tab_s, msk.astype(F32)[None, :], cos_a.T, sin_a.T


def _prep_weights(norm_g, w_in, mla_q_gain, mla_kv_gain, mla_w_uq, mla_w_ukv, diff_subln, pool_w,
                  pool_scale, w_out, router_gw, router_gb, router_ew, router_eb, exp_w_gu, exp_w_down):
    sizes = (MLA_Q_RANK, MLA_KV_RANK, MLA_ROPE, 256, 128, 128, 256, 256, 256, MIX_D)
    offs = [0]
    for s in sizes:
        offs.append(offs[-1] + s)
    cq, ckv, kr, bq, bk, bv, dq, dk, dv, u = [w_in[:, :, offs[j]:offs[j + 1]] for j in range(len(sizes))]
    w_r = jnp.concatenate([cq, bq, dq, dv, u, ckv, bk, bv], axis=-1).astype(BF)
    w_kt = jnp.swapaxes(jnp.concatenate([ckv, kr, dk], axis=-1), 1, 2).astype(BF)
    ukv = mla_w_ukv.reshape(DEPTH, MLA_KV_RANK, MLA_HEADS, MLA_NOPE + MLA_V)
    w_uk = ukv[..., :MLA_NOPE].reshape(DEPTH, MLA_KV_RANK, MLA_HEADS * MLA_NOPE)
    w_uv = ukv[..., MLA_NOPE:].reshape(DEPTH, MLA_KV_RANK, MLA_HEADS * MLA_V)
    eye = jnp.eye(len(POOL_WINDOWS), dtype=F32)
    pool_bd = jnp.einsum('lgcd,gh->lgchd', pool_w, eye).reshape(DEPTH, MIX_D, MIX_D)
    r_w = jnp.concatenate([router_ew, router_gw], axis=-1)
    r_w = jnp.pad(r_w, ((0, 0), (0, 0), (0, LANES - r_w.shape[-1])))
    r_hi = r_w.astype(BF)
    r_lo = (r_w - r_hi.astype(F32)).astype(BF)
    r_b = jnp.concatenate([router_eb, router_gb], axis=-1)
    r_b = jnp.pad(r_b, ((0, 0), (0, LANES - r_b.shape[-1])))[:, None, :]
    return dict(
        norm_g=norm_g, w_r=w_r, w_kt=w_kt,
        q_gain=mla_q_gain[:, None, :], kv_gain_r=mla_kv_gain[:, None, :], kv_gain_c=mla_kv_gain[:, :, None],
        w_uq=mla_w_uq.astype(BF), w_ukt=jnp.swapaxes(w_uk, 1, 2).astype(BF), w_uv=w_uv.astype(BF),
        subln=jnp.tile(diff_subln, (1, LANES // DIFF_V))[:, None, :],
        pool_bd=pool_bd.astype(BF), pool_scale=pool_scale[:, None, :], w_out=w_out.astype(BF),
        r_hi=r_hi, r_lo=r_lo, r_b=r_b, w_gu=exp_w_gu.astype(BF), w_down=exp_w_down.astype(BF))


def kernel(x, c, ctx, c_ctx, ada_w, ada_b, norm_g, w_in, mla_q_gain, mla_kv_gain, mla_w_uq, mla_w_ukv,
           swa_sink, diff_lambda, diff_subln, pool_w, pool_scale, w_out, router_gw, router_gb,
           router_ew, router_eb, exp_w_gu, exp_w_down):
    B, L, D = x.shape
    C = ctx.shape[1]
    assert D == D_MODEL and B + 1 <= MOD_ROWS and L % 512 == 0 and C % 256 == 0
    pw = _prep_weights(norm_g, w_in, mla_q_gain, mla_kv_gain, mla_w_uq, mla_w_ukv, diff_subln, pool_w,
                       pool_scale, w_out, router_gw, router_gb, router_ew, router_eb, exp_w_gu, exp_w_down)
    cc = jnp.concatenate([c, c_ctx[None, :], jnp.zeros((MOD_ROWS - B - 1, D), F32)], axis=0)
    mod = jnp.swapaxes(_adaln(cc, ada_w, ada_b), 1, 2)
    tabs_x = _rope_tables(L, True)
    tabs_c = _rope_tables(C, False)
    tm_x, tm_c, tq = 512, 256, 256

    h_ctx = ctx
    for layer in range(DEPTH):
        update_ctx = layer < DEPTH - 1
        lam_init = 0.8 - 0.6 * math.exp(-0.3 * layer)
        mod_l = mod[layer]
        qa, qb, qc, u, va, vb, vc, kb, kta, ktc = _inproj(layer, x, mod_l, None, tm_x, pw, tabs_x)
        cqa, cqb, cqc, cu, cva, cvb, cvc, ckb, ckta, cktc = _inproj(layer, h_ctx, mod_l, B, tm_c, pw, tabs_c)
        diff_extra = (diff_lambda, pw["subln"])

        oa = _dense_attn("mla", layer, qa, [kta, ckta], [va, cva], tq)
        ob = _swa(layer, swa_sink, qb, kb, vb, ckb, cvb, tq)
        oc = _dense_attn("diff", layer, qc, [ktc, cktc], [vc, cvc], tq, diff_extra, lam_init)
        x, hrx, rtx, cntx = _outproj(layer, oa, ob, oc, u, x, mod_l, None, tm_x, pw)
        if update_ctx:
            coa = _dense_attn("mla", layer, cqa, [ckta], [cva], tq)
            cob = _swa(layer, swa_sink, cqb, None, None, ckb, cvb, tq)
            coc = _dense_attn("diff", layer, cqc, [cktc], [cvc], tq, diff_extra, lam_init)
            h_ctx, hrc, rtc, cntc = _outproj(layer, coa, cob, coc, cu, h_ctx, mod_l, B, tm_c, pw)
        x = _moe(layer, hrx, rtx, cntx, x, mod_l, None, tm_x, pw)
        if update_ctx:
            h_ctx = _moe(layer, hrc, rtc, cntc, h_ctx, mod_l, B, tm_c, pw)
    return x
```

```python
import functools
import math

import jax
import jax.numpy as jnp
from jax import lax
from jax.experimental import pallas as pl
from jax.experimental.pallas import tpu as pltpu

F32 = jnp.float32
BF = jnp.bfloat16

D_MODEL = 1024
DEPTH = 2
GRID_W = 64
EPS = 1e-6
ROPE_BASE = 10000.0
N_MOD = 6

MLA_HEADS = 4
MLA_Q_RANK = 256
MLA_KV_RANK = 128
MLA_NOPE = 64
MLA_ROPE = 32
MLA_V = 64
MLA_QK = MLA_NOPE + MLA_ROPE
MLA_SCALE = MLA_QK ** -0.5

SWA_HEADS = 4
SWA_KV_HEADS = 2
SWA_HEAD_DIM = 64
SWA_WINDOW = 128
SWA_SCALE = SWA_HEAD_DIM ** -0.5

DIFF_HEADS = 4
DIFF_QK = 32
DIFF_V = 64
DIFF_SCALE = DIFF_QK ** -0.5

POOL_WINDOWS = (2, 4, 8, 16)
POOL_GROUP = 64
POOL_HALO = 8
MIX_D = len(POOL_WINDOWS) * POOL_GROUP

N_EXPERT_GROUPS = 4
EXPERTS_PER_GROUP = 4
N_EXPERTS = 16
D_EXPERT = 512
PAIR_A = (0, 2, 2, 3, 3, 3)
PAIR_B = (1, 1, 0, 0, 1, 2)
PAIRS_PER_GROUP = len(PAIR_A)
N_BUCKETS = N_EXPERT_GROUPS * PAIRS_PER_GROUP

LOG2E = math.log2(math.e)
NEG = -1e30
LANES = 128
MOD_ROWS = 24
SUBLANES = 8
TOK_TILE = (D_MODEL // LANES, LANES)
MOE_TM = 256
DMA_UNROLL = 8
KEY_CHUNK = 256
VMEM_LIMIT = 48 * 1024 * 1024

R_CQ, R_BQ, R_DQ, R_DV, R_U, R_CKV, R_BK, R_BV, R_END = 0, 256, 512, 768, 1024, 1280, 1408, 1536, 1664
T_CKV, T_KR, T_DK, T_END = 0, 128, 160, 416
P_QA, P_QB, P_KB, P_QC, P_END = 0, 384, 640, 768, 1024


def _dot(a, b):
    return jnp.dot(a, b, preferred_element_type=F32)


def _dot_nt(a, b):
    return lax.dot_general(a, b, (((1,), (1,)), ((), ())), preferred_element_type=F32)


def _rms(x, g):
    return x * lax.rsqrt(jnp.mean(x * x, axis=-1, keepdims=True) + EPS) * g


def _params(sem):
    return pltpu.CompilerParams(dimension_semantics=sem, vmem_limit_bytes=VMEM_LIMIT)


def _adaln_kernel(c_ref, w_ref, b_ref, o_ref):
    c = c_ref[...]
    s = (c * jax.nn.sigmoid(c)).astype(BF)
    o_ref[...] = _dot(s, w_ref[...].astype(BF)) + b_ref[...]


def _adaln(cc, ada_w, ada_b):
    return pl.pallas_call(
        _adaln_kernel,
        out_shape=jax.ShapeDtypeStruct((DEPTH, N_MOD, MOD_ROWS, D_MODEL), F32),
        grid=(DEPTH, N_MOD),
        in_specs=[
            pl.BlockSpec((MOD_ROWS, D_MODEL), lambda l, j: (0, 0)),
            pl.BlockSpec((None, D_MODEL, D_MODEL), lambda l, j: (l, 0, j)),
            pl.BlockSpec((None, None, 1, D_MODEL), lambda l, j: (l, j, 0, 0)),
        ],
        out_specs=pl.BlockSpec((None, None, MOD_ROWS, D_MODEL), lambda l, j: (l, j, 0, 0)),
        compiler_params=_params(("arbitrary", "arbitrary")),
        name="adaln",
    )(cc, ada_w, ada_b.reshape(DEPTH, N_MOD, 1, D_MODEL))


def _rope_rows(xf, c, s, m1, half):
    w = xf.shape[-1]
    up = pltpu.roll(xf, w - half, axis=1)
    dn = pltpu.roll(xf, half, axis=1)
    return xf * c + jnp.where(m1 > 0.5, up, dn) * s


def _rope_cols(xt, c, s):
    h = xt.shape[0] // 2
    x1, x2 = xt[:h], xt[h:]
    return jnp.concatenate([x1 * c - x2 * s, x1 * s + x2 * c], axis=0)


def _inproj_kernel(x_ref, mod_ref, ng_ref, wr_ref, wkt_ref, qg_ref, kvgr_ref, kvgc_ref,
                   wuq_ref, wukt_ref, wuv_ref, tc_ref, ts_ref, m_ref, ct_ref, st_ref,
                   qa_ref, qb_ref, qc_ref, u_ref, va_ref, vb_ref, vc_ref, kb_ref, kta_ref, ktc_ref):
    x = x_ref[...]
    mod = mod_ref[...]
    hx = _rms(x, ng_ref[0:1, :]) * (1.0 + mod[1:2, :]) + mod[0:1, :]
    hb = hx.astype(BF)
    zr = _dot(hb, wr_ref[...])
    zk = _dot_nt(wkt_ref[...], hb)

    tc = tc_ref[...]
    ts = ts_ref[...]
    m1 = m_ref[...]

    cqn = _rms(zr[:, R_CQ:R_CQ + MLA_Q_RANK], qg_ref[...]).astype(BF)
    qa = _dot(cqn, wuq_ref[...])
    qa_ref[...] = _rope_rows(qa, tc[:, P_QA:P_QB], ts[:, P_QA:P_QB], m1[:, P_QA:P_QB],
                             MLA_ROPE // 2).astype(BF)
    qb_ref[...] = _rope_rows(zr[:, R_BQ:R_DQ], tc[:, P_QB:P_KB], ts[:, P_QB:P_KB], m1[:, P_QB:P_KB],
                             SWA_HEAD_DIM // 2).astype(BF)
    qc_ref[...] = _rope_rows(zr[:, R_DQ:R_DV], tc[:, P_QC:P_END], ts[:, P_QC:P_END], m1[:, P_QC:P_END],
                             DIFF_QK // 2).astype(BF)
    kb_ref[...] = _rope_rows(zr[:, R_BK:R_BV], tc[:, P_KB:P_QC], ts[:, P_KB:P_QC], m1[:, P_KB:P_QC],
                             SWA_HEAD_DIM // 2).astype(BF)
    u_ref[...] = zr[:, R_U:R_CKV]
    vb_ref[...] = zr[:, R_BV:R_END].astype(BF)
    vc_ref[...] = zr[:, R_DV:R_U].astype(BF)

    ckvn = _rms(zr[:, R_CKV:R_BK], kvgr_ref[...]).astype(BF)
    va_ref[...] = _dot(ckvn, wuv_ref[...]).astype(BF)

    ckvt = zk[T_CKV:T_KR]
    ckvtn = ckvt * lax.rsqrt(jnp.mean(ckvt * ckvt, axis=0, keepdims=True) + EPS) * kvgc_ref[...]
    knt = _dot(wukt_ref[...], ckvtn.astype(BF))
    ct = ct_ref[...]
    st = st_ref[...]
    krt = _rope_cols(zk[T_KR:T_DK], ct, st)
    pieces = []
    for h in range(MLA_HEADS):
        pieces.append(knt[h * MLA_NOPE:(h + 1) * MLA_NOPE])
        pieces.append(krt)
    kta_ref[...] = jnp.concatenate(pieces, axis=0).astype(BF)

    dk = zk[T_DK:T_END]
    ktc_ref[...] = jnp.concatenate(
        [_rope_cols(dk[j * DIFF_QK:(j + 1) * DIFF_QK], ct, st) for j in range(2 * DIFF_HEADS)],
        axis=0).astype(BF)


def _inproj(layer, x, mod_l, mod_row, tm, pw, tabs):
    B, L, _ = x.shape
    nt = L // tm
    tab_c, tab_s, msk, cos_t, sin_t = tabs
    row = (lambda b: b) if mod_row is None else (lambda b: mod_row)

    def wspec(shape):
        nd = len(shape)
        return pl.BlockSpec((None,) + shape, lambda i, b: (layer,) + (0,) * nd)

    tok = lambda w: pl.BlockSpec((None, tm, w), lambda i, b: (b, i, 0))
    tokt = lambda w: pl.BlockSpec((None, w, tm), lambda i, b: (b, 0, i))
    in_specs = [
        tok(D_MODEL),
        pl.BlockSpec((None, N_MOD, D_MODEL), lambda i, b: (row(b), 0, 0)),
        wspec((4, D_MODEL)),
        wspec((D_MODEL, R_END)),
        wspec((T_END, D_MODEL)),
        wspec((1, MLA_Q_RANK)),
        wspec((1, MLA_KV_RANK)),
        wspec((MLA_KV_RANK, 1)),
        wspec((MLA_Q_RANK, MLA_HEADS * MLA_QK)),
        wspec((MLA_HEADS * MLA_NOPE, MLA_KV_RANK)),
        wspec((MLA_KV_RANK, MLA_HEADS * MLA_V)),
        pl.BlockSpec((tm, P_END), lambda i, b: (i, 0)),
        pl.BlockSpec((tm, P_END), lambda i, b: (i, 0)),
        pl.BlockSpec((1, P_END), lambda i, b: (0, 0)),
        pl.BlockSpec((MLA_ROPE // 2, tm), lambda i, b: (0, i)),
        pl.BlockSpec((MLA_ROPE // 2, tm), lambda i, b: (0, i)),
    ]
    widths = [(MLA_HEADS * MLA_QK, BF), (256, BF), (256, BF), (MIX_D, F32),
              (256, BF), (128, BF), (256, BF), (128, BF)]
    out_shape = [jax.ShapeDtypeStruct((B, L, w), dt) for w, dt in widths]
    out_specs = [tok(w) for w, _ in widths]
    out_shape += [jax.ShapeDtypeStruct((B, MLA_HEADS * MLA_QK, L), BF),
                  jax.ShapeDtypeStruct((B, 2 * DIFF_HEADS * DIFF_QK, L), BF)]
    out_specs += [tokt(MLA_HEADS * MLA_QK), tokt(2 * DIFF_HEADS * DIFF_QK)]
    return pl.pallas_call(
        _inproj_kernel,
        out_shape=out_shape,
        grid=(nt, B),
        in_specs=in_specs,
        out_specs=out_specs,
        compiler_params=_params(("arbitrary", "arbitrary")),
        name="inproj",
    )(x, mod_l, pw["norm_g"], pw["w_r"], pw["w_kt"], pw["q_gain"], pw["kv_gain_r"], pw["kv_gain_c"],
      pw["w_uq"], pw["w_ukt"], pw["w_uv"], tab_c, tab_s, msk, cos_t, sin_t)


def _softmax_pv(ss, vs, extra=None):
    mc = None
    for s in ss:
        for c in range(0, s.shape[1], KEY_CHUNK):
            blk = s[:, c:c + KEY_CHUNK]
            mc = blk if mc is None else jnp.maximum(mc, blk)
    m = jnp.max(mc, axis=-1, keepdims=True)
    if extra is not None:
        m = jnp.maximum(m, extra)
    lacc = None
    r = None
    for s, v in zip(ss, vs):
        for c in range(0, s.shape[1], KEY_CHUNK):
            e = jnp.exp2(s[:, c:c + KEY_CHUNK] - m)
            rs = _dot(e.astype(BF), v[c:c + KEY_CHUNK, :])
            lacc = e if lacc is None else lacc + e
            r = rs if r is None else r + rs
    l = jnp.sum(lacc, axis=-1, keepdims=True)
    if extra is not None:
        l = l + jnp.exp2(extra - m)
    return r / l


def _attn_heads(heads):
    outs = []
    prev = None
    for step in range(len(heads) + 1):
        cur = heads[step] if step < len(heads) else None
        kts = (cur if cur is not None else heads[step - 1])[1]
        cur_s, mc = [], None
        if prev is not None:
            prev_s, prev_m, prev_vs = prev
            lacc, r = None, None
        idx = 0
        for seg, kt in enumerate(kts):
            for c in range(0, kt.shape[1], KEY_CHUNK):
                if cur is not None:
                    s = _dot(cur[0], cur[1][seg][:, c:c + KEY_CHUNK])
                    cur_s.append(s)
                    mc = s if mc is None else jnp.maximum(mc, s)
                if prev is not None:
                    e = jnp.exp2(prev_s[idx] - prev_m)
                    rs = _dot(e.astype(BF), prev_vs[seg][c:c + KEY_CHUNK, :])
                    lacc = e if lacc is None else lacc + e
                    r = rs if r is None else r + rs
                idx += 1
        if prev is not None:
            outs.append(r / jnp.sum(lacc, axis=-1, keepdims=True))
        prev = (cur_s, jnp.max(mc, axis=-1, keepdims=True), cur[2]) if cur is not None else None
    return outs


def _mla_kernel(*refs, nseg):
    q_ref = refs[0]
    kt_refs = refs[1:1 + nseg]
    v_refs = refs[1 + nseg:1 + 2 * nseg]
    o_ref = refs[1 + 2 * nseg]
    q = q_ref[...]
    lane = lax.broadcasted_iota(jnp.int32, (1, LANES), 1)
    heads = []
    for h in range(MLA_HEADS):
        sl = slice(h * MLA_QK, (h + 1) * MLA_QK)
        vs = [v[:, (h // 2) * LANES:(h // 2 + 1) * LANES] for v in v_refs]
        heads.append((q[:, sl], [kt.at[sl, :] for kt in kt_refs], vs))
    outs = _attn_heads(heads)
    blocks = [jnp.where(lane < MLA_V, outs[2 * p], outs[2 * p + 1]) for p in range(MLA_HEADS // 2)]
    o_ref[...] = jnp.concatenate(blocks, axis=-1).astype(BF)


def _diff_kernel(*refs, nseg, lam_init):
    q_ref = refs[0]
    kt_refs = refs[1:1 + nseg]
    v_refs = refs[1 + nseg:1 + 2 * nseg]
    dl_ref, g_ref, o_ref = refs[1 + 2 * nseg:]
    q = q_ref[...]
    dl = dl_ref[...]
    lam = (jnp.exp(jnp.sum(dl[0:1] * dl[1:2], axis=-1, keepdims=True))
           - jnp.exp(jnp.sum(dl[2:3] * dl[3:4], axis=-1, keepdims=True)) + lam_init)
    lane = lax.broadcasted_iota(jnp.int32, (1, LANES), 1)
    lo = lane < DIFF_V
    heads = []
    for j in range(2 * DIFF_HEADS):
        sl = slice(j * DIFF_QK, (j + 1) * DIFF_QK)
        vs = [v[:, (j // 4) * LANES:(j // 4 + 1) * LANES] for v in v_refs]
        heads.append((q[:, sl], [kt.at[sl, :] for kt in kt_refs], vs))
    comps = _attn_heads(heads)
    blocks = []
    for p in range(DIFF_HEADS // 2):
        pair = [comps[2 * h] - lam * comps[2 * h + 1] for h in (2 * p, 2 * p + 1)]
        o = jnp.where(lo, pair[0], pair[1])
        o2 = o * o
        ms_lo = jnp.sum(jnp.where(lo, o2, 0.0), axis=-1, keepdims=True) * (1.0 / DIFF_V)
        ms_hi = jnp.sum(jnp.where(lo, 0.0, o2), axis=-1, keepdims=True) * (1.0 / DIFF_V)
        ms = jnp.where(lo, ms_lo, ms_hi)
        blocks.append(o * lax.rsqrt(ms + EPS) * g_ref[...] * (1.0 - lam_init))
    o_ref[...] = jnp.concatenate(blocks, axis=-1).astype(BF)


def _dense_attn(kind, layer, q, kts, vs, tq, extra_in=(), lam_init=0.0):
    B, L, wq = q.shape
    nseg = len(kts)
    in_specs = [pl.BlockSpec((None, tq, wq), lambda b, i: (b, i, 0))]
    for kt in kts:
        in_specs.append(pl.BlockSpec((None,) + kt.shape[1:], lambda b, i: (b, 0, 0)))
    for v in vs:
        in_specs.append(pl.BlockSpec((None,) + v.shape[1:], lambda b, i: (b, 0, 0)))
    if kind == "mla":
        body = functools.partial(_mla_kernel, nseg=nseg)
    else:
        body = functools.partial(_diff_kernel, nseg=nseg, lam_init=lam_init)
        dl, g = extra_in
        in_specs.append(pl.BlockSpec((None, 4, DIFF_QK), lambda b, i: (layer, 0, 0)))
        in_specs.append(pl.BlockSpec((None, 1, LANES), lambda b, i: (layer, 0, 0)))
    return pl.pallas_call(
        body,
        out_shape=jax.ShapeDtypeStruct((B, L, 256), BF),
        grid=(B, L // tq),
        in_specs=in_specs,
        out_specs=pl.BlockSpec((None, tq, 256), lambda b, i: (b, i, 0)),
        compiler_params=_params(("arbitrary", "arbitrary")),
        name=kind,
    )(q, *kts, *vs, *extra_in)


def _swa_kernel(*refs, layer, tq, seq_len, local):
    if local:
        sink_ref, q_ref, kl_ref, vl_ref, kc_ref, vc_ref, o_ref = refs
    else:
        sink_ref, q_ref, kc_ref, vc_ref, o_ref = refs
    i = pl.program_id(1)
    q = q_ref[...]
    lane = lax.broadcasted_iota(jnp.int32, (1, LANES), 1)
    rows = lax.broadcasted_iota(jnp.int32, (2 * tq, 1), 0)
    top = rows < tq
    if local:
        win = tq + 2 * SWA_WINDOW
        q0 = i * tq
        ws = pl.multiple_of(jnp.clip(q0 - SWA_WINDOW, 0, seq_len - win), LANES)
        kwin = kl_ref[pl.ds(ws, win), :]
        vwin = vl_ref[pl.ds(ws, win), :]
        qpos = q0 + jnp.where(top, rows, rows - tq)
        kpos = ws + lax.broadcasted_iota(jnp.int32, (1, win), 1)
        band = jnp.abs(kpos - qpos) <= SWA_WINDOW
    blocks = []
    for k in range(SWA_KV_HEADS):
        q2 = jnp.concatenate([q[:, (2 * k) * SWA_HEAD_DIM:(2 * k + 1) * SWA_HEAD_DIM],
                              q[:, (2 * k + 1) * SWA_HEAD_DIM:(2 * k + 2) * SWA_HEAD_DIM]], axis=0)
        sl = slice(k * SWA_HEAD_DIM, (k + 1) * SWA_HEAD_DIM)
        sk = jnp.where(top, sink_ref[layer, 2 * k], sink_ref[layer, 2 * k + 1]) * LOG2E
        ss, vs = [_dot_nt(q2, kc_ref[:, sl])], [vc_ref[...]]
        if local:
            ss.append(jnp.where(band, _dot_nt(q2, kwin[:, sl]), NEG))
            vs.append(vwin)
        r = _softmax_pv(ss, vs, extra=sk)
        r_top, r_bot = r[:tq], r[tq:]
        if k == 0:
            r_bot = pltpu.roll(r_bot, SWA_HEAD_DIM, axis=1)
        else:
            r_top = pltpu.roll(r_top, SWA_HEAD_DIM, axis=1)
        blocks.append(jnp.where(lane < SWA_HEAD_DIM, r_top, r_bot))
    o_ref[...] = jnp.concatenate(blocks, axis=-1).astype(BF)


def _swa(layer, sink, q, k_loc, v_loc, k_ctx, v_ctx, tq):
    B, L, _ = q.shape
    C = k_ctx.shape[1]
    local = k_loc is not None
    full = lambda n: pl.BlockSpec((None, n, 128), lambda b, i: (b, 0, 0))
    in_specs = [pl.BlockSpec(memory_space=pltpu.SMEM),
                pl.BlockSpec((None, tq, 256), lambda b, i: (b, i, 0))]
    args = [sink, q]
    if local:
        in_specs += [full(L), full(L)]
        args += [k_loc, v_loc]
    in_specs += [full(C), full(C)]
    args += [k_ctx, v_ctx]
    return pl.pallas_call(
        functools.partial(_swa_kernel, layer=layer, tq=tq, seq_len=L, local=local),
        out_shape=jax.ShapeDtypeStruct((B, L, 256), BF),
        grid=(B, L // tq),
        in_specs=in_specs,
        out_specs=pl.BlockSpec((None, tq, 256), lambda b, i: (b, i, 0)),
        compiler_params=_params(("arbitrary", "arbitrary")),
        name="swa",
    )(*args)


def _pool_diffs(u, prev, nxt, i, nt, tm, seq_len):
    n = tm + 2 * POOL_HALO
    ue = jnp.concatenate([jnp.where(i > 0, prev, 0.0), u, jnp.where(i < nt - 1, nxt, 0.0)], axis=0)

    def sh(a, k):
        return pltpu.roll(a, k % n, axis=0)

    s2 = ue + sh(ue, 1)
    s4 = sh(s2, 1) + sh(s2, -1)
    s8 = sh(s4, 2) + sh(s4, -2)
    s16 = sh(s8, 4) + sh(s8, -4)
    lane = lax.broadcasted_iota(jnp.int32, (1, MIX_D), 1)
    t = i * tm + lax.broadcasted_iota(jnp.int32, (tm, 1), 0)
    win = None
    cnt = None
    for gi, (w, s) in enumerate(zip(POOL_WINDOWS, (s2, s4, s8, s16))):
        half = w // 2
        c = (jnp.minimum(t - half + w, seq_len) - jnp.maximum(t - half, 0)).astype(F32)
        sv = s[POOL_HALO:POOL_HALO + tm]
        if win is None:
            win, cnt = sv, c
        else:
            sel = lane >= gi * POOL_GROUP
            win = jnp.where(sel, sv, win)
            cnt = jnp.where(sel, c, cnt)
    return win / cnt - u


def _outproj_kernel(oa_ref, ob_ref, oc_ref, u_ref, up_ref, un_ref, x_ref, mod_ref, ng_ref,
                    wo_ref, pw_ref, ps_ref, rwh_ref, rwl_ref, rb_ref,
                    xo_ref, hr_ref, rt_ref, cnt_ref, carry_ref, *, tm, seq_len, nt):
    i = pl.program_id(1)

    @pl.when((pl.program_id(0) == 0) & (i == 0))
    def _():
        carry_ref[...] = jnp.zeros_like(carry_ref)

    u = u_ref[...]
    dg = _pool_diffs(u, up_ref[...], un_ref[...], i, nt, tm, seq_len)
    od = _dot(dg.astype(BF), pw_ref[...]) * ps_ref[...]
    mix = jnp.concatenate([oa_ref[...], ob_ref[...], oc_ref[...], od.astype(BF)], axis=-1)
    ox = _dot(mix, wo_ref[...])
    mod = mod_ref[...]
    x = x_ref[...] + mod[2:3, :] * _rms(ox, ng_ref[1:2, :])
    xo_ref[...] = x
    h = _rms(x, ng_ref[2:3, :]) * (1.0 + mod[4:5, :]) + mod[3:4, :]
    hr_ref[...] = h.reshape((tm,) + TOK_TILE)
    hh = h.astype(BF)
    hl = (h - hh.astype(F32)).astype(BF)
    wh = rwh_ref[...]
    logits = _dot(hh, wh) + _dot(hl, wh) + _dot(hh, rwl_ref[...]) + rb_ref[...]

    lane = lax.broadcasted_iota(jnp.int32, (1, LANES), 1)
    lanef = lane.astype(F32)
    is_g = (lane >= N_EXPERTS) & (lane < N_EXPERTS + N_EXPERT_GROUPS)
    gl = jnp.where(is_g, logits, NEG)
    gmax = jnp.max(gl, axis=-1, keepdims=True)
    gsel = jnp.min(jnp.where(gl == gmax, lanef, 1e9), axis=-1, keepdims=True) - N_EXPERTS
    p_g = 1.0 / jnp.sum(jnp.where(is_g, jnp.exp(gl - gmax), 0.0), axis=-1, keepdims=True)
    grp = jnp.right_shift(lane, 2).astype(F32)
    el = jnp.where((lane < N_EXPERTS) & (grp == gsel), logits, NEG)
    v1 = jnp.max(el, axis=-1, keepdims=True)
    i1 = jnp.min(jnp.where(el == v1, lanef, 1e9), axis=-1, keepdims=True)
    el2 = jnp.where(lanef == i1, NEG, el)
    v2 = jnp.max(el2, axis=-1, keepdims=True)
    i2 = jnp.min(jnp.where(el2 == v2, lanef, 1e9), axis=-1, keepdims=True)
    e21 = jnp.exp(v2 - v1)
    w1 = p_g / (1.0 + e21)
    w2 = w1 * e21

    l1 = i1 - EXPERTS_PER_GROUP * gsel
    l2 = i2 - EXPERTS_PER_GROUP * gsel
    code = jnp.minimum(l1, l2) * EXPERTS_PER_GROUP + jnp.maximum(l1, l2)
    pidx = jnp.full_like(code, float(PAIRS_PER_GROUP - 1))
    slot_a = jnp.full_like(code, float(PAIR_A[-1]))
    for p in range(PAIRS_PER_GROUP - 1):
        lo_e, hi_e = min(PAIR_A[p], PAIR_B[p]), max(PAIR_A[p], PAIR_B[p])
        hit = code == float(lo_e * EXPERTS_PER_GROUP + hi_e)
        pidx = jnp.where(hit, float(p), pidx)
        slot_a = jnp.where(hit, float(PAIR_A[p]), slot_a)
    first_is_a = l1 == slot_a
    w_a = jnp.where(first_is_a, w1, w2)
    w_b = jnp.where(first_is_a, w2, w1)
    bucket = gsel * PAIRS_PER_GROUP + pidx

    onehot = lanef == bucket
    rr = lax.broadcasted_iota(jnp.int32, (tm, tm), 0)
    cc = lax.broadcasted_iota(jnp.int32, (tm, tm), 1)
    tri = jnp.where(rr >= cc, 1.0, 0.0).astype(BF)
    prefix = _dot(tri, jnp.where(onehot, 1.0, 0.0).astype(BF))
    carry = carry_ref[...]
    rank = jnp.sum(jnp.where(onehot, prefix - 1.0 + carry, 0.0), axis=-1, keepdims=True)
    carry = carry + prefix[tm - 1:tm, :]
    carry_ref[...] = carry
    cnt_ref[...] = carry
    rt_ref[...] = jnp.where(lane == 0, w_a, jnp.where(lane == 1, w_b, jnp.where(
        lane == 2, bucket, jnp.where(lane == 3, rank, 0.0))))


def _outproj(layer, oa, ob, oc, u, x, mod_l, mod_row, tm, pw):
    B, L, _ = x.shape
    nt = L // tm
    hb = tm // POOL_HALO
    nhb = L // POOL_HALO
    row = (lambda b: b) if mod_row is None else (lambda b: mod_row)

    def wspec(shape):
        nd = len(shape)
        return pl.BlockSpec((None,) + shape, lambda b, i: (layer,) + (0,) * nd)

    tok = lambda w: pl.BlockSpec((None, tm, w), lambda b, i: (b, i, 0))
    in_specs = [
        tok(256), tok(256), tok(256), tok(MIX_D),
        pl.BlockSpec((None, POOL_HALO, MIX_D), lambda b, i: (b, jnp.maximum(i * hb - 1, 0), 0)),
        pl.BlockSpec((None, POOL_HALO, MIX_D), lambda b, i: (b, jnp.minimum((i + 1) * hb, nhb - 1), 0)),
        tok(D_MODEL),
        pl.BlockSpec((None, N_MOD, D_MODEL), lambda b, i: (row(b), 0, 0)),
        wspec((4, D_MODEL)),
        wspec((D_MODEL, D_MODEL)),
        wspec((MIX_D, MIX_D)),
        wspec((1, MIX_D)),
        wspec((D_MODEL, LANES)),
        wspec((D_MODEL, LANES)),
        wspec((1, LANES)),
    ]
    return pl.pallas_call(
        functools.partial(_outproj_kernel, tm=tm, seq_len=L, nt=nt),
        out_shape=[jax.ShapeDtypeStruct((B, L, D_MODEL), F32),
                   jax.ShapeDtypeStruct((B * L,) + TOK_TILE, F32),
                   jax.ShapeDtypeStruct((B, L, LANES), F32),
                   jax.ShapeDtypeStruct((1, LANES), F32)],
        grid=(B, nt),
        in_specs=in_specs,
        out_specs=[tok(D_MODEL),
                   pl.BlockSpec((tm,) + TOK_TILE, lambda b, i: (b * nt + i, 0, 0)),
                   tok(LANES), pl.BlockSpec((1, LANES), lambda b, i: (0, 0))],
        scratch_shapes=[pltpu.VMEM((1, LANES), F32)],
        compiler_params=_params(("arbitrary", "arbitrary")),
        name="outproj",
    )(oa, ob, oc, u, u, u, x, mod_l, pw["norm_g"], pw["w_out"], pw["pool_bd"], pw["pool_scale"],
      pw["r_hi"], pw["r_lo"], pw["r_b"])


def _route_tables(route, counts, n_tiles):
    T = route.shape[0] * route.shape[1]
    bucket = route[:, :, 2].reshape(T).astype(jnp.int32)
    rank = route[:, :, 3].reshape(T).astype(jnp.int32)
    cnt = counts[0, :N_BUCKETS].astype(jnp.int32)
    tiles_b = (cnt + MOE_TM - 1) // MOE_TM
    ends = jnp.cumsum(tiles_b)
    starts = ends - tiles_b
    pos = jnp.take(starts * MOE_TM, bucket) + rank
    n_used = ends[-1]
    tile = jnp.arange(n_tiles, dtype=jnp.int32)
    src = jnp.minimum(tile, n_used - 1)
    tb = jnp.sum((src[:, None] >= ends[None, :]).astype(jnp.int32), axis=1)
    tb = jnp.minimum(tb, N_BUCKETS - 1)
    grp, pidx = tb // PAIRS_PER_GROUP, tb % PAIRS_PER_GROUP
    ea = grp * EXPERTS_PER_GROUP + jnp.take(jnp.array(PAIR_A, jnp.int32), pidx)
    eb = grp * EXPERTS_PER_GROUP + jnp.take(jnp.array(PAIR_B, jnp.int32), pidx)
    zlast = jnp.where(tiles_b > 0, (ends - 1) * MOE_TM, -1)
    idle = n_used + jnp.arange(N_BUCKETS, dtype=jnp.int32)
    zidle = jnp.where(idle < n_tiles, idle * MOE_TM, -1)
    zrow = jnp.concatenate([zlast, zidle]).astype(jnp.int32)
    return pos, zrow, src, ea, eb, n_used.reshape(1).astype(jnp.int32)


def _row_copy(src_ref, src_row, dst_ref, dst_row, sem):
    return pltpu.make_async_copy(src_ref.at[src_row], dst_ref.at[dst_row], sem)


def _scatter_kernel(pos_ref, zrow_ref, hr_ref, rt_ref, hs_ref, ws_ref, zbuf_ref, zw_ref, sem, zsem, *, tm):
    i = pl.program_id(0)

    @pl.when(i == 0)
    def _():
        zbuf_ref[...] = jnp.zeros_like(zbuf_ref)
        zw_ref[...] = jnp.zeros_like(zw_ref)
        for start in (True, False):
            for b in range(2 * N_BUCKETS):
                @pl.when(zrow_ref[b] >= 0)
                def _():
                    z0 = pl.multiple_of(zrow_ref[b], MOE_TM)
                    for cp in (pltpu.make_async_copy(zbuf_ref, hs_ref.at[pl.ds(z0, MOE_TM)], zsem),
                               pltpu.make_async_copy(zw_ref, ws_ref.at[pl.ds(z0, MOE_TM)], zsem)):
                        cp.start() if start else cp.wait()

    def copies(r):
        p = pos_ref[i * tm + r]
        return (_row_copy(hr_ref, r, hs_ref, p, sem),
                pltpu.make_async_copy(rt_ref.at[pl.ds(r, 1)], ws_ref.at[pl.ds(p, 1)], sem))

    _row_dma_loop(tm, copies)


def _row_dma_loop(n, copies):
    def issue(g, carry):
        for k in range(DMA_UNROLL):
            for cp in copies(g * DMA_UNROLL + k):
                cp.start(priority=k % 2)
        return carry

    lax.fori_loop(0, n // DMA_UNROLL, issue, 0)

    def drain(g, carry):
        for k in range(DMA_UNROLL):
            for cp in copies(g * DMA_UNROLL + k):
                cp.wait()
        return carry

    lax.fori_loop(0, n // DMA_UNROLL, drain, 0)


def _scatter_rows(hr, route, pos, zrow, n_tiles, tm):
    T = hr.shape[0]
    return pl.pallas_call(
        functools.partial(_scatter_kernel, tm=tm),
        out_shape=[jax.ShapeDtypeStruct((n_tiles * MOE_TM,) + TOK_TILE, F32),
                   jax.ShapeDtypeStruct((n_tiles * MOE_TM, LANES), F32)],
        grid_spec=pltpu.PrefetchScalarGridSpec(
            num_scalar_prefetch=2,
            grid=(T // tm,),
            in_specs=[pl.BlockSpec((tm,) + TOK_TILE, lambda i, pos, zrow: (i, 0, 0)),
                      pl.BlockSpec((tm, LANES), lambda i, pos, zrow: (i, 0))],
            out_specs=[pl.BlockSpec(memory_space=pl.ANY), pl.BlockSpec(memory_space=pl.ANY)],
            scratch_shapes=[pltpu.VMEM((MOE_TM,) + TOK_TILE, F32), pltpu.VMEM((MOE_TM, LANES), F32),
                            pltpu.SemaphoreType.DMA(()), pltpu.SemaphoreType.DMA(())]),
        compiler_params=_params(("arbitrary",)),
        name="moe_scatter",
    )(pos, zrow, hr, route)


def _expert(h, wgu, wd):
    gu = _dot(h, wgu)
    g = gu[:, :D_EXPERT]
    act = (g * jax.nn.sigmoid(g)) * gu[:, D_EXPERT:]
    return _dot(act.astype(BF), wd)


def _moe_kernel(src_ref, ea_ref, eb_ref, nu_ref, hs_ref, ws_ref, wgua_ref, wgub_ref, wda_ref, wdb_ref, o_ref):
    j = pl.program_id(0)

    @pl.when(j < nu_ref[0])
    def _():
        h = hs_ref[...].reshape(MOE_TM, D_MODEL).astype(BF)
        w = ws_ref[...]
        y = (w[:, 0:1] * _expert(h, wgua_ref[...], wda_ref[...])
             + w[:, 1:2] * _expert(h, wgub_ref[...], wdb_ref[...]))
        o_ref[...] = y.reshape((MOE_TM,) + TOK_TILE)

    @pl.when(j >= nu_ref[0])
    def _():
        o_ref[...] = jnp.zeros_like(o_ref)


def _moe_sorted(layer, hs, ws, src, ea, eb, n_used, pw):
    n_tiles = hs.shape[0] // MOE_TM
    wgu = lambda sel: pl.BlockSpec((None, None, D_MODEL, 2 * D_EXPERT),
                                   lambda j, src, ea, eb, nu: (layer, (ea, eb)[sel][j], 0, 0))
    wd = lambda sel: pl.BlockSpec((None, None, D_EXPERT, D_MODEL),
                                  lambda j, src, ea, eb, nu: (layer, (ea, eb)[sel][j], 0, 0))
    return pl.pallas_call(
        _moe_kernel,
        out_shape=jax.ShapeDtypeStruct((n_tiles * MOE_TM,) + TOK_TILE, F32),
        grid_spec=pltpu.PrefetchScalarGridSpec(
            num_scalar_prefetch=4,
            grid=(n_tiles,),
            in_specs=[pl.BlockSpec((MOE_TM,) + TOK_TILE, lambda j, src, ea, eb, nu: (src[j], 0, 0)),
                      pl.BlockSpec((MOE_TM, LANES), lambda j, src, ea, eb, nu: (src[j], 0)),
                      wgu(0), wgu(1), wd(0), wd(1)],
            out_specs=pl.BlockSpec((MOE_TM,) + TOK_TILE, lambda j, src, ea, eb, nu: (j, 0, 0))),
        compiler_params=_params(("arbitrary",)),
        name="moe",
    )(src, ea, eb, n_used, hs, ws, pw["w_gu"], pw["w_gu"], pw["w_down"], pw["w_down"])


def _unsort_kernel(pos_ref, ys_ref, x_ref, mod_ref, ng_ref, o_ref, buf_ref, sem, *, tm):
    i = pl.program_id(0)
    _row_dma_loop(tm, lambda r: (_row_copy(ys_ref, pos_ref[i * tm + r], buf_ref, r, sem),))
    fx = buf_ref[...].reshape(tm, D_MODEL)
    o_ref[...] = x_ref[...] + mod_ref[5:6, :] * _rms(fx, ng_ref[3:4, :])


def _unsort_residual(layer, ys, pos, x, mod_l, mod_row, tm, pw):
    B, L, _ = x.shape
    T = B * L
    tiles_per_seq = L // tm
    row = (lambda i: i // tiles_per_seq) if mod_row is None else (lambda i: mod_row)
    out = pl.pallas_call(
        functools.partial(_unsort_kernel, tm=tm),
        out_shape=jax.ShapeDtypeStruct((T, D_MODEL), F32),
        grid_spec=pltpu.PrefetchScalarGridSpec(
            num_scalar_prefetch=1,
            grid=(T // tm,),
            in_specs=[pl.BlockSpec(memory_space=pl.ANY),
                      pl.BlockSpec((tm, D_MODEL), lambda i, pos: (i, 0)),
                      pl.BlockSpec((None, N_MOD, D_MODEL), lambda i, pos: (row(i), 0, 0)),
                      pl.BlockSpec((None, 4, D_MODEL), lambda i, pos: (layer, 0, 0))],
            out_specs=pl.BlockSpec((tm, D_MODEL), lambda i, pos: (i, 0)),
            scratch_shapes=[pltpu.VMEM((tm,) + TOK_TILE, F32), pltpu.SemaphoreType.DMA(())]),
        compiler_params=_params(("arbitrary",)),
        name="moe_unsort",
    )(pos, ys, x.reshape(T, D_MODEL), mod_l, pw["norm_g"])
    return out.reshape(B, L, D_MODEL)


def _moe(layer, hr, route, counts, x, mod_l, mod_row, tm, pw):
    B, L, _ = x.shape
    T = B * L
    n_tiles = T // MOE_TM + N_BUCKETS
    pos, zrow, src, ea, eb, n_used = _route_tables(route, counts, n_tiles)
    hs, ws = _scatter_rows(hr, route.reshape(T, LANES), pos, zrow, n_tiles, tm)
    ys = _moe_sorted(layer, hs, ws, src, ea, eb, n_used, pw)
    return _unsort_residual(layer, ys, pos, x, mod_l, mod_row, tm, pw)


def _axial(length, dim):
    rows = length // GRID_W
    row = jnp.repeat(jnp.arange(rows), GRID_W).astype(F32)
    col = jnp.tile(jnp.arange(GRID_W), rows).astype(F32)
    n = dim // 4
    inv = ROPE_BASE ** (-jnp.arange(n, dtype=F32) / n)
    ang = jnp.concatenate([row[:, None] * inv, col[:, None] * inv], axis=-1)
    return jnp.cos(ang), jnp.sin(ang)


def _rope_tables(length, rotate):
    if rotate:
        cos_a, sin_a = _axial(length, MLA_ROPE)
        cos_b, sin_b = _axial(length, SWA_HEAD_DIM)
    else:
        cos_a, sin_a = jnp.ones((length, MLA_ROPE // 2), F32), jnp.zeros((length, MLA_ROPE // 2), F32)
        cos_b, sin_b = jnp.ones((length, SWA_HEAD_DIM // 2), F32), jnp.zeros((length, SWA_HEAD_DIM // 2), F32)
    one = jnp.ones((length, MLA_NOPE), F32)
    zero = jnp.zeros((length, MLA_NOPE), F32)
    sa, sb, sc = MLA_SCALE * LOG2E, SWA_SCALE * LOG2E, DIFF_SCALE * LOG2E
    c_qa = jnp.tile(jnp.concatenate([one, cos_a, cos_a], -1), (1, MLA_HEADS)) * sa
    s_qa = jnp.tile(jnp.concatenate([zero, -sin_a, sin_a], -1), (1, MLA_HEADS)) * sa
    c_b = jnp.concatenate([cos_b, cos_b], -1)
    s_b = jnp.concatenate([-sin_b, sin_b], -1)
    c_c = jnp.tile(jnp.concatenate([cos_a, cos_a], -1), (1, 2 * DIFF_HEADS)) * sc
    s_c = jnp.tile(jnp.concatenate([-sin_a, sin_a], -1), (1, 2 * DIFF_HEADS)) * sc
    tab_c = jnp.concatenate([c_qa, jnp.tile(c_b, (1, SWA_HEADS)) * sb, jnp.tile(c_b, (1, SWA_KV_HEADS)), c_c], -1)
    tab_s = jnp.concatenate([s_qa, jnp.tile(s_b, (1, SWA_HEADS)) * sb, jnp.tile(s_b, (1, SWA_KV_HEADS)), s_c], -1)
    m_qa = jnp.tile(jnp.concatenate([jnp.zeros((MLA_NOPE,)), jnp.ones((MLA_ROPE // 2,)),
                                     jnp.zeros((MLA_ROPE // 2,))]), MLA_HEADS)
    m_b = jnp.concatenate([jnp.ones((SWA_HEAD_DIM // 2,)), jnp.zeros((SWA_HEAD_DIM // 2,))])
    m_c = jnp.concatenate([jnp.ones((DIFF_QK // 2,)), jnp.zeros((DIFF_QK // 2,))])
    msk = jnp.concatenate([m_qa, jnp.tile(m_b, SWA_HEADS + SWA_KV_HEADS), jnp.tile(m_c, 2 * DIFF_HEADS)])
    return tab_c, tab_s, msk.astype(F32)[None, :], cos_a.T, sin_a.T


def _prep_weights(norm_g, w_in, mla_q_gain, mla_kv_gain, mla_w_uq, mla_w_ukv, diff_subln, pool_w,
                  pool_scale, w_out, router_gw, router_gb, router_ew, router_eb, exp_w_gu, exp_w_down):
    sizes = (MLA_Q_RANK, MLA_KV_RANK, MLA_ROPE, 256, 128, 128, 256, 256, 256, MIX_D)
    offs = [0]
    for s in sizes:
        offs.append(offs[-1] + s)
    cq, ckv, kr, bq, bk, bv, dq, dk, dv, u = [w_in[:, :, offs[j]:offs[j + 1]] for j in range(len(sizes))]
    w_r = jnp.concatenate([cq, bq, dq, dv, u, ckv, bk, bv], axis=-1).astype(BF)
    w_kt = jnp.swapaxes(jnp.concatenate([ckv, kr, dk], axis=-1), 1, 2).astype(BF)
    ukv = mla_w_ukv.reshape(DEPTH, MLA_KV_RANK, MLA_HEADS, MLA_NOPE + MLA_V)
    w_uk = ukv[..., :MLA_NOPE].reshape(DEPTH, MLA_KV_RANK, MLA_HEADS * MLA_NOPE)
    w_uv = ukv[..., MLA_NOPE:].reshape(DEPTH, MLA_KV_RANK, MLA_HEADS * MLA_V)
    eye = jnp.eye(len(POOL_WINDOWS), dtype=F32)
    pool_bd = jnp.einsum('lgcd,gh->lgchd', pool_w, eye).reshape(DEPTH, MIX_D, MIX_D)
    r_w = jnp.concatenate([router_ew, router_gw], axis=-1)
    r_w = jnp.pad(r_w, ((0, 0), (0, 0), (0, LANES - r_w.shape[-1])))
    r_hi = r_w.astype(BF)
    r_lo = (r_w - r_hi.astype(F32)).astype(BF)
    r_b = jnp.concatenate([router_eb, router_gb], axis=-1)
    r_b = jnp.pad(r_b, ((0, 0), (0, LANES - r_b.shape[-1])))[:, None, :]
    return dict(
        norm_g=norm_g, w_r=w_r, w_kt=w_kt,
        q_gain=mla_q_gain[:, None, :], kv_gain_r=mla_kv_gain[:, None, :], kv_gain_c=mla_kv_gain[:, :, None],
        w_uq=mla_w_uq.astype(BF), w_ukt=jnp.swapaxes(w_uk, 1, 2).astype(BF), w_uv=w_uv.astype(BF),
        subln=jnp.tile(diff_subln, (1, LANES // DIFF_V))[:, None, :],
        pool_bd=pool_bd.astype(BF), pool_scale=pool_scale[:, None, :], w_out=w_out.astype(BF),
        r_hi=r_hi, r_lo=r_lo, r_b=r_b, w_gu=exp_w_gu.astype(BF), w_down=exp_w_down.astype(BF))


def kernel(x, c, ctx, c_ctx, ada_w, ada_b, norm_g, w_in, mla_q_gain, mla_kv_gain, mla_w_uq, mla_w_ukv,
           swa_sink, diff_lambda, diff_subln, pool_w, pool_scale, w_out, router_gw, router_gb,
           router_ew, router_eb, exp_w_gu, exp_w_down):
    B, L, D = x.shape
    C = ctx.shape[1]
    assert D == D_MODEL and B + 1 <= MOD_ROWS and L % 512 == 0 and C % 256 == 0
    pw = _prep_weights(norm_g, w_in, mla_q_gain, mla_kv_gain, mla_w_uq, mla_w_ukv, diff_subln, pool_w,
                       pool_scale, w_out, router_gw, router_gb, router_ew, router_eb, exp_w_gu, exp_w_down)
    cc = jnp.concatenate([c, c_ctx[None, :], jnp.zeros((MOD_ROWS - B - 1, D), F32)], axis=0)
    mod = jnp.swapaxes(_adaln(cc, ada_w, ada_b), 1, 2)
    tabs_x = _rope_tables(L, True)
    tabs_c = _rope_tables(C, False)
    tm_x, tm_c, tq, tq_x = 512, 256, 256, 512

    h_ctx = ctx
    for layer in range(DEPTH):
        update_ctx = layer < DEPTH - 1
        lam_init = 0.8 - 0.6 * math.exp(-0.3 * layer)
        mod_l = mod[layer]
        qa, qb, qc, u, va, vb, vc, kb, kta, ktc = _inproj(layer, x, mod_l, None, tm_x, pw, tabs_x)
        cqa, cqb, cqc, cu, cva, cvb, cvc, ckb, ckta, cktc = _inproj(layer, h_ctx, mod_l, B, tm_c, pw, tabs_c)
        diff_extra = (diff_lambda, pw["subln"])

        oa = _dense_attn("mla", layer, qa, [kta, ckta], [va, cva], tq_x)
        ob = _swa(layer, swa_sink, qb, kb, vb, ckb, cvb, tq)
        oc = _dense_attn("diff", layer, qc, [ktc, cktc], [vc, cvc], tq_x, diff_extra, lam_init)
        x, hrx, rtx, cntx = _outproj(layer, oa, ob, oc, u, x, mod_l, None, tm_x, pw)
        if update_ctx:
            coa = _dense_attn("mla", layer, cqa, [ckta], [cva], tq)
            cob = _swa(layer, swa_sink, cqb, None, None, ckb, cvb, tq)
            coc = _dense_attn("diff", layer, cqc, [cktc], [cvc], tq, diff_extra, lam_init)
            h_ctx, hrc, rtc, cntc = _outproj(layer, coa, cob, coc, cu, h_ctx, mod_l, B, tm_c, pw)
        x = _moe(layer, hrx, rtx, cntx, x, mod_l, None, tm_x, pw)
        if update_ctx:
            h_ctx = _moe(layer, hrc, rtc, cntc, h_ctx, mod_l, B, tm_c, pw)
    return x
```

```python
import functools
import math

import jax
import jax.numpy as jnp
from jax import lax
from jax.experimental import pallas as pl
from jax.experimental.pallas import tpu as pltpu

F32 = jnp.float32
BF = jnp.bfloat16

D_MODEL = 1024
DEPTH = 2
GRID_W = 64
EPS = 1e-6
ROPE_BASE = 10000.0
N_MOD = 6

MLA_HEADS = 4
MLA_Q_RANK = 256
MLA_KV_RANK = 128
MLA_NOPE = 64
MLA_ROPE = 32
MLA_V = 64
MLA_QK = MLA_NOPE + MLA_ROPE
MLA_SCALE = MLA_QK ** -0.5

SWA_HEADS = 4
SWA_KV_HEADS = 2
SWA_HEAD_DIM = 64
SWA_WINDOW = 128
SWA_SCALE = SWA_HEAD_DIM ** -0.5

DIFF_HEADS = 4
DIFF_QK = 32
DIFF_V = 64
DIFF_SCALE = DIFF_QK ** -0.5

POOL_WINDOWS = (2, 4, 8, 16)
POOL_GROUP = 64
POOL_HALO = 8
MIX_D = len(POOL_WINDOWS) * POOL_GROUP

N_EXPERT_GROUPS = 4
EXPERTS_PER_GROUP = 4
N_EXPERTS = 16
D_EXPERT = 512
PAIR_A = (0, 2, 2, 3, 3, 3)
PAIR_B = (1, 1, 0, 0, 1, 2)
PAIRS_PER_GROUP = len(PAIR_A)
N_BUCKETS = N_EXPERT_GROUPS * PAIRS_PER_GROUP

LOG2E = math.log2(math.e)
NEG = -1e30
LANES = 128
MOD_ROWS = 24
SUBLANES = 8
TOK_TILE = (D_MODEL // LANES, LANES)
MOE_TM = 256
DMA_UNROLL = 8
KEY_CHUNK = 256
VMEM_LIMIT = 48 * 1024 * 1024

R_CQ, R_BQ, R_DQ, R_DV, R_U, R_CKV, R_BK, R_BV, R_END = 0, 256, 512, 768, 1024, 1280, 1408, 1536, 1664
T_CKV, T_KR, T_DK, T_END = 0, 128, 160, 416
P_QA, P_QB, P_KB, P_QC, P_END = 0, 384, 640, 768, 1024


def _dot(a, b):
    return jnp.dot(a, b, preferred_element_type=F32)


def _dot_nt(a, b):
    return lax.dot_general(a, b, (((1,), (1,)), ((), ())), preferred_element_type=F32)


def _rms(x, g):
    return x * lax.rsqrt(jnp.mean(x * x, axis=-1, keepdims=True) + EPS) * g


def _params(sem):
    return pltpu.CompilerParams(dimension_semantics=sem, vmem_limit_bytes=VMEM_LIMIT)


def _adaln_kernel(c_ref, w_ref, b_ref, o_ref):
    c = c_ref[...]
    s = (c * jax.nn.sigmoid(c)).astype(BF)
    o_ref[...] = _dot(s, w_ref[...].astype(BF)) + b_ref[...]


def _adaln(cc, ada_w, ada_b):
    return pl.pallas_call(
        _adaln_kernel,
        out_shape=jax.ShapeDtypeStruct((DEPTH, N_MOD, MOD_ROWS, D_MODEL), F32),
        grid=(DEPTH, N_MOD),
        in_specs=[
            pl.BlockSpec((MOD_ROWS, D_MODEL), lambda l, j: (0, 0)),
            pl.BlockSpec((None, D_MODEL, D_MODEL), lambda l, j: (l, 0, j)),
            pl.BlockSpec((None, None, 1, D_MODEL), lambda l, j: (l, j, 0, 0)),
        ],
        out_specs=pl.BlockSpec((None, None, MOD_ROWS, D_MODEL), lambda l, j: (l, j, 0, 0)),
        compiler_params=_params(("arbitrary", "arbitrary")),
        name="adaln",
    )(cc, ada_w, ada_b.reshape(DEPTH, N_MOD, 1, D_MODEL))


def _rope_rows(xf, c, s, m1, half):
    w = xf.shape[-1]
    up = pltpu.roll(xf, w - half, axis=1)
    dn = pltpu.roll(xf, half, axis=1)
    return xf * c + jnp.where(m1 > 0.5, up, dn) * s


def _rope_cols(xt, c, s):
    h = xt.shape[0] // 2
    x1, x2 = xt[:h], xt[h:]
    return jnp.concatenate([x1 * c - x2 * s, x1 * s + x2 * c], axis=0)


def _inproj_kernel(x_ref, mod_ref, ng_ref, wr_ref, wkt_ref, qg_ref, kvgr_ref, kvgc_ref,
                   wuq_ref, wukt_ref, wuv_ref, tc_ref, ts_ref, m_ref, ct_ref, st_ref,
                   qa_ref, qb_ref, qc_ref, u_ref, va_ref, vb_ref, vc_ref, kb_ref, kta_ref, ktc_ref):
    x = x_ref[...]
    mod = mod_ref[...]
    hx = _rms(x, ng_ref[0:1, :]) * (1.0 + mod[1:2, :]) + mod[0:1, :]
    hb = hx.astype(BF)
    zr = _dot(hb, wr_ref[...])
    zk = _dot_nt(wkt_ref[...], hb)

    tc = tc_ref[...]
    ts = ts_ref[...]
    m1 = m_ref[...]

    cqn = _rms(zr[:, R_CQ:R_CQ + MLA_Q_RANK], qg_ref[...]).astype(BF)
    qa = _dot(cqn, wuq_ref[...])
    qa_ref[...] = _rope_rows(qa, tc[:, P_QA:P_QB], ts[:, P_QA:P_QB], m1[:, P_QA:P_QB],
                             MLA_ROPE // 2).astype(BF)
    qb_ref[...] = _rope_rows(zr[:, R_BQ:R_DQ], tc[:, P_QB:P_KB], ts[:, P_QB:P_KB], m1[:, P_QB:P_KB],
                             SWA_HEAD_DIM // 2).astype(BF)
    qc_ref[...] = _rope_rows(zr[:, R_DQ:R_DV], tc[:, P_QC:P_END], ts[:, P_QC:P_END], m1[:, P_QC:P_END],
                             DIFF_QK // 2).astype(BF)
    kb_ref[...] = _rope_rows(zr[:, R_BK:R_BV], tc[:, P_KB:P_QC], ts[:, P_KB:P_QC], m1[:, P_KB:P_QC],
                             SWA_HEAD_DIM // 2).astype(BF)
    u_ref[...] = zr[:, R_U:R_CKV]
    vb_ref[...] = zr[:, R_BV:R_END].astype(BF)
    vc_ref[...] = zr[:, R_DV:R_U].astype(BF)

    ckvn = _rms(zr[:, R_CKV:R_BK], kvgr_ref[...]).astype(BF)
    va_ref[...] = _dot(ckvn, wuv_ref[...]).astype(BF)

    ckvt = zk[T_CKV:T_KR]
    ckvtn = ckvt * lax.rsqrt(jnp.mean(ckvt * ckvt, axis=0, keepdims=True) + EPS) * kvgc_ref[...]
    knt = _dot(wukt_ref[...], ckvtn.astype(BF))
    ct = ct_ref[...]
    st = st_ref[...]
    krt = _rope_cols(zk[T_KR:T_DK], ct, st)
    pieces = []
    for h in range(MLA_HEADS):
        pieces.append(knt[h * MLA_NOPE:(h + 1) * MLA_NOPE])
        pieces.append(krt)
    kta_ref[...] = jnp.concatenate(pieces, axis=0).astype(BF)

    dk = zk[T_DK:T_END]
    ktc_ref[...] = jnp.concatenate(
        [_rope_cols(dk[j * DIFF_QK:(j + 1) * DIFF_QK], ct, st) for j in range(2 * DIFF_HEADS)],
        axis=0).astype(BF)


def _inproj(layer, x, mod_l, mod_row, tm, pw, tabs):
    B, L, _ = x.shape
    nt = L // tm
    tab_c, tab_s, msk, cos_t, sin_t = tabs
    row = (lambda b: b) if mod_row is None else (lambda b: mod_row)

    def wspec(shape):
        nd = len(shape)
        return pl.BlockSpec((None,) + shape, lambda i, b: (layer,) + (0,) * nd)

    tok = lambda w: pl.BlockSpec((None, tm, w), lambda i, b: (b, i, 0))
    tokt = lambda w: pl.BlockSpec((None, w, tm), lambda i, b: (b, 0, i))
    in_specs = [
        tok(D_MODEL),
        pl.BlockSpec((None, N_MOD, D_MODEL), lambda i, b: (row(b), 0, 0)),
        wspec((4, D_MODEL)),
        wspec((D_MODEL, R_END)),
        wspec((T_END, D_MODEL)),
        wspec((1, MLA_Q_RANK)),
        wspec((1, MLA_KV_RANK)),
        wspec((MLA_KV_RANK, 1)),
        wspec((MLA_Q_RANK, MLA_HEADS * MLA_QK)),
        wspec((MLA_HEADS * MLA_NOPE, MLA_KV_RANK)),
        wspec((MLA_KV_RANK, MLA_HEADS * MLA_V)),
        pl.BlockSpec((tm, P_END), lambda i, b: (i, 0)),
        pl.BlockSpec((tm, P_END), lambda i, b: (i, 0)),
        pl.BlockSpec((1, P_END), lambda i, b: (0, 0)),
        pl.BlockSpec((MLA_ROPE // 2, tm), lambda i, b: (0, i)),
        pl.BlockSpec((MLA_ROPE // 2, tm), lambda i, b: (0, i)),
    ]
    widths = [(MLA_HEADS * MLA_QK, BF), (256, BF), (256, BF), (MIX_D, F32),
              (256, BF), (128, BF), (256, BF), (128, BF)]
    out_shape = [jax.ShapeDtypeStruct((B, L, w), dt) for w, dt in widths]
    out_specs = [tok(w) for w, _ in widths]
    out_shape += [jax.ShapeDtypeStruct((B, MLA_HEADS * MLA_QK, L), BF),
                  jax.ShapeDtypeStruct((B, 2 * DIFF_HEADS * DIFF_QK, L), BF)]
    out_specs += [tokt(MLA_HEADS * MLA_QK), tokt(2 * DIFF_HEADS * DIFF_QK)]
    return pl.pallas_call(
        _inproj_kernel,
        out_shape=out_shape,
        grid=(nt, B),
        in_specs=in_specs,
        out_specs=out_specs,
        compiler_params=_params(("arbitrary", "arbitrary")),
        name="inproj",
    )(x, mod_l, pw["norm_g"], pw["w_r"], pw["w_kt"], pw["q_gain"], pw["kv_gain_r"], pw["kv_gain_c"],
      pw["w_uq"], pw["w_ukt"], pw["w_uv"], tab_c, tab_s, msk, cos_t, sin_t)


def _softmax_pv(ss, vs, extra=None):
    mc = None
    for s in ss:
        for c in range(0, s.shape[1], KEY_CHUNK):
            blk = s[:, c:c + KEY_CHUNK]
            mc = blk if mc is None else jnp.maximum(mc, blk)
    m = jnp.max(mc, axis=-1, keepdims=True)
    if extra is not None:
        m = jnp.maximum(m, extra)
    lacc = None
    r = None
    for s, v in zip(ss, vs):
        for c in range(0, s.shape[1], KEY_CHUNK):
            e = jnp.exp2(s[:, c:c + KEY_CHUNK] - m)
            rs = _dot(e.astype(BF), v[c:c + KEY_CHUNK, :])
            lacc = e if lacc is None else lacc + e
            r = rs if r is None else r + rs
    l = jnp.sum(lacc, axis=-1, keepdims=True)
    if extra is not None:
        l = l + jnp.exp2(extra - m)
    return r / l


def _attn_heads(heads):
    outs = []
    prev = None
    for step in range(len(heads) + 1):
        cur = heads[step] if step < len(heads) else None
        kts = (cur if cur is not None else heads[step - 1])[1]
        cur_s, mc = [], None
        if prev is not None:
            prev_s, prev_m, prev_vs = prev
            lacc, r = None, None
        idx = 0
        for seg, kt in enumerate(kts):
            for c in range(0, kt.shape[1], KEY_CHUNK):
                if cur is not None:
                    s = _dot(cur[0], cur[1][seg][:, c:c + KEY_CHUNK])
                    cur_s.append(s)
                    mc = s if mc is None else jnp.maximum(mc, s)
                if prev is not None:
                    e = jnp.exp2(prev_s[idx] - prev_m)
                    rs = _dot(e.astype(BF), prev_vs[seg][c:c + KEY_CHUNK, :])
                    lacc = e if lacc is None else lacc + e
                    r = rs if r is None else r + rs
                idx += 1
        if prev is not None:
            outs.append(r / jnp.sum(lacc, axis=-1, keepdims=True))
        prev = (cur_s, jnp.max(mc, axis=-1, keepdims=True), cur[2]) if cur is not None else None
    return outs


def _mla_kernel(*refs, nseg):
    q_ref = refs[0]
    kt_refs = refs[1:1 + nseg]
    v_refs = refs[1 + nseg:1 + 2 * nseg]
    o_ref = refs[1 + 2 * nseg]
    q = q_ref[...]
    lane = lax.broadcasted_iota(jnp.int32, (1, LANES), 1)
    heads = []
    for h in range(MLA_HEADS):
        sl = slice(h * MLA_QK, (h + 1) * MLA_QK)
        vs = [v[:, (h // 2) * LANES:(h // 2 + 1) * LANES] for v in v_refs]
        heads.append((q[:, sl], [kt.at[sl, :] for kt in kt_refs], vs))
    outs = _attn_heads(heads)
    blocks = [jnp.where(lane < MLA_V, outs[2 * p], outs[2 * p + 1]) for p in range(MLA_HEADS // 2)]
    o_ref[...] = jnp.concatenate(blocks, axis=-1).astype(BF)


def _diff_kernel(*refs, nseg, lam_init):
    q_ref = refs[0]
    kt_refs = refs[1:1 + nseg]
    v_refs = refs[1 + nseg:1 + 2 * nseg]
    dl_ref, g_ref, o_ref = refs[1 + 2 * nseg:]
    q = q_ref[...]
    dl = dl_ref[...]
    lam = (jnp.exp(jnp.sum(dl[0:1] * dl[1:2], axis=-1, keepdims=True))
           - jnp.exp(jnp.sum(dl[2:3] * dl[3:4], axis=-1, keepdims=True)) + lam_init)
    lane = lax.broadcasted_iota(jnp.int32, (1, LANES), 1)
    lo = lane < DIFF_V
    heads = []
    for j in range(2 * DIFF_HEADS):
        sl = slice(j * DIFF_QK, (j + 1) * DIFF_QK)
        vs = [v[:, (j // 4) * LANES:(j // 4 + 1) * LANES] for v in v_refs]
        heads.append((q[:, sl], [kt.at[sl, :] for kt in kt_refs], vs))
    comps = _attn_heads(heads)
    blocks = []
    for p in range(DIFF_HEADS // 2):
        pair = [comps[2 * h] - lam * comps[2 * h + 1] for h in (2 * p, 2 * p + 1)]
        o = jnp.where(lo, pair[0], pair[1])
        o2 = o * o
        ms_lo = jnp.sum(jnp.where(lo, o2, 0.0), axis=-1, keepdims=True) * (1.0 / DIFF_V)
        ms_hi = jnp.sum(jnp.where(lo, 0.0, o2), axis=-1, keepdims=True) * (1.0 / DIFF_V)
        ms = jnp.where(lo, ms_lo, ms_hi)
        blocks.append(o * lax.rsqrt(ms + EPS) * g_ref[...] * (1.0 - lam_init))
    o_ref[...] = jnp.concatenate(blocks, axis=-1).astype(BF)


def _dense_attn(kind, layer, q, kts, vs, tq, extra_in=(), lam_init=0.0):
    B, L, wq = q.shape
    nseg = len(kts)
    in_specs = [pl.BlockSpec((None, tq, wq), lambda b, i: (b, i, 0))]
    for kt in kts:
        in_specs.append(pl.BlockSpec((None,) + kt.shape[1:], lambda b, i: (b, 0, 0)))
    for v in vs:
        in_specs.append(pl.BlockSpec((None,) + v.shape[1:], lambda b, i: (b, 0, 0)))
    if kind == "mla":
        body = functools.partial(_mla_kernel, nseg=nseg)
    else:
        body = functools.partial(_diff_kernel, nseg=nseg, lam_init=lam_init)
        dl, g = extra_in
        in_specs.append(pl.BlockSpec((None, 4, DIFF_QK), lambda b, i: (layer, 0, 0)))
        in_specs.append(pl.BlockSpec((None, 1, LANES), lambda b, i: (layer, 0, 0)))
    return pl.pallas_call(
        body,
        out_shape=jax.ShapeDtypeStruct((B, L, 256), BF),
        grid=(B, L // tq),
        in_specs=in_specs,
        out_specs=pl.BlockSpec((None, tq, 256), lambda b, i: (b, i, 0)),
        compiler_params=_params(("arbitrary", "arbitrary")),
        name=kind,
    )(q, *kts, *vs, *extra_in)


def _swa_kernel(*refs, layer, tq, seq_len, local):
    if local:
        sink_ref, q_ref, kl_ref, vl_ref, kc_ref, vc_ref, o_ref = refs
    else:
        sink_ref, q_ref, kc_ref, vc_ref, o_ref = refs
    i = pl.program_id(1)
    q = q_ref[...]
    lane = lax.broadcasted_iota(jnp.int32, (1, LANES), 1)
    rows = lax.broadcasted_iota(jnp.int32, (2 * tq, 1), 0)
    top = rows < tq
    if local:
        win = tq + 2 * SWA_WINDOW
        q0 = i * tq
        ws = pl.multiple_of(jnp.clip(q0 - SWA_WINDOW, 0, seq_len - win), LANES)
        kwin = kl_ref[pl.ds(ws, win), :]
        vwin = vl_ref[pl.ds(ws, win), :]
        qpos = q0 + jnp.where(top, rows, rows - tq)
        kpos = ws + lax.broadcasted_iota(jnp.int32, (1, win), 1)
        band = jnp.abs(kpos - qpos) <= SWA_WINDOW
    blocks = []
    for k in range(SWA_KV_HEADS):
        q2 = jnp.concatenate([q[:, (2 * k) * SWA_HEAD_DIM:(2 * k + 1) * SWA_HEAD_DIM],
                              q[:, (2 * k + 1) * SWA_HEAD_DIM:(2 * k + 2) * SWA_HEAD_DIM]], axis=0)
        sl = slice(k * SWA_HEAD_DIM, (k + 1) * SWA_HEAD_DIM)
        sk = jnp.where(top, sink_ref[layer, 2 * k], sink_ref[layer, 2 * k + 1]) * LOG2E
        ss, vs = [_dot_nt(q2, kc_ref[:, sl])], [vc_ref[...]]
        if local:
            ss.append(jnp.where(band, _dot_nt(q2, kwin[:, sl]), NEG))
            vs.append(vwin)
        r = _softmax_pv(ss, vs, extra=sk)
        r_top, r_bot = r[:tq], r[tq:]
        if k == 0:
            r_bot = pltpu.roll(r_bot, SWA_HEAD_DIM, axis=1)
        else:
            r_top = pltpu.roll(r_top, SWA_HEAD_DIM, axis=1)
        blocks.append(jnp.where(lane < SWA_HEAD_DIM, r_top, r_bot))
    o_ref[...] = jnp.concatenate(blocks, axis=-1).astype(BF)


def _swa(layer, sink, q, k_loc, v_loc, k_ctx, v_ctx, tq):
    B, L, _ = q.shape
    C = k_ctx.shape[1]
    local = k_loc is not None
    full = lambda n: pl.BlockSpec((None, n, 128), lambda b, i: (b, 0, 0))
    in_specs = [pl.BlockSpec(memory_space=pltpu.SMEM),
                pl.BlockSpec((None, tq, 256), lambda b, i: (b, i, 0))]
    args = [sink, q]
    if local:
        in_specs += [full(L), full(L)]
        args += [k_loc, v_loc]
    in_specs += [full(C), full(C)]
    args += [k_ctx, v_ctx]
    return pl.pallas_call(
        functools.partial(_swa_kernel, layer=layer, tq=tq, seq_len=L, local=local),
        out_shape=jax.ShapeDtypeStruct((B, L, 256), BF),
        grid=(B, L // tq),
        in_specs=in_specs,
        out_specs=pl.BlockSpec((None, tq, 256), lambda b, i: (b, i, 0)),
        compiler_params=_params(("arbitrary", "arbitrary")),
        name="swa",
    )(*args)


def _pool_diffs(u, prev, nxt, i, nt, tm, seq_len):
    n = tm + 2 * POOL_HALO
    ue = jnp.concatenate([jnp.where(i > 0, prev, 0.0), u, jnp.where(i < nt - 1, nxt, 0.0)], axis=0)

    def sh(a, k):
        return pltpu.roll(a, k % n, axis=0)

    s2 = ue + sh(ue, 1)
    s4 = sh(s2, 1) + sh(s2, -1)
    s8 = sh(s4, 2) + sh(s4, -2)
    s16 = sh(s8, 4) + sh(s8, -4)
    lane = lax.broadcasted_iota(jnp.int32, (1, MIX_D), 1)
    t = i * tm + lax.broadcasted_iota(jnp.int32, (tm, 1), 0)
    win = None
    cnt = None
    for gi, (w, s) in enumerate(zip(POOL_WINDOWS, (s2, s4, s8, s16))):
        half = w // 2
        c = (jnp.minimum(t - half + w, seq_len) - jnp.maximum(t - half, 0)).astype(F32)
        sv = s[POOL_HALO:POOL_HALO + tm]
        if win is None:
            win, cnt = sv, c
        else:
            sel = lane >= gi * POOL_GROUP
            win = jnp.where(sel, sv, win)
            cnt = jnp.where(sel, c, cnt)
    return win / cnt - u


def _outproj_kernel(oa_ref, ob_ref, oc_ref, u_ref, up_ref, un_ref, x_ref, mod_ref, ng_ref,
                    wo_ref, pw_ref, ps_ref, rw_ref, rb_ref,
                    xo_ref, hr_ref, rt_ref, cnt_ref, carry_ref, *, tm, seq_len, nt):
    i = pl.program_id(1)

    @pl.when((pl.program_id(0) == 0) & (i == 0))
    def _():
        carry_ref[...] = jnp.zeros_like(carry_ref)

    u = u_ref[...]
    dg = _pool_diffs(u, up_ref[...], un_ref[...], i, nt, tm, seq_len)
    od = _dot(dg.astype(BF), pw_ref[...]) * ps_ref[...]
    mix = jnp.concatenate([oa_ref[...], ob_ref[...], oc_ref[...], od.astype(BF)], axis=-1)
    ox = _dot(mix, wo_ref[...])
    mod = mod_ref[...]
    x = x_ref[...] + mod[2:3, :] * _rms(ox, ng_ref[1:2, :])
    xo_ref[...] = x
    h = _rms(x, ng_ref[2:3, :]) * (1.0 + mod[4:5, :]) + mod[3:4, :]
    hr_ref[...] = h.reshape((tm,) + TOK_TILE)
    hh = h.astype(BF)
    hl = (h - hh.astype(F32)).astype(BF)
    hw = _dot(hh, rw_ref[...])
    logits = hw[:, :LANES] + hw[:, LANES:] + _dot(hl, rw_ref[:, :LANES]) + rb_ref[...]

    lane = lax.broadcasted_iota(jnp.int32, (1, LANES), 1)
    lanef = lane.astype(F32)
    is_g = (lane >= N_EXPERTS) & (lane < N_EXPERTS + N_EXPERT_GROUPS)
    gl = jnp.where(is_g, logits, NEG)
    gmax = jnp.max(gl, axis=-1, keepdims=True)
    gsel = jnp.min(jnp.where(gl == gmax, lanef, 1e9), axis=-1, keepdims=True) - N_EXPERTS
    p_g = 1.0 / jnp.sum(jnp.where(is_g, jnp.exp(gl - gmax), 0.0), axis=-1, keepdims=True)
    grp = jnp.right_shift(lane, 2).astype(F32)
    el = jnp.where((lane < N_EXPERTS) & (grp == gsel), logits, NEG)
    v1 = jnp.max(el, axis=-1, keepdims=True)
    i1 = jnp.min(jnp.where(el == v1, lanef, 1e9), axis=-1, keepdims=True)
    el2 = jnp.where(lanef == i1, NEG, el)
    v2 = jnp.max(el2, axis=-1, keepdims=True)
    i2 = jnp.min(jnp.where(el2 == v2, lanef, 1e9), axis=-1, keepdims=True)
    e21 = jnp.exp(v2 - v1)
    w1 = p_g / (1.0 + e21)
    w2 = w1 * e21

    l1 = i1 - EXPERTS_PER_GROUP * gsel
    l2 = i2 - EXPERTS_PER_GROUP * gsel
    code = jnp.minimum(l1, l2) * EXPERTS_PER_GROUP + jnp.maximum(l1, l2)
    pidx = jnp.full_like(code, float(PAIRS_PER_GROUP - 1))
    slot_a = jnp.full_like(code, float(PAIR_A[-1]))
    for p in range(PAIRS_PER_GROUP - 1):
        lo_e, hi_e = min(PAIR_A[p], PAIR_B[p]), max(PAIR_A[p], PAIR_B[p])
        hit = code == float(lo_e * EXPERTS_PER_GROUP + hi_e)
        pidx = jnp.where(hit, float(p), pidx)
        slot_a = jnp.where(hit, float(PAIR_A[p]), slot_a)
    first_is_a = l1 == slot_a
    w_a = jnp.where(first_is_a, w1, w2)
    w_b = jnp.where(first_is_a, w2, w1)
    bucket = gsel * PAIRS_PER_GROUP + pidx

    onehot = lanef == bucket
    rr = lax.broadcasted_iota(jnp.int32, (tm, tm), 0)
    cc = lax.broadcasted_iota(jnp.int32, (tm, tm), 1)
    tri = jnp.where(rr >= cc, 1.0, 0.0).astype(BF)
    prefix = _dot(tri, jnp.where(onehot, 1.0, 0.0).astype(BF))
    carry = carry_ref[...]
    rank = jnp.sum(jnp.where(onehot, prefix - 1.0 + carry, 0.0), axis=-1, keepdims=True)
    carry = carry + prefix[tm - 1:tm, :]
    carry_ref[...] = carry
    cnt_ref[...] = carry
    rt_ref[...] = jnp.where(lane == 0, w_a, jnp.where(lane == 1, w_b, jnp.where(
        lane == 2, bucket, jnp.where(lane == 3, rank, 0.0))))


def _outproj(layer, oa, ob, oc, u, x, mod_l, mod_row, tm, pw):
    B, L, _ = x.shape
    nt = L // tm
    hb = tm // POOL_HALO
    nhb = L // POOL_HALO
    row = (lambda b: b) if mod_row is None else (lambda b: mod_row)

    def wspec(shape):
        nd = len(shape)
        return pl.BlockSpec((None,) + shape, lambda b, i: (layer,) + (0,) * nd)

    tok = lambda w: pl.BlockSpec((None, tm, w), lambda b, i: (b, i, 0))
    in_specs = [
        tok(256), tok(256), tok(256), tok(MIX_D),
        pl.BlockSpec((None, POOL_HALO, MIX_D), lambda b, i: (b, jnp.maximum(i * hb - 1, 0), 0)),
        pl.BlockSpec((None, POOL_HALO, MIX_D), lambda b, i: (b, jnp.minimum((i + 1) * hb, nhb - 1), 0)),
        tok(D_MODEL),
        pl.BlockSpec((None, N_MOD, D_MODEL), lambda b, i: (row(b), 0, 0)),
        wspec((4, D_MODEL)),
        wspec((D_MODEL, D_MODEL)),
        wspec((MIX_D, MIX_D)),
        wspec((1, MIX_D)),
        wspec((D_MODEL, 2 * LANES)),
        wspec((1, LANES)),
    ]
    return pl.pallas_call(
        functools.partial(_outproj_kernel, tm=tm, seq_len=L, nt=nt),
        out_shape=[jax.ShapeDtypeStruct((B, L, D_MODEL), F32),
                   jax.ShapeDtypeStruct((B * L,) + TOK_TILE, F32),
                   jax.ShapeDtypeStruct((B, L, LANES), F32),
                   jax.ShapeDtypeStruct((1, LANES), F32)],
        grid=(B, nt),
        in_specs=in_specs,
        out_specs=[tok(D_MODEL),
                   pl.BlockSpec((tm,) + TOK_TILE, lambda b, i: (b * nt + i, 0, 0)),
                   tok(LANES), pl.BlockSpec((1, LANES), lambda b, i: (0, 0))],
        scratch_shapes=[pltpu.VMEM((1, LANES), F32)],
        compiler_params=_params(("arbitrary", "arbitrary")),
        name="outproj",
    )(oa, ob, oc, u, u, u, x, mod_l, pw["norm_g"], pw["w_out"], pw["pool_bd"], pw["pool_scale"],
      pw["r_w"], pw["r_b"])


def _route_tables(route, counts, n_tiles):
    T = route.shape[0] * route.shape[1]
    bucket = route[:, :, 2].reshape(T).astype(jnp.int32)
    rank = route[:, :, 3].reshape(T).astype(jnp.int32)
    cnt = counts[0, :N_BUCKETS].astype(jnp.int32)
    tiles_b = (cnt + MOE_TM - 1) // MOE_TM
    ends = jnp.cumsum(tiles_b)
    starts = ends - tiles_b
    pos = jnp.take(starts * MOE_TM, bucket) + rank
    n_used = ends[-1]
    tile = jnp.arange(n_tiles, dtype=jnp.int32)
    src = jnp.minimum(tile, n_used - 1)
    tb = jnp.sum((src[:, None] >= ends[None, :]).astype(jnp.int32), axis=1)
    tb = jnp.minimum(tb, N_BUCKETS - 1)
    grp, pidx = tb // PAIRS_PER_GROUP, tb % PAIRS_PER_GROUP
    ea = grp * EXPERTS_PER_GROUP + jnp.take(jnp.array(PAIR_A, jnp.int32), pidx)
    eb = grp * EXPERTS_PER_GROUP + jnp.take(jnp.array(PAIR_B, jnp.int32), pidx)
    zlast = jnp.where(tiles_b > 0, (ends - 1) * MOE_TM, -1)
    idle = n_used + jnp.arange(N_BUCKETS, dtype=jnp.int32)
    zidle = jnp.where(idle < n_tiles, idle * MOE_TM, -1)
    pad_lo = jnp.concatenate([starts * MOE_TM + cnt, (n_used * MOE_TM)[None]])
    pad_hi = jnp.concatenate([ends * MOE_TM, jnp.full((1,), n_tiles * MOE_TM, jnp.int32)])
    zrow = jnp.concatenate([zlast, zidle, pad_lo, pad_hi]).astype(jnp.int32)
    return pos, zrow, src, ea, eb, n_used.reshape(1).astype(jnp.int32)


def _row_copy(src_ref, src_row, dst_ref, dst_row, sem):
    return pltpu.make_async_copy(src_ref.at[src_row], dst_ref.at[dst_row], sem)


def _scatter_kernel(pos_ref, zrow_ref, hr_ref, rt_ref, hs_ref, ws_ref, inv_ref, zbuf_ref, zw_ref, sem, zsem,
                    *, tm):
    i = pl.program_id(0)

    @pl.when(i == 0)
    def _():
        zbuf_ref[...] = jnp.zeros_like(zbuf_ref)
        zw_ref[...] = jnp.zeros_like(zw_ref)
        for start in (True, False):
            for b in range(2 * N_BUCKETS):
                @pl.when(zrow_ref[b] >= 0)
                def _():
                    z0 = pl.multiple_of(zrow_ref[b], MOE_TM)
                    for cp in (pltpu.make_async_copy(zbuf_ref, hs_ref.at[pl.ds(z0, MOE_TM)], zsem),
                               pltpu.make_async_copy(zw_ref, ws_ref.at[pl.ds(z0, MOE_TM)], zsem)):
                        cp.start() if start else cp.wait()

        def no_token(s, carry):
            inv_ref[s] = -1
            return carry

        for b in range(N_BUCKETS + 1):
            lax.fori_loop(zrow_ref[2 * N_BUCKETS + b], zrow_ref[3 * N_BUCKETS + 1 + b], no_token, 0)

    def copies(r):
        p = pos_ref[i * tm + r]
        return (_row_copy(hr_ref, r, hs_ref, p, sem),
                pltpu.make_async_copy(rt_ref.at[pl.ds(r, 1)], ws_ref.at[pl.ds(p, 1)], sem))

    _row_dma_loop(tm, copies)

    def invert(r, carry):
        inv_ref[pos_ref[i * tm + r]] = i * tm + r
        return carry

    lax.fori_loop(0, tm, invert, 0, unroll=DMA_UNROLL)


def _row_dma_loop(n, copies):
    def issue(g, carry):
        for k in range(DMA_UNROLL):
            for cp in copies(g * DMA_UNROLL + k):
                cp.start(priority=k % 2)
        return carry

    lax.fori_loop(0, n // DMA_UNROLL, issue, 0)

    def drain(g, carry):
        for k in range(DMA_UNROLL):
            for cp in copies(g * DMA_UNROLL + k):
                cp.wait()
        return carry

    lax.fori_loop(0, n // DMA_UNROLL, drain, 0)


def _scatter_rows(hr, route, pos, zrow, n_tiles, tm):
    T = hr.shape[0]
    return pl.pallas_call(
        functools.partial(_scatter_kernel, tm=tm),
        out_shape=[jax.ShapeDtypeStruct((n_tiles * MOE_TM,) + TOK_TILE, F32),
                   jax.ShapeDtypeStruct((n_tiles * MOE_TM, LANES), F32),
                   jax.ShapeDtypeStruct((n_tiles * MOE_TM,), jnp.int32)],
        grid_spec=pltpu.PrefetchScalarGridSpec(
            num_scalar_prefetch=2,
            grid=(T // tm,),
            in_specs=[pl.BlockSpec((tm,) + TOK_TILE, lambda i, pos, zrow: (i, 0, 0)),
                      pl.BlockSpec((tm, LANES), lambda i, pos, zrow: (i, 0))],
            out_specs=[pl.BlockSpec(memory_space=pl.ANY), pl.BlockSpec(memory_space=pl.ANY),
                       pl.BlockSpec(memory_space=pltpu.SMEM)],
            scratch_shapes=[pltpu.VMEM((MOE_TM,) + TOK_TILE, F32), pltpu.VMEM((MOE_TM, LANES), F32),
                            pltpu.SemaphoreType.DMA(()), pltpu.SemaphoreType.DMA(())]),
        compiler_params=_params(("arbitrary",)),
        name="moe_scatter",
    )(pos, zrow, hr, route)


def _tile_row_copies(j, inv_ref, obuf_ref, ys_ref, sem, n_tok, start):
    slot = j % 2
    for r in range(MOE_TM):
        tok = inv_ref[j * MOE_TM + r]
        dst = jnp.where(tok < 0, n_tok + slot * MOE_TM + r, tok)
        cp = _row_copy(obuf_ref.at[slot], r, ys_ref, dst, sem.at[slot])
        cp.start(priority=r % 2) if start else cp.wait()


def _moe_kernel(src_ref, ea_ref, eb_ref, nu_ref, inv_ref, hs_ref, ws_ref,
                wgua_ref, wgub_ref, wda_ref, wdb_ref, ys_ref, obuf_ref, sem, *, n_tok):
    j = pl.program_id(0)
    last = pl.num_programs(0) - 1
    copies = functools.partial(_tile_row_copies, inv_ref=inv_ref, obuf_ref=obuf_ref, ys_ref=ys_ref,
                               sem=sem, n_tok=n_tok)

    @pl.when(j == 0)
    def _():
        obuf_ref[...] = jnp.zeros_like(obuf_ref)
        for slot in range(2):
            cp = pltpu.make_async_copy(obuf_ref.at[slot], ys_ref.at[pl.ds(n_tok + slot * MOE_TM, MOE_TM)],
                                       sem.at[slot])
            cp.start()
            cp.wait()

    n_used = nu_ref[0]

    def experts():
        h = hs_ref[...].reshape(MOE_TM, D_MODEL).astype(BF)
        w = ws_ref[...]
        gua = _dot(h, wgua_ref[...])
        gub = _dot(h, wgub_ref[...])
        ga, gb = gua[:, :D_EXPERT], gub[:, :D_EXPERT]
        ya = _dot(((ga * jax.nn.sigmoid(ga)) * gua[:, D_EXPERT:]).astype(BF), wda_ref[...])
        yb = _dot(((gb * jax.nn.sigmoid(gb)) * gub[:, D_EXPERT:]).astype(BF), wdb_ref[...])
        obuf_ref[j % 2] = (w[:, 0:1] * ya + w[:, 1:2] * yb).reshape((MOE_TM,) + TOK_TILE)

    @pl.when((j >= 2) & (j - 2 < n_used))
    def _():
        copies(j - 2, start=False)

    @pl.when((j >= 1) & (j < n_used))
    def _():
        copies(j - 1, start=True)
        experts()

    @pl.when((j == 0) & (j < n_used))
    def _():
        experts()

    @pl.when((j >= 1) & (j == n_used))
    def _():
        copies(j - 1, start=True)

    @pl.when((j == last) & (j < n_used))
    def _():
        copies(j, start=True)

    @pl.when((j == last) & (j - 1 < n_used))
    def _():
        copies(j - 1, start=False)

    @pl.when((j == last) & (j < n_used))
    def _():
        copies(j, start=False)


def _moe_sorted(layer, n_tok, hs, ws, src, ea, eb, n_used, inv, pw):
    n_tiles = hs.shape[0] // MOE_TM
    assert n_tiles >= 2
    wgu = lambda sel: pl.BlockSpec((None, None, D_MODEL, 2 * D_EXPERT),
                                   lambda j, src, ea, eb, *_: (layer, (ea, eb)[sel][j], 0, 0))
    wd = lambda sel: pl.BlockSpec((None, None, D_EXPERT, D_MODEL),
                                  lambda j, src, ea, eb, *_: (layer, (ea, eb)[sel][j], 0, 0))
    return pl.pallas_call(
        functools.partial(_moe_kernel, n_tok=n_tok),
        out_shape=jax.ShapeDtypeStruct((n_tok + 2 * MOE_TM,) + TOK_TILE, F32),
        grid_spec=pltpu.PrefetchScalarGridSpec(
            num_scalar_prefetch=5,
            grid=(n_tiles,),
            in_specs=[pl.BlockSpec((MOE_TM,) + TOK_TILE, lambda j, src, *_: (src[j], 0, 0)),
                      pl.BlockSpec((MOE_TM, LANES), lambda j, src, *_: (src[j], 0)),
                      wgu(0), wgu(1), wd(0), wd(1)],
            out_specs=pl.BlockSpec(memory_space=pl.ANY),
            scratch_shapes=[pltpu.VMEM((2, MOE_TM) + TOK_TILE, F32), pltpu.SemaphoreType.DMA((2,))]),
        compiler_params=_params(("arbitrary",)),
        name="moe",
    )(src, ea, eb, n_used, inv, hs, ws, pw["w_gu"], pw["w_gu"], pw["w_down"], pw["w_down"])


def _residual_kernel(ys_ref, x_ref, mod_ref, ng_ref, o_ref, *, tm):
    fx = ys_ref[...].reshape(tm, D_MODEL)
    o_ref[...] = x_ref[...] + mod_ref[5:6, :] * _rms(fx, ng_ref[3:4, :])


def _moe_residual(layer, ys, x, mod_l, mod_row, tm, pw):
    B, L, _ = x.shape
    T = B * L
    tiles_per_seq = L // tm
    row = (lambda i: i // tiles_per_seq) if mod_row is None else (lambda i: mod_row)
    out = pl.pallas_call(
        functools.partial(_residual_kernel, tm=tm),
        out_shape=jax.ShapeDtypeStruct((T, D_MODEL), F32),
        grid=(T // tm,),
        in_specs=[pl.BlockSpec((tm,) + TOK_TILE, lambda i: (i, 0, 0)),
                  pl.BlockSpec((tm, D_MODEL), lambda i: (i, 0)),
                  pl.BlockSpec((None, N_MOD, D_MODEL), lambda i: (row(i), 0, 0)),
                  pl.BlockSpec((None, 4, D_MODEL), lambda i: (layer, 0, 0))],
        out_specs=pl.BlockSpec((tm, D_MODEL), lambda i: (i, 0)),
        compiler_params=_params(("arbitrary",)),
        name="moe_residual",
    )(ys, x.reshape(T, D_MODEL), mod_l, pw["norm_g"])
    return out.reshape(B, L, D_MODEL)


def _moe(layer, hr, route, counts, x, mod_l, mod_row, tm, pw):
    B, L, _ = x.shape
    T = B * L
    n_tiles = T // MOE_TM + N_BUCKETS
    pos, zrow, src, ea, eb, n_used = _route_tables(route, counts, n_tiles)
    hs, ws, inv = _scatter_rows(hr, route.reshape(T, LANES), pos, zrow, n_tiles, tm)
    ys = _moe_sorted(layer, T, hs, ws, src, ea, eb, n_used, inv, pw)
    return _moe_residual(layer, ys, x, mod_l, mod_row, tm, pw)


def _axial(length, dim):
    rows = length // GRID_W
    row = jnp.repeat(jnp.arange(rows), GRID_W).astype(F32)
    col = jnp.tile(jnp.arange(GRID_W), rows).astype(F32)
    n = dim // 4
    inv = ROPE_BASE ** (-jnp.arange(n, dtype=F32) / n)
    ang = jnp.concatenate([row[:, None] * inv, col[:, None] * inv], axis=-1)
    return jnp.cos(ang), jnp.sin(ang)


def _rope_tables(length, rotate):
    if rotate:
        cos_a, sin_a = _axial(length, MLA_ROPE)
        cos_b, sin_b = _axial(length, SWA_HEAD_DIM)
    else:
        cos_a, sin_a = jnp.ones((length, MLA_ROPE // 2), F32), jnp.zeros((length, MLA_ROPE // 2), F32)
        cos_b, sin_b = jnp.ones((length, SWA_HEAD_DIM // 2), F32), jnp.zeros((length, SWA_HEAD_DIM // 2), F32)
    one = jnp.ones((length, MLA_NOPE), F32)
    zero = jnp.zeros((length, MLA_NOPE), F32)
    sa, sb, sc = MLA_SCALE * LOG2E, SWA_SCALE * LOG2E, DIFF_SCALE * LOG2E
    c_qa = jnp.tile(jnp.concatenate([one, cos_a, cos_a], -1), (1, MLA_HEADS)) * sa
    s_qa = jnp.tile(jnp.concatenate([zero, -sin_a, sin_a], -1), (1, MLA_HEADS)) * sa
    c_b = jnp.concatenate([cos_b, cos_b], -1)
    s_b = jnp.concatenate([-sin_b, sin_b], -1)
    c_c = jnp.tile(jnp.concatenate([cos_a, cos_a], -1), (1, 2 * DIFF_HEADS)) * sc
    s_c = jnp.tile(jnp.concatenate([-sin_a, sin_a], -1), (1, 2 * DIFF_HEADS)) * sc
    tab_c = jnp.concatenate([c_qa, jnp.tile(c_b, (1, SWA_HEADS)) * sb, jnp.tile(c_b, (1, SWA_KV_HEADS)), c_c], -1)
    tab_s = jnp.concatenate([s_qa, jnp.tile(s_b, (1, SWA_HEADS)) * sb, jnp.tile(s_b, (1, SWA_KV_HEADS)), s_c], -1)
    m_qa = jnp.tile(jnp.concatenate([jnp.zeros((MLA_NOPE,)), jnp.ones((MLA_ROPE // 2,)),
                                     jnp.zeros((MLA_ROPE // 2,))]), MLA_HEADS)
    m_b = jnp.concatenate([jnp.ones((SWA_HEAD_DIM // 2,)), jnp.zeros((SWA_HEAD_DIM // 2,))])
    m_c = jnp.concatenate([jnp.ones((DIFF_QK // 2,)), jnp.zeros((DIFF_QK // 2,))])
    msk = jnp.concatenate([m_qa, jnp.tile(m_b, SWA_HEADS + SWA_KV_HEADS), jnp.tile(m_c, 2 * DIFF_HEADS)])
    return tab_c, tab_s, msk.astype(F32)[None, :], cos_a.T, sin_a.T


def _prep_weights(norm_g, w_in, mla_q_gain, mla_kv_gain, mla_w_uq, mla_w_ukv, diff_subln, pool_w,
                  pool_scale, w_out, router_gw, router_gb, router_ew, router_eb, exp_w_gu, exp_w_down):
    sizes = (MLA_Q_RANK, MLA_KV_RANK, MLA_ROPE, 256, 128, 128, 256, 256, 256, MIX_D)
    offs = [0]
    for s in sizes:
        offs.append(offs[-1] + s)
    cq, ckv, kr, bq, bk, bv, dq, dk, dv, u = [w_in[:, :, offs[j]:offs[j + 1]] for j in range(len(sizes))]
    w_r = jnp.concatenate([cq, bq, dq, dv, u, ckv, bk, bv], axis=-1).astype(BF)
    w_kt = jnp.swapaxes(jnp.concatenate([ckv, kr, dk], axis=-1), 1, 2).astype(BF)
    ukv = mla_w_ukv.reshape(DEPTH, MLA_KV_RANK, MLA_HEADS, MLA_NOPE + MLA_V)
    w_uk = ukv[..., :MLA_NOPE].reshape(DEPTH, MLA_KV_RANK, MLA_HEADS * MLA_NOPE)
    w_uv = ukv[..., MLA_NOPE:].reshape(DEPTH, MLA_KV_RANK, MLA_HEADS * MLA_V)
    eye = jnp.eye(len(POOL_WINDOWS), dtype=F32)
    pool_bd = jnp.einsum('lgcd,gh->lgchd', pool_w, eye).reshape(DEPTH, MIX_D, MIX_D)
    r_w = jnp.concatenate([router_ew, router_gw], axis=-1)
    r_w = jnp.pad(r_w, ((0, 0), (0, 0), (0, LANES - r_w.shape[-1])))
    r_hi = r_w.astype(BF)
    r_lo = (r_w - r_hi.astype(F32)).astype(BF)
    r_b = jnp.concatenate([router_eb, router_gb], axis=-1)
    r_b = jnp.pad(r_b, ((0, 0), (0, LANES - r_b.shape[-1])))[:, None, :]
    return dict(
        norm_g=norm_g, w_r=w_r, w_kt=w_kt,
        q_gain=mla_q_gain[:, None, :], kv_gain_r=mla_kv_gain[:, None, :], kv_gain_c=mla_kv_gain[:, :, None],
        w_uq=mla_w_uq.astype(BF), w_ukt=jnp.swapaxes(w_uk, 1, 2).astype(BF), w_uv=w_uv.astype(BF),
        subln=jnp.tile(diff_subln, (1, LANES // DIFF_V))[:, None, :],
        pool_bd=pool_bd.astype(BF), pool_scale=pool_scale[:, None, :], w_out=w_out.astype(BF),
        r_w=jnp.concatenate([r_hi, r_lo], axis=-1), r_b=r_b, w_gu=exp_w_gu.astype(BF), w_down=exp_w_down.astype(BF))


def kernel(x, c, ctx, c_ctx, ada_w, ada_b, norm_g, w_in, mla_q_gain, mla_kv_gain, mla_w_uq, mla_w_ukv,
           swa_sink, diff_lambda, diff_subln, pool_w, pool_scale, w_out, router_gw, router_gb,
           router_ew, router_eb, exp_w_gu, exp_w_down):
    B, L, D = x.shape
    C = ctx.shape[1]
    assert D == D_MODEL and B + 1 <= MOD_ROWS and L % 512 == 0 and C % 256 == 0
    pw = _prep_weights(norm_g, w_in, mla_q_gain, mla_kv_gain, mla_w_uq, mla_w_ukv, diff_subln, pool_w,
                       pool_scale, w_out, router_gw, router_gb, router_ew, router_eb, exp_w_gu, exp_w_down)
    cc = jnp.concatenate([c, c_ctx[None, :], jnp.zeros((MOD_ROWS - B - 1, D), F32)], axis=0)
    mod = jnp.swapaxes(_adaln(cc, ada_w, ada_b), 1, 2)
    tabs_x = _rope_tables(L, True)
    tabs_c = _rope_tables(C, False)
    tm_x, tm_c, tq, tq_x = 512, 256, 256, 512

    h_ctx = ctx
    for layer in range(DEPTH):
        update_ctx = layer < DEPTH - 1
        lam_init = 0.8 - 0.6 * math.exp(-0.3 * layer)
        mod_l = mod[layer]
        qa, qb, qc, u, va, vb, vc, kb, kta, ktc = _inproj(layer, x, mod_l, None, tm_x, pw, tabs_x)
        cqa, cqb, cqc, cu, cva, cvb, cvc, ckb, ckta, cktc = _inproj(layer, h_ctx, mod_l, B, tm_c, pw, tabs_c)
        diff_extra = (diff_lambda, pw["subln"])

        oa = _dense_attn("mla", layer, qa, [kta, ckta], [va, cva], tq_x)
        ob = _swa(layer, swa_sink, qb, kb, vb, ckb, cvb, tq)
        oc = _dense_attn("diff", layer, qc, [ktc, cktc], [vc, cvc], tq_x, diff_extra, lam_init)
        x, hrx, rtx, cntx = _outproj(layer, oa, ob, oc, u, x, mod_l, None, tm_x, pw)
        if update_ctx:
            coa = _dense_attn("mla", layer, cqa, [ckta], [cva], tq)
            cob = _swa(layer, swa_sink, cqb, None, None, ckb, cvb, tq)
            coc = _dense_attn("diff", layer, cqc, [cktc], [cvc], tq, diff_extra, lam_init)
            h_ctx, hrc, rtc, cntc = _outproj(layer, coa, cob, coc, cu, h_ctx, mod_l, B, tm_c, pw)
        x = _moe(layer, hrx, rtx, cntx, x, mod_l, None, tm_x, pw)
        if update_ctx:
            h_ctx = _moe(layer, hrc, rtc, cntc, h_ctx, mod_l, B, tm_c, pw)
    return x
```

```python
import functools
import math

import jax
import jax.numpy as jnp
from jax import lax
from jax.experimental import pallas as pl
from jax.experimental.pallas import tpu as pltpu

F32 = jnp.float32
BF = jnp.bfloat16

D_MODEL = 1024
DEPTH = 2
GRID_W = 64
EPS = 1e-6
ROPE_BASE = 10000.0
N_MOD = 6

MLA_HEADS = 4
MLA_Q_RANK = 256
MLA_KV_RANK = 128
MLA_NOPE = 64
MLA_ROPE = 32
MLA_V = 64
MLA_QK = MLA_NOPE + MLA_ROPE
MLA_SCALE = MLA_QK ** -0.5

SWA_HEADS = 4
SWA_KV_HEADS = 2
SWA_HEAD_DIM = 64
SWA_WINDOW = 128
SWA_SCALE = SWA_HEAD_DIM ** -0.5

DIFF_HEADS = 4
DIFF_QK = 32
DIFF_V = 64
DIFF_SCALE = DIFF_QK ** -0.5

POOL_WINDOWS = (2, 4, 8, 16)
POOL_GROUP = 64
POOL_HALO = 8
MIX_D = len(POOL_WINDOWS) * POOL_GROUP

N_EXPERT_GROUPS = 4
EXPERTS_PER_GROUP = 4
N_EXPERTS = 16
D_EXPERT = 512
PAIR_A = (0, 2, 2, 3, 3, 3)
PAIR_B = (1, 1, 0, 0, 1, 2)
PAIRS_PER_GROUP = len(PAIR_A)
N_BUCKETS = N_EXPERT_GROUPS * PAIRS_PER_GROUP

LOG2E = math.log2(math.e)
NEG = -1e30
LANES = 128
MOD_ROWS = 24
SUBLANES = 8
TOK_TILE = (D_MODEL // LANES, LANES)
MOE_TM = 256
DMA_UNROLL = 8
KEY_CHUNK = 256
VMEM_LIMIT = 48 * 1024 * 1024

R_CQ, R_BQ, R_DQ, R_DV, R_U, R_CKV, R_BK, R_BV, R_END = 0, 256, 512, 768, 1024, 1280, 1408, 1536, 1664
T_CKV, T_KR, T_DK, T_END = 0, 128, 160, 416
P_QA, P_QB, P_KB, P_QC, P_END = 0, 384, 640, 768, 1024


def _dot(a, b):
    return jnp.dot(a, b, preferred_element_type=F32)


def _dot_nt(a, b):
    return lax.dot_general(a, b, (((1,), (1,)), ((), ())), preferred_element_type=F32)


def _rms(x, g):
    return x * lax.rsqrt(jnp.mean(x * x, axis=-1, keepdims=True) + EPS) * g


def _params(sem):
    return pltpu.CompilerParams(dimension_semantics=sem, vmem_limit_bytes=VMEM_LIMIT)


def _adaln_kernel(c_ref, w_ref, b_ref, o_ref):
    c = c_ref[...]
    s = (c * jax.nn.sigmoid(c)).astype(BF)
    o_ref[...] = _dot(s, w_ref[...].astype(BF)) + b_ref[...]


def _adaln(cc, ada_w, ada_b):
    return pl.pallas_call(
        _adaln_kernel,
        out_shape=jax.ShapeDtypeStruct((DEPTH, N_MOD, MOD_ROWS, D_MODEL), F32),
        grid=(DEPTH, N_MOD),
        in_specs=[
            pl.BlockSpec((MOD_ROWS, D_MODEL), lambda l, j: (0, 0)),
            pl.BlockSpec((None, D_MODEL, D_MODEL), lambda l, j: (l, 0, j)),
            pl.BlockSpec((None, None, 1, D_MODEL), lambda l, j: (l, j, 0, 0)),
        ],
        out_specs=pl.BlockSpec((None, None, MOD_ROWS, D_MODEL), lambda l, j: (l, j, 0, 0)),
        compiler_params=_params(("arbitrary", "arbitrary")),
        name="adaln",
    )(cc, ada_w, ada_b.reshape(DEPTH, N_MOD, 1, D_MODEL))


def _rope_rows(xf, c, s, m1, half):
    w = xf.shape[-1]
    up = pltpu.roll(xf, w - half, axis=1)
    dn = pltpu.roll(xf, half, axis=1)
    return xf * c + jnp.where(m1 > 0.5, up, dn) * s


def _rope_cols(xt, c, s):
    h = xt.shape[0] // 2
    x1, x2 = xt[:h], xt[h:]
    return jnp.concatenate([x1 * c - x2 * s, x1 * s + x2 * c], axis=0)


def _inproj_kernel(x_ref, mod_ref, ng_ref, wr_ref, wkt_ref, qg_ref, kvgr_ref, kvgc_ref,
                   wuq_ref, wukt_ref, wuv_ref, tc_ref, ts_ref, m_ref, ct_ref, st_ref,
                   qa_ref, qb_ref, qc_ref, u_ref, va_ref, vb_ref, vc_ref, kb_ref, kta_ref, ktc_ref):
    x = x_ref[...]
    mod = mod_ref[...]
    hx = _rms(x, ng_ref[0:1, :]) * (1.0 + mod[1:2, :]) + mod[0:1, :]
    hb = hx.astype(BF)
    zr = _dot(hb, wr_ref[...])
    zk = _dot_nt(wkt_ref[...], hb)

    tc = tc_ref[...]
    ts = ts_ref[...]
    m1 = m_ref[...]

    cqn = _rms(zr[:, R_CQ:R_CQ + MLA_Q_RANK], qg_ref[...]).astype(BF)
    qa = _dot(cqn, wuq_ref[...])
    qa_ref[...] = _rope_rows(qa, tc[:, P_QA:P_QB], ts[:, P_QA:P_QB], m1[:, P_QA:P_QB],
                             MLA_ROPE // 2).astype(BF)
    qb_ref[...] = _rope_rows(zr[:, R_BQ:R_DQ], tc[:, P_QB:P_KB], ts[:, P_QB:P_KB], m1[:, P_QB:P_KB],
                             SWA_HEAD_DIM // 2).astype(BF)
    qc_ref[...] = _rope_rows(zr[:, R_DQ:R_DV], tc[:, P_QC:P_END], ts[:, P_QC:P_END], m1[:, P_QC:P_END],
                             DIFF_QK // 2).astype(BF)
    kb_ref[...] = _rope_rows(zr[:, R_BK:R_BV], tc[:, P_KB:P_QC], ts[:, P_KB:P_QC], m1[:, P_KB:P_QC],
                             SWA_HEAD_DIM // 2).astype(BF)
    u_ref[...] = zr[:, R_U:R_CKV]
    vb_ref[...] = zr[:, R_BV:R_END].astype(BF)
    vc_ref[...] = zr[:, R_DV:R_U].astype(BF)

    ckvn = _rms(zr[:, R_CKV:R_BK], kvgr_ref[...]).astype(BF)
    va_ref[...] = _dot(ckvn, wuv_ref[...]).astype(BF)

    ckvt = zk[T_CKV:T_KR]
    ckvtn = ckvt * lax.rsqrt(jnp.mean(ckvt * ckvt, axis=0, keepdims=True) + EPS) * kvgc_ref[...]
    knt = _dot(wukt_ref[...], ckvtn.astype(BF))
    ct = ct_ref[...]
    st = st_ref[...]
    krt = _rope_cols(zk[T_KR:T_DK], ct, st)
    pieces = []
    for h in range(MLA_HEADS):
        pieces.append(knt[h * MLA_NOPE:(h + 1) * MLA_NOPE])
        pieces.append(krt)
    kta_ref[...] = jnp.concatenate(pieces, axis=0).astype(BF)

    dk = zk[T_DK:T_END]
    ktc_ref[...] = jnp.concatenate(
        [_rope_cols(dk[j * DIFF_QK:(j + 1) * DIFF_QK], ct, st) for j in range(2 * DIFF_HEADS)],
        axis=0).astype(BF)


def _inproj(layer, x, mod_l, mod_row, tm, pw, tabs):
    B, L, _ = x.shape
    nt = L // tm
    tab_c, tab_s, msk, cos_t, sin_t = tabs
    row = (lambda b: b) if mod_row is None else (lambda b: mod_row)

    def wspec(shape):
        nd = len(shape)
        return pl.BlockSpec((None,) + shape, lambda i, b: (layer,) + (0,) * nd)

    tok = lambda w: pl.BlockSpec((None, tm, w), lambda i, b: (b, i, 0))
    tokt = lambda w: pl.BlockSpec((None, w, tm), lambda i, b: (b, 0, i))
    in_specs = [
        tok(D_MODEL),
        pl.BlockSpec((None, N_MOD, D_MODEL), lambda i, b: (row(b), 0, 0)),
        wspec((4, D_MODEL)),
        wspec((D_MODEL, R_END)),
        wspec((T_END, D_MODEL)),
        wspec((1, MLA_Q_RANK)),
        wspec((1, MLA_KV_RANK)),
        wspec((MLA_KV_RANK, 1)),
        wspec((MLA_Q_RANK, MLA_HEADS * MLA_QK)),
        wspec((MLA_HEADS * MLA_NOPE, MLA_KV_RANK)),
        wspec((MLA_KV_RANK, MLA_HEADS * MLA_V)),
        pl.BlockSpec((tm, P_END), lambda i, b: (i, 0)),
        pl.BlockSpec((tm, P_END), lambda i, b: (i, 0)),
        pl.BlockSpec((1, P_END), lambda i, b: (0, 0)),
        pl.BlockSpec((MLA_ROPE // 2, tm), lambda i, b: (0, i)),
        pl.BlockSpec((MLA_ROPE // 2, tm), lambda i, b: (0, i)),
    ]
    widths = [(MLA_HEADS * MLA_QK, BF), (256, BF), (256, BF), (MIX_D, F32),
              (256, BF), (128, BF), (256, BF), (128, BF)]
    out_shape = [jax.ShapeDtypeStruct((B, L, w), dt) for w, dt in widths]
    out_specs = [tok(w) for w, _ in widths]
    out_shape += [jax.ShapeDtypeStruct((B, MLA_HEADS * MLA_QK, L), BF),
                  jax.ShapeDtypeStruct((B, 2 * DIFF_HEADS * DIFF_QK, L), BF)]
    out_specs += [tokt(MLA_HEADS * MLA_QK), tokt(2 * DIFF_HEADS * DIFF_QK)]
    return pl.pallas_call(
        _inproj_kernel,
        out_shape=out_shape,
        grid=(nt, B),
        in_specs=in_specs,
        out_specs=out_specs,
        compiler_params=_params(("arbitrary", "arbitrary")),
        name="inproj",
    )(x, mod_l, pw["norm_g"], pw["w_r"], pw["w_kt"], pw["q_gain"], pw["kv_gain_r"], pw["kv_gain_c"],
      pw["w_uq"], pw["w_ukt"], pw["w_uv"], tab_c, tab_s, msk, cos_t, sin_t)


def _attn_heads(heads, sizes):
    chunks = [(seg, c) for seg, n in enumerate(sizes) for c in range(0, n, KEY_CHUNK)]
    outs = []
    prev = None
    for step in range(len(heads) + 1):
        cur = heads[step] if step < len(heads) else None
        cur_s, mc = [], None
        if prev is not None:
            prev_s, prev_m, prev_vs, prev_x = prev
            lacc, r = None, None
        for idx, (seg, c) in enumerate(chunks):
            if cur is not None:
                s = cur[0](seg, c)
                cur_s.append(s)
                mc = s if mc is None else jnp.maximum(mc, s)
            if prev is not None:
                e = jnp.exp2(prev_s[idx] - prev_m)
                rs = _dot(e.astype(BF), prev_vs[seg][c:c + KEY_CHUNK, :])
                lacc = e if lacc is None else lacc + e
                r = rs if r is None else r + rs
        if prev is not None:
            l = jnp.sum(lacc, axis=-1, keepdims=True)
            if prev_x is not None:
                l = l + jnp.exp2(prev_x - prev_m)
            outs.append(r / l)
        if cur is not None:
            m = jnp.max(mc, axis=-1, keepdims=True)
            if cur[2] is not None:
                m = jnp.maximum(m, cur[2])
            prev = (cur_s, m, cur[1], cur[2])
        else:
            prev = None
    return outs


def _kt_scores(qh, kt_refs, sl, seg, c):
    return _dot(qh, kt_refs[seg][sl, c:c + KEY_CHUNK])


def _mla_kernel(*refs, nseg):
    q_ref = refs[0]
    kt_refs = refs[1:1 + nseg]
    v_refs = refs[1 + nseg:1 + 2 * nseg]
    o_ref = refs[1 + 2 * nseg]
    q = q_ref[...]
    lane = lax.broadcasted_iota(jnp.int32, (1, LANES), 1)
    heads = []
    for h in range(MLA_HEADS):
        sl = slice(h * MLA_QK, (h + 1) * MLA_QK)
        vs = [v[:, (h // 2) * LANES:(h // 2 + 1) * LANES] for v in v_refs]
        heads.append((functools.partial(_kt_scores, q[:, sl], kt_refs, sl), vs, None))
    outs = _attn_heads(heads, [kt.shape[1] for kt in kt_refs])
    blocks = [jnp.where(lane < MLA_V, outs[2 * p], outs[2 * p + 1]) for p in range(MLA_HEADS // 2)]
    o_ref[...] = jnp.concatenate(blocks, axis=-1).astype(BF)


def _diff_kernel(*refs, nseg, lam_init):
    q_ref = refs[0]
    kt_refs = refs[1:1 + nseg]
    v_refs = refs[1 + nseg:1 + 2 * nseg]
    dl_ref, g_ref, o_ref = refs[1 + 2 * nseg:]
    q = q_ref[...]
    dl = dl_ref[...]
    lam = (jnp.exp(jnp.sum(dl[0:1] * dl[1:2], axis=-1, keepdims=True))
           - jnp.exp(jnp.sum(dl[2:3] * dl[3:4], axis=-1, keepdims=True)) + lam_init)
    lane = lax.broadcasted_iota(jnp.int32, (1, LANES), 1)
    lo = lane < DIFF_V
    heads = []
    for j in range(2 * DIFF_HEADS):
        sl = slice(j * DIFF_QK, (j + 1) * DIFF_QK)
        vs = [v[:, (j // 4) * LANES:(j // 4 + 1) * LANES] for v in v_refs]
        heads.append((functools.partial(_kt_scores, q[:, sl], kt_refs, sl), vs, None))
    comps = _attn_heads(heads, [kt.shape[1] for kt in kt_refs])
    blocks = []
    for p in range(DIFF_HEADS // 2):
        pair = [comps[2 * h] - lam * comps[2 * h + 1] for h in (2 * p, 2 * p + 1)]
        o = jnp.where(lo, pair[0], pair[1])
        o2 = o * o
        ms_lo = jnp.sum(jnp.where(lo, o2, 0.0), axis=-1, keepdims=True) * (1.0 / DIFF_V)
        ms_hi = jnp.sum(jnp.where(lo, 0.0, o2), axis=-1, keepdims=True) * (1.0 / DIFF_V)
        ms = jnp.where(lo, ms_lo, ms_hi)
        blocks.append(o * lax.rsqrt(ms + EPS) * g_ref[...] * (1.0 - lam_init))
    o_ref[...] = jnp.concatenate(blocks, axis=-1).astype(BF)


def _dense_attn(kind, layer, q, kts, vs, tq, extra_in=(), lam_init=0.0):
    B, L, wq = q.shape
    nseg = len(kts)
    in_specs = [pl.BlockSpec((None, tq, wq), lambda b, i: (b, i, 0))]
    for kt in kts:
        in_specs.append(pl.BlockSpec((None,) + kt.shape[1:], lambda b, i: (b, 0, 0)))
    for v in vs:
        in_specs.append(pl.BlockSpec((None,) + v.shape[1:], lambda b, i: (b, 0, 0)))
    if kind == "mla":
        body = functools.partial(_mla_kernel, nseg=nseg)
    else:
        body = functools.partial(_diff_kernel, nseg=nseg, lam_init=lam_init)
        dl, g = extra_in
        in_specs.append(pl.BlockSpec((None, 4, DIFF_QK), lambda b, i: (layer, 0, 0)))
        in_specs.append(pl.BlockSpec((None, 1, LANES), lambda b, i: (layer, 0, 0)))
    return pl.pallas_call(
        body,
        out_shape=jax.ShapeDtypeStruct((B, L, 256), BF),
        grid=(B, L // tq),
        in_specs=in_specs,
        out_specs=pl.BlockSpec((None, tq, 256), lambda b, i: (b, i, 0)),
        compiler_params=_params(("arbitrary", "arbitrary")),
        name=kind,
    )(q, *kts, *vs, *extra_in)


def _swa_kernel(*refs, layer, tq, seq_len, local):
    if local:
        sink_ref, q_ref, kl_ref, vl_ref, kc_ref, vc_ref, o_ref = refs
    else:
        sink_ref, q_ref, kc_ref, vc_ref, o_ref = refs
    i = pl.program_id(1)
    q = q_ref[...]
    lane = lax.broadcasted_iota(jnp.int32, (1, LANES), 1)
    rows = lax.broadcasted_iota(jnp.int32, (2 * tq, 1), 0)
    top = rows < tq
    if local:
        win = tq + 2 * SWA_WINDOW
        q0 = i * tq
        ws = pl.multiple_of(jnp.clip(q0 - SWA_WINDOW, 0, seq_len - win), LANES)
        kwin = kl_ref[pl.ds(ws, win), :]
        vwin = vl_ref[pl.ds(ws, win), :]
        qpos = q0 + jnp.where(top, rows, rows - tq)
        kpos = ws + lax.broadcasted_iota(jnp.int32, (1, win), 1)
        band = jnp.abs(kpos - qpos) <= SWA_WINDOW
    kc = kc_ref[...]
    vs = [vc_ref[...]] + ([vwin] if local else [])
    sizes = [kc.shape[0]] + ([win] if local else [])

    def scores(q2, sl, seg, c):
        if seg == 0:
            return _dot_nt(q2, kc[c:c + KEY_CHUNK, sl])
        return jnp.where(band[:, c:c + KEY_CHUNK], _dot_nt(q2, kwin[c:c + KEY_CHUNK, sl]), NEG)

    heads = []
    for k in range(SWA_KV_HEADS):
        q2 = jnp.concatenate([q[:, (2 * k) * SWA_HEAD_DIM:(2 * k + 1) * SWA_HEAD_DIM],
                              q[:, (2 * k + 1) * SWA_HEAD_DIM:(2 * k + 2) * SWA_HEAD_DIM]], axis=0)
        sl = slice(k * SWA_HEAD_DIM, (k + 1) * SWA_HEAD_DIM)
        sk = jnp.where(top, sink_ref[layer, 2 * k], sink_ref[layer, 2 * k + 1]) * LOG2E
        heads.append((functools.partial(scores, q2, sl), vs, sk))
    outs = _attn_heads(heads, sizes)
    blocks = []
    for k, r in enumerate(outs):
        r_top, r_bot = r[:tq], r[tq:]
        if k == 0:
            r_bot = pltpu.roll(r_bot, SWA_HEAD_DIM, axis=1)
        else:
            r_top = pltpu.roll(r_top, SWA_HEAD_DIM, axis=1)
        blocks.append(jnp.where(lane < SWA_HEAD_DIM, r_top, r_bot))
    o_ref[...] = jnp.concatenate(blocks, axis=-1).astype(BF)


def _swa(layer, sink, q, k_loc, v_loc, k_ctx, v_ctx, tq):
    B, L, _ = q.shape
    C = k_ctx.shape[1]
    local = k_loc is not None
    full = lambda n: pl.BlockSpec((None, n, 128), lambda b, i: (b, 0, 0))
    in_specs = [pl.BlockSpec(memory_space=pltpu.SMEM),
                pl.BlockSpec((None, tq, 256), lambda b, i: (b, i, 0))]
    args = [sink, q]
    if local:
        in_specs += [full(L), full(L)]
        args += [k_loc, v_loc]
    in_specs += [full(C), full(C)]
    args += [k_ctx, v_ctx]
    return pl.pallas_call(
        functools.partial(_swa_kernel, layer=layer, tq=tq, seq_len=L, local=local),
        out_shape=jax.ShapeDtypeStruct((B, L, 256), BF),
        grid=(B, L // tq),
        in_specs=in_specs,
        out_specs=pl.BlockSpec((None, tq, 256), lambda b, i: (b, i, 0)),
        compiler_params=_params(("arbitrary", "arbitrary")),
        name="swa",
    )(*args)


def _pool_diffs(u, prev, nxt, i, nt, tm, seq_len):
    n = tm + 2 * POOL_HALO
    ue = jnp.concatenate([jnp.where(i > 0, prev, 0.0), u, jnp.where(i < nt - 1, nxt, 0.0)], axis=0)

    def sh(a, k):
        return pltpu.roll(a, k % n, axis=0)

    s2 = ue + sh(ue, 1)
    s4 = sh(s2, 1) + sh(s2, -1)
    s8 = sh(s4, 2) + sh(s4, -2)
    s16 = sh(s8, 4) + sh(s8, -4)
    lane = lax.broadcasted_iota(jnp.int32, (1, MIX_D), 1)
    t = i * tm + lax.broadcasted_iota(jnp.int32, (tm, 1), 0)
    win = None
    cnt = None
    for gi, (w, s) in enumerate(zip(POOL_WINDOWS, (s2, s4, s8, s16))):
        half = w // 2
        c = (jnp.minimum(t - half + w, seq_len) - jnp.maximum(t - half, 0)).astype(F32)
        sv = s[POOL_HALO:POOL_HALO + tm]
        if win is None:
            win, cnt = sv, c
        else:
            sel = lane >= gi * POOL_GROUP
            win = jnp.where(sel, sv, win)
            cnt = jnp.where(sel, c, cnt)
    return win / cnt - u


def _outproj_kernel(oa_ref, ob_ref, oc_ref, u_ref, up_ref, un_ref, x_ref, mod_ref, ng_ref,
                    wo_ref, pw_ref, ps_ref, rw_ref, rb_ref,
                    xo_ref, hr_ref, rt_ref, cnt_ref, carry_ref, *, tm, seq_len, nt):
    i = pl.program_id(1)

    @pl.when((pl.program_id(0) == 0) & (i == 0))
    def _():
        carry_ref[...] = jnp.zeros_like(carry_ref)

    u = u_ref[...]
    dg = _pool_diffs(u, up_ref[...], un_ref[...], i, nt, tm, seq_len)
    od = _dot(dg.astype(BF), pw_ref[...]) * ps_ref[...]
    mix = jnp.concatenate([oa_ref[...], ob_ref[...], oc_ref[...], od.astype(BF)], axis=-1)
    ox = _dot(mix, wo_ref[...])
    mod = mod_ref[...]
    x = x_ref[...] + mod[2:3, :] * _rms(ox, ng_ref[1:2, :])
    xo_ref[...] = x
    h = _rms(x, ng_ref[2:3, :]) * (1.0 + mod[4:5, :]) + mod[3:4, :]
    hr_ref[...] = h.reshape((tm,) + TOK_TILE)
    hh = h.astype(BF)
    hl = (h - hh.astype(F32)).astype(BF)
    hw = _dot(hh, rw_ref[...])
    logits = hw[:, :LANES] + hw[:, LANES:] + _dot(hl, rw_ref[:, :LANES]) + rb_ref[...]

    lane = lax.broadcasted_iota(jnp.int32, (1, LANES), 1)
    lanef = lane.astype(F32)
    is_g = (lane >= N_EXPERTS) & (lane < N_EXPERTS + N_EXPERT_GROUPS)
    gl = jnp.where(is_g, logits, NEG)
    gmax = jnp.max(gl, axis=-1, keepdims=True)
    gsel = jnp.min(jnp.where(gl == gmax, lanef, 1e9), axis=-1, keepdims=True) - N_EXPERTS
    p_g = 1.0 / jnp.sum(jnp.where(is_g, jnp.exp(gl - gmax), 0.0), axis=-1, keepdims=True)
    grp = jnp.right_shift(lane, 2).astype(F32)
    el = jnp.where((lane < N_EXPERTS) & (grp == gsel), logits, NEG)
    v1 = jnp.max(el, axis=-1, keepdims=True)
    i1 = jnp.min(jnp.where(el == v1, lanef, 1e9), axis=-1, keepdims=True)
    el2 = jnp.where(lanef == i1, NEG, el)
    v2 = jnp.max(el2, axis=-1, keepdims=True)
    i2 = jnp.min(jnp.where(el2 == v2, lanef, 1e9), axis=-1, keepdims=True)
    e21 = jnp.exp(v2 - v1)
    w1 = p_g / (1.0 + e21)
    w2 = w1 * e21

    l1 = i1 - EXPERTS_PER_GROUP * gsel
    l2 = i2 - EXPERTS_PER_GROUP * gsel
    code = jnp.minimum(l1, l2) * EXPERTS_PER_GROUP + jnp.maximum(l1, l2)
    pidx = jnp.full_like(code, float(PAIRS_PER_GROUP - 1))
    slot_a = jnp.full_like(code, float(PAIR_A[-1]))
    for p in range(PAIRS_PER_GROUP - 1):
        lo_e, hi_e = min(PAIR_A[p], PAIR_B[p]), max(PAIR_A[p], PAIR_B[p])
        hit = code == float(lo_e * EXPERTS_PER_GROUP + hi_e)
        pidx = jnp.where(hit, float(p), pidx)
        slot_a = jnp.where(hit, float(PAIR_A[p]), slot_a)
    first_is_a = l1 == slot_a
    w_a = jnp.where(first_is_a, w1, w2)
    w_b = jnp.where(first_is_a, w2, w1)
    bucket = gsel * PAIRS_PER_GROUP + pidx

    onehot = lanef == bucket
    rr = lax.broadcasted_iota(jnp.int32, (tm, tm), 0)
    cc = lax.broadcasted_iota(jnp.int32, (tm, tm), 1)
    tri = jnp.where(rr >= cc, 1.0, 0.0).astype(BF)
    prefix = _dot(tri, jnp.where(onehot, 1.0, 0.0).astype(BF))
    carry = carry_ref[...]
    rank = jnp.sum(jnp.where(onehot, prefix - 1.0 + carry, 0.0), axis=-1, keepdims=True)
    carry = carry + prefix[tm - 1:tm, :]
    carry_ref[...] = carry
    cnt_ref[...] = carry
    rt_ref[...] = jnp.where(lane == 0, w_a, jnp.where(lane == 1, w_b, jnp.where(
        lane == 2, bucket, jnp.where(lane == 3, rank, 0.0))))


def _outproj(layer, oa, ob, oc, u, x, mod_l, mod_row, tm, pw):
    B, L, _ = x.shape
    nt = L // tm
    hb = tm // POOL_HALO
    nhb = L // POOL_HALO
    row = (lambda b: b) if mod_row is None else (lambda b: mod_row)

    def wspec(shape):
        nd = len(shape)
        return pl.BlockSpec((None,) + shape, lambda b, i: (layer,) + (0,) * nd)

    tok = lambda w: pl.BlockSpec((None, tm, w), lambda b, i: (b, i, 0))
    in_specs = [
        tok(256), tok(256), tok(256), tok(MIX_D),
        pl.BlockSpec((None, POOL_HALO, MIX_D), lambda b, i: (b, jnp.maximum(i * hb - 1, 0), 0)),
        pl.BlockSpec((None, POOL_HALO, MIX_D), lambda b, i: (b, jnp.minimum((i + 1) * hb, nhb - 1), 0)),
        tok(D_MODEL),
        pl.BlockSpec((None, N_MOD, D_MODEL), lambda b, i: (row(b), 0, 0)),
        wspec((4, D_MODEL)),
        wspec((D_MODEL, D_MODEL)),
        wspec((MIX_D, MIX_D)),
        wspec((1, MIX_D)),
        wspec((D_MODEL, 2 * LANES)),
        wspec((1, LANES)),
    ]
    return pl.pallas_call(
        functools.partial(_outproj_kernel, tm=tm, seq_len=L, nt=nt),
        out_shape=[jax.ShapeDtypeStruct((B, L, D_MODEL), F32),
                   jax.ShapeDtypeStruct((B * L,) + TOK_TILE, F32),
                   jax.ShapeDtypeStruct((B, L, LANES), F32),
                   jax.ShapeDtypeStruct((1, LANES), F32)],
        grid=(B, nt),
        in_specs=in_specs,
        out_specs=[tok(D_MODEL),
                   pl.BlockSpec((tm,) + TOK_TILE, lambda b, i: (b * nt + i, 0, 0)),
                   tok(LANES), pl.BlockSpec((1, LANES), lambda b, i: (0, 0))],
        scratch_shapes=[pltpu.VMEM((1, LANES), F32)],
        compiler_params=_params(("arbitrary", "arbitrary")),
        name="outproj",
    )(oa, ob, oc, u, u, u, x, mod_l, pw["norm_g"], pw["w_out"], pw["pool_bd"], pw["pool_scale"],
      pw["r_w"], pw["r_b"])


def _route_tables(route, counts, n_tiles):
    T = route.shape[0] * route.shape[1]
    bucket = route[:, :, 2].reshape(T).astype(jnp.int32)
    rank = route[:, :, 3].reshape(T).astype(jnp.int32)
    cnt = counts[0, :N_BUCKETS].astype(jnp.int32)
    tiles_b = (cnt + MOE_TM - 1) // MOE_TM
    ends = jnp.cumsum(tiles_b)
    starts = ends - tiles_b
    pos = jnp.take(starts * MOE_TM, bucket) + rank
    n_used = ends[-1]
    tile = jnp.arange(n_tiles, dtype=jnp.int32)
    src = jnp.minimum(tile, n_used - 1)
    tb = jnp.sum((src[:, None] >= ends[None, :]).astype(jnp.int32), axis=1)
    tb = jnp.minimum(tb, N_BUCKETS - 1)
    grp, pidx = tb // PAIRS_PER_GROUP, tb % PAIRS_PER_GROUP
    ea = grp * EXPERTS_PER_GROUP + jnp.take(jnp.array(PAIR_A, jnp.int32), pidx)
    eb = grp * EXPERTS_PER_GROUP + jnp.take(jnp.array(PAIR_B, jnp.int32), pidx)
    zlast = jnp.where(tiles_b > 0, (ends - 1) * MOE_TM, -1)
    idle = n_used + jnp.arange(N_BUCKETS, dtype=jnp.int32)
    zidle = jnp.where(idle < n_tiles, idle * MOE_TM, -1)
    pad_lo = jnp.concatenate([starts * MOE_TM + cnt, (n_used * MOE_TM)[None]])
    pad_hi = jnp.concatenate([ends * MOE_TM, jnp.full((1,), n_tiles * MOE_TM, jnp.int32)])
    zrow = jnp.concatenate([zlast, zidle, pad_lo, pad_hi]).astype(jnp.int32)
    return pos, zrow, src, ea, eb, n_used.reshape(1).astype(jnp.int32)


def _row_copy(src_ref, src_row, dst_ref, dst_row, sem):
    return pltpu.make_async_copy(src_ref.at[src_row], dst_ref.at[dst_row], sem)


def _scatter_kernel(pos_ref, zrow_ref, hr_ref, rt_ref, hs_ref, ws_ref, inv_ref, zbuf_ref, zw_ref, sem, zsem,
                    *, tm):
    i = pl.program_id(0)

    @pl.when(i == 0)
    def _():
        zbuf_ref[...] = jnp.zeros_like(zbuf_ref)
        zw_ref[...] = jnp.zeros_like(zw_ref)
        for start in (True, False):
            for b in range(2 * N_BUCKETS):
                @pl.when(zrow_ref[b] >= 0)
                def _():
                    z0 = pl.multiple_of(zrow_ref[b], MOE_TM)
                    for cp in (pltpu.make_async_copy(zbuf_ref, hs_ref.at[pl.ds(z0, MOE_TM)], zsem),
                               pltpu.make_async_copy(zw_ref, ws_ref.at[pl.ds(z0, MOE_TM)], zsem)):
                        cp.start() if start else cp.wait()

        def no_token(s, carry):
            inv_ref[s] = -1
            return carry

        for b in range(N_BUCKETS + 1):
            lax.fori_loop(zrow_ref[2 * N_BUCKETS + b], zrow_ref[3 * N_BUCKETS + 1 + b], no_token, 0)

    def copies(r):
        p = pos_ref[i * tm + r]
        return (_row_copy(hr_ref, r, hs_ref, p, sem),
                pltpu.make_async_copy(rt_ref.at[pl.ds(r, 1)], ws_ref.at[pl.ds(p, 1)], sem))

    def invert(r):
        inv_ref[pos_ref[i * tm + r]] = i * tm + r

    _row_dma_loop(tm, copies, invert)


def _row_dma_loop(n, copies, on_issue=None):
    def issue(g, carry):
        for k in range(DMA_UNROLL):
            for cp in copies(g * DMA_UNROLL + k):
                cp.start(priority=k % 2)
            if on_issue is not None:
                on_issue(g * DMA_UNROLL + k)
        return carry

    lax.fori_loop(0, n // DMA_UNROLL, issue, 0)

    def drain(g, carry):
        for k in range(DMA_UNROLL):
            for cp in copies(g * DMA_UNROLL + k):
                cp.wait()
        return carry

    lax.fori_loop(0, n // DMA_UNROLL, drain, 0)


def _scatter_rows(hr, route, pos, zrow, n_tiles, tm):
    T = hr.shape[0]
    return pl.pallas_call(
        functools.partial(_scatter_kernel, tm=tm),
        out_shape=[jax.ShapeDtypeStruct((n_tiles * MOE_TM,) + TOK_TILE, F32),
                   jax.ShapeDtypeStruct((n_tiles * MOE_TM, LANES), F32),
                   jax.ShapeDtypeStruct((n_tiles * MOE_TM,), jnp.int32)],
        grid_spec=pltpu.PrefetchScalarGridSpec(
            num_scalar_prefetch=2,
            grid=(T // tm,),
            in_specs=[pl.BlockSpec((tm,) + TOK_TILE, lambda i, pos, zrow: (i, 0, 0)),
                      pl.BlockSpec((tm, LANES), lambda i, pos, zrow: (i, 0))],
            out_specs=[pl.BlockSpec(memory_space=pl.ANY), pl.BlockSpec(memory_space=pl.ANY),
                       pl.BlockSpec(memory_space=pltpu.SMEM)],
            scratch_shapes=[pltpu.VMEM((MOE_TM,) + TOK_TILE, F32), pltpu.VMEM((MOE_TM, LANES), F32),
                            pltpu.SemaphoreType.DMA(()), pltpu.SemaphoreType.DMA(())]),
        compiler_params=_params(("arbitrary",)),
        name="moe_scatter",
    )(pos, zrow, hr, route)


def _tile_row_copies(j, inv_ref, obuf_ref, ys_ref, sem, n_tok, start):
    slot = j % 2
    for r in range(MOE_TM):
        tok = inv_ref[j * MOE_TM + r]
        dst = jnp.where(tok < 0, n_tok + slot * MOE_TM + r, tok)
        cp = _row_copy(obuf_ref.at[slot], r, ys_ref, dst, sem.at[slot])
        cp.start(priority=r % 2) if start else cp.wait()


def _moe_kernel(src_ref, ea_ref, eb_ref, nu_ref, inv_ref, hs_ref, ws_ref,
                wgua_ref, wgub_ref, wda_ref, wdb_ref, ys_ref, obuf_ref, sem, *, n_tok):
    j = pl.program_id(0)
    last = pl.num_programs(0) - 1
    copies = functools.partial(_tile_row_copies, inv_ref=inv_ref, obuf_ref=obuf_ref, ys_ref=ys_ref,
                               sem=sem, n_tok=n_tok)

    @pl.when(j == 0)
    def _():
        obuf_ref[...] = jnp.zeros_like(obuf_ref)
        for slot in range(2):
            cp = pltpu.make_async_copy(obuf_ref.at[slot], ys_ref.at[pl.ds(n_tok + slot * MOE_TM, MOE_TM)],
                                       sem.at[slot])
            cp.start()
            cp.wait()

    n_used = nu_ref[0]

    def experts():
        h = hs_ref[...].reshape(MOE_TM, D_MODEL).astype(BF)
        w = ws_ref[...]
        gua = _dot(h, wgua_ref[...])
        gub = _dot(h, wgub_ref[...])
        ga, gb = gua[:, :D_EXPERT], gub[:, :D_EXPERT]
        ya = _dot(((ga * jax.nn.sigmoid(ga)) * gua[:, D_EXPERT:]).astype(BF), wda_ref[...])
        yb = _dot(((gb * jax.nn.sigmoid(gb)) * gub[:, D_EXPERT:]).astype(BF), wdb_ref[...])
        obuf_ref[j % 2] = (w[:, 0:1] * ya + w[:, 1:2] * yb).reshape((MOE_TM,) + TOK_TILE)

    @pl.when((j >= 2) & (j - 2 < n_used))
    def _():
        copies(j - 2, start=False)

    @pl.when((j >= 1) & (j < n_used))
    def _():
        copies(j - 1, start=True)
        experts()

    @pl.when((j == 0) & (j < n_used))
    def _():
        experts()

    @pl.when((j >= 1) & (j == n_used))
    def _():
        copies(j - 1, start=True)

    @pl.when((j == last) & (j < n_used))
    def _():
        copies(j, start=True)

    @pl.when((j == last) & (j - 1 < n_used))
    def _():
        copies(j - 1, start=False)

    @pl.when((j == last) & (j < n_used))
    def _():
        copies(j, start=False)


def _moe_sorted(layer, n_tok, hs, ws, src, ea, eb, n_used, inv, pw):
    n_tiles = hs.shape[0] // MOE_TM
    assert n_tiles >= 2
    wgu = lambda sel: pl.BlockSpec((None, None, D_MODEL, 2 * D_EXPERT),
                                   lambda j, src, ea, eb, *_: (layer, (ea, eb)[sel][j], 0, 0))
    wd = lambda sel: pl.BlockSpec((None, None, D_EXPERT, D_MODEL),
                                  lambda j, src, ea, eb, *_: (layer, (ea, eb)[sel][j], 0, 0))
    return pl.pallas_call(
        functools.partial(_moe_kernel, n_tok=n_tok),
        out_shape=jax.ShapeDtypeStruct((n_tok + 2 * MOE_TM,) + TOK_TILE, F32),
        grid_spec=pltpu.PrefetchScalarGridSpec(
            num_scalar_prefetch=5,
            grid=(n_tiles,),
            in_specs=[pl.BlockSpec((MOE_TM,) + TOK_TILE, lambda j, src, *_: (src[j], 0, 0)),
                      pl.BlockSpec((MOE_TM, LANES), lambda j, src, *_: (src[j], 0)),
                      wgu(0), wgu(1), wd(0), wd(1)],
            out_specs=pl.BlockSpec(memory_space=pl.ANY),
            scratch_shapes=[pltpu.VMEM((2, MOE_TM) + TOK_TILE, F32), pltpu.SemaphoreType.DMA((2,))]),
        compiler_params=_params(("arbitrary",)),
        name="moe",
    )(src, ea, eb, n_used, inv, hs, ws, pw["w_gu"], pw["w_gu"], pw["w_down"], pw["w_down"])


def _residual_kernel(ys_ref, x_ref, mod_ref, ng_ref, o_ref, *, tm):
    fx = ys_ref[...].reshape(tm, D_MODEL)
    o_ref[...] = x_ref[...] + mod_ref[5:6, :] * _rms(fx, ng_ref[3:4, :])


def _moe_residual(layer, ys, x, mod_l, mod_row, tm, pw):
    B, L, _ = x.shape
    T = B * L
    tiles_per_seq = L // tm
    row = (lambda i: i // tiles_per_seq) if mod_row is None else (lambda i: mod_row)
    out = pl.pallas_call(
        functools.partial(_residual_kernel, tm=tm),
        out_shape=jax.ShapeDtypeStruct((T, D_MODEL), F32),
        grid=(T // tm,),
        in_specs=[pl.BlockSpec((tm,) + TOK_TILE, lambda i: (i, 0, 0)),
                  pl.BlockSpec((tm, D_MODEL), lambda i: (i, 0)),
                  pl.BlockSpec((None, N_MOD, D_MODEL), lambda i: (row(i), 0, 0)),
                  pl.BlockSpec((None, 4, D_MODEL), lambda i: (layer, 0, 0))],
        out_specs=pl.BlockSpec((tm, D_MODEL), lambda i: (i, 0)),
        compiler_params=_params(("arbitrary",)),
        name="moe_residual",
    )(ys, x.reshape(T, D_MODEL), mod_l, pw["norm_g"])
    return out.reshape(B, L, D_MODEL)


def _moe(layer, hr, route, counts, x, mod_l, mod_row, tm, pw):
    B, L, _ = x.shape
    T = B * L
    n_tiles = T // MOE_TM + N_BUCKETS
    pos, zrow, src, ea, eb, n_used = _route_tables(route, counts, n_tiles)
    hs, ws, inv = _scatter_rows(hr, route.reshape(T, LANES), pos, zrow, n_tiles, tm)
    ys = _moe_sorted(layer, T, hs, ws, src, ea, eb, n_used, inv, pw)
    return _moe_residual(layer, ys, x, mod_l, mod_row, tm, pw)


def _axial(length, dim):
    rows = length // GRID_W
    row = jnp.repeat(jnp.arange(rows), GRID_W).astype(F32)
    col = jnp.tile(jnp.arange(GRID_W), rows).astype(F32)
    n = dim // 4
    inv = ROPE_BASE ** (-jnp.arange(n, dtype=F32) / n)
    ang = jnp.concatenate([row[:, None] * inv, col[:, None] * inv], axis=-1)
    return jnp.cos(ang), jnp.sin(ang)


def _rope_tables(length, rotate):
    if rotate:
        cos_a, sin_a = _axial(length, MLA_ROPE)
        cos_b, sin_b = _axial(length, SWA_HEAD_DIM)
    else:
        cos_a, sin_a = jnp.ones((length, MLA_ROPE // 2), F32), jnp.zeros((length, MLA_ROPE // 2), F32)
        cos_b, sin_b = jnp.ones((length, SWA_HEAD_DIM // 2), F32), jnp.zeros((length, SWA_HEAD_DIM // 2), F32)
    one = jnp.ones((length, MLA_NOPE), F32)
    zero = jnp.zeros((length, MLA_NOPE), F32)
    sa, sb, sc = MLA_SCALE * LOG2E, SWA_SCALE * LOG2E, DIFF_SCALE * LOG2E
    c_qa = jnp.tile(jnp.concatenate([one, cos_a, cos_a], -1), (1, MLA_HEADS)) * sa
    s_qa = jnp.tile(jnp.concatenate([zero, -sin_a, sin_a], -1), (1, MLA_HEADS)) * sa
    c_b = jnp.concatenate([cos_b, cos_b], -1)
    s_b = jnp.concatenate([-sin_b, sin_b], -1)
    c_c = jnp.tile(jnp.concatenate([cos_a, cos_a], -1), (1, 2 * DIFF_HEADS)) * sc
    s_c = jnp.tile(jnp.concatenate([-sin_a, sin_a], -1), (1, 2 * DIFF_HEADS)) * sc
    tab_c = jnp.concatenate([c_qa, jnp.tile(c_b, (1, SWA_HEADS)) * sb, jnp.tile(c_b, (1, SWA_KV_HEADS)), c_c], -1)
    tab_s = jnp.concatenate([s_qa, jnp.tile(s_b, (1, SWA_HEADS)) * sb, jnp.tile(s_b, (1, SWA_KV_HEADS)), s_c], -1)
    m_qa = jnp.tile(jnp.concatenate([jnp.zeros((MLA_NOPE,)), jnp.ones((MLA_ROPE // 2,)),
                                     jnp.zeros((MLA_ROPE // 2,))]), MLA_HEADS)
    m_b = jnp.concatenate([jnp.ones((SWA_HEAD_DIM // 2,)), jnp.zeros((SWA_HEAD_DIM // 2,))])
    m_c = jnp.concatenate([jnp.ones((DIFF_QK // 2,)), jnp.zeros((DIFF_QK // 2,))])
    msk = jnp.concatenate([m_qa, jnp.tile(m_b, SWA_HEADS + SWA_KV_HEADS), jnp.tile(m_c, 2 * DIFF_HEADS)])
    return tab_c, tab_s, msk.astype(F32)[None, :], cos_a.T, sin_a.T


def _prep_weights(norm_g, w_in, mla_q_gain, mla_kv_gain, mla_w_uq, mla_w_ukv, diff_subln, pool_w,
                  pool_scale, w_out, router_gw, router_gb, router_ew, router_eb, exp_w_gu, exp_w_down):
    sizes = (MLA_Q_RANK, MLA_KV_RANK, MLA_ROPE, 256, 128, 128, 256, 256, 256, MIX_D)
    offs = [0]
    for s in sizes:
        offs.append(offs[-1] + s)
    cq, ckv, kr, bq, bk, bv, dq, dk, dv, u = [w_in[:, :, offs[j]:offs[j + 1]] for j in range(len(sizes))]
    w_r = jnp.concatenate([cq, bq, dq, dv, u, ckv, bk, bv], axis=-1).astype(BF)
    w_kt = jnp.swapaxes(jnp.concatenate([ckv, kr, dk], axis=-1), 1, 2).astype(BF)
    ukv = mla_w_ukv.reshape(DEPTH, MLA_KV_RANK, MLA_HEADS, MLA_NOPE + MLA_V)
    w_uk = ukv[..., :MLA_NOPE].reshape(DEPTH, MLA_KV_RANK, MLA_HEADS * MLA_NOPE)
    w_uv = ukv[..., MLA_NOPE:].reshape(DEPTH, MLA_KV_RANK, MLA_HEADS * MLA_V)
    eye = jnp.eye(len(POOL_WINDOWS), dtype=F32)
    pool_bd = jnp.einsum('lgcd,gh->lgchd', pool_w, eye).reshape(DEPTH, MIX_D, MIX_D)
    r_w = jnp.concatenate([router_ew, router_gw], axis=-1)
    r_w = jnp.pad(r_w, ((0, 0), (0, 0), (0, LANES - r_w.shape[-1])))
    r_hi = r_w.astype(BF)
    r_lo = (r_w - r_hi.astype(F32)).astype(BF)
    r_b = jnp.concatenate([router_eb, router_gb], axis=-1)
    r_b = jnp.pad(r_b, ((0, 0), (0, LANES - r_b.shape[-1])))[:, None, :]
    return dict(
        norm_g=norm_g, w_r=w_r, w_kt=w_kt,
        q_gain=mla_q_gain[:, None, :], kv_gain_r=mla_kv_gain[:, None, :], kv_gain_c=mla_kv_gain[:, :, None],
        w_uq=mla_w_uq.astype(BF), w_ukt=jnp.swapaxes(w_uk, 1, 2).astype(BF), w_uv=w_uv.astype(BF),
        subln=jnp.tile(diff_subln, (1, LANES // DIFF_V))[:, None, :],
        pool_bd=pool_bd.astype(BF), pool_scale=pool_scale[:, None, :], w_out=w_out.astype(BF),
        r_w=jnp.concatenate([r_hi, r_lo], axis=-1), r_b=r_b, w_gu=exp_w_gu.astype(BF), w_down=exp_w_down.astype(BF))


def kernel(x, c, ctx, c_ctx, ada_w, ada_b, norm_g, w_in, mla_q_gain, mla_kv_gain, mla_w_uq, mla_w_ukv,
           swa_sink, diff_lambda, diff_subln, pool_w, pool_scale, w_out, router_gw, router_gb,
           router_ew, router_eb, exp_w_gu, exp_w_down):
    B, L, D = x.shape
    C = ctx.shape[1]
    assert D == D_MODEL and B + 1 <= MOD_ROWS and L % 512 == 0 and C % 256 == 0
    pw = _prep_weights(norm_g, w_in, mla_q_gain, mla_kv_gain, mla_w_uq, mla_w_ukv, diff_subln, pool_w,
                       pool_scale, w_out, router_gw, router_gb, router_ew, router_eb, exp_w_gu, exp_w_down)
    cc = jnp.concatenate([c, c_ctx[None, :], jnp.zeros((MOD_ROWS - B - 1, D), F32)], axis=0)
    mod = jnp.swapaxes(_adaln(cc, ada_w, ada_b), 1, 2)
    tabs_x = _rope_tables(L, True)
    tabs_c = _rope_tables(C, False)
    tm_x, tm_c, tq, tq_x = 512, 256, 256, 512

    h_ctx = ctx
    for layer in range(DEPTH):
        update_ctx = layer < DEPTH - 1
        lam_init = 0.8 - 0.6 * math.exp(-0.3 * layer)
        mod_l = mod[layer]
        qa, qb, qc, u, va, vb, vc, kb, kta, ktc = _inproj(layer, x, mod_l, None, tm_x, pw, tabs_x)
        cqa, cqb, cqc, cu, cva, cvb, cvc, ckb, ckta, cktc = _inproj(layer, h_ctx, mod_l, B, tm_c, pw, tabs_c)
        diff_extra = (diff_lambda, pw["subln"])

        oa = _dense_attn("mla", layer, qa, [kta, ckta], [va, cva], tq_x)
        ob = _swa(layer, swa_sink, qb, kb, vb, ckb, cvb, tq)
        oc = _dense_attn("diff", layer, qc, [ktc, cktc], [vc, cvc], tq_x, diff_extra, lam_init)
        x, hrx, rtx, cntx = _outproj(layer, oa, ob, oc, u, x, mod_l, None, tm_x, pw)
        if update_ctx:
            coa = _dense_attn("mla", layer, cqa, [ckta], [cva], tq)
            cob = _swa(layer, swa_sink, cqb, None, None, ckb, cvb, tq)
            coc = _dense_attn("diff", layer, cqc, [cktc], [cvc], tq, diff_extra, lam_init)
            h_ctx, hrc, rtc, cntc = _outproj(layer, coa, cob, coc, cu, h_ctx, mod_l, B, tm_c, pw)
        x = _moe(layer, hrx, rtx, cntx, x, mod_l, None, tm_x, pw)
        if update_ctx:
            h_ctx = _moe(layer, hrc, rtc, cntc, h_ctx, mod_l, B, tm_c, pw)
    return x
```

```python
import functools
import math

import jax
import jax.numpy as jnp
from jax import lax
from jax.experimental import pallas as pl
from jax.experimental.pallas import tpu as pltpu

F32 = jnp.float32
BF = jnp.bfloat16

D_MODEL = 1024
DEPTH = 2
GRID_W = 64
EPS = 1e-6
ROPE_BASE = 10000.0
N_MOD = 6

MLA_HEADS = 4
MLA_Q_RANK = 256
MLA_KV_RANK = 128
MLA_NOPE = 64
MLA_ROPE = 32
MLA_V = 64
MLA_QK = MLA_NOPE + MLA_ROPE
MLA_SCALE = MLA_QK ** -0.5

SWA_HEADS = 4
SWA_KV_HEADS = 2
SWA_HEAD_DIM = 64
SWA_WINDOW = 128
SWA_SCALE = SWA_HEAD_DIM ** -0.5

DIFF_HEADS = 4
DIFF_QK = 32
DIFF_V = 64
DIFF_SCALE = DIFF_QK ** -0.5

POOL_WINDOWS = (2, 4, 8, 16)
POOL_GROUP = 64
POOL_HALO = 8
MIX_D = len(POOL_WINDOWS) * POOL_GROUP

N_EXPERT_GROUPS = 4
EXPERTS_PER_GROUP = 4
N_EXPERTS = 16
D_EXPERT = 512
PAIR_A = (0, 2, 2, 3, 3, 3)
PAIR_B = (1, 1, 0, 0, 1, 2)
PAIRS_PER_GROUP = len(PAIR_A)
N_BUCKETS = N_EXPERT_GROUPS * PAIRS_PER_GROUP

LOG2E = math.log2(math.e)
NEG = -1e30
LANES = 128
MOD_ROWS = 24
SUBLANES = 8
TOK_TILE = (D_MODEL // LANES, LANES)
MOE_TM = 256
DMA_UNROLL = 8
KEY_CHUNK = 256
VMEM_LIMIT = 48 * 1024 * 1024

R_CQ, R_BQ, R_DQ, R_DV, R_U, R_CKV, R_BK, R_BV, R_END = 0, 256, 512, 768, 1024, 1280, 1408, 1536, 1664
T_CKV, T_KR, T_DK, T_END = 0, 128, 160, 416
P_QA, P_QB, P_KB, P_QC, P_END = 0, 384, 640, 768, 1024


def _dot(a, b):
    return jnp.dot(a, b, preferred_element_type=F32)


def _dot_nt(a, b):
    return lax.dot_general(a, b, (((1,), (1,)), ((), ())), preferred_element_type=F32)


def _rms(x, g):
    return x * lax.rsqrt(jnp.mean(x * x, axis=-1, keepdims=True) + EPS) * g


def _params(sem):
    return pltpu.CompilerParams(dimension_semantics=sem, vmem_limit_bytes=VMEM_LIMIT)


def _adaln_kernel(c_ref, w_ref, b_ref, o_ref):
    c = c_ref[...]
    s = (c * jax.nn.sigmoid(c)).astype(BF)
    o_ref[...] = _dot(s, w_ref[...].astype(BF)) + b_ref[...]


def _adaln(cc, ada_w, ada_b):
    return pl.pallas_call(
        _adaln_kernel,
        out_shape=jax.ShapeDtypeStruct((DEPTH, N_MOD, MOD_ROWS, D_MODEL), F32),
        grid=(DEPTH, N_MOD),
        in_specs=[
            pl.BlockSpec((MOD_ROWS, D_MODEL), lambda l, j: (0, 0)),
            pl.BlockSpec((None, D_MODEL, D_MODEL), lambda l, j: (l, 0, j)),
            pl.BlockSpec((None, None, 1, D_MODEL), lambda l, j: (l, j, 0, 0)),
        ],
        out_specs=pl.BlockSpec((None, None, MOD_ROWS, D_MODEL), lambda l, j: (l, j, 0, 0)),
        compiler_params=_params(("arbitrary", "arbitrary")),
        name="adaln",
    )(cc, ada_w, ada_b.reshape(DEPTH, N_MOD, 1, D_MODEL))


def _rope_rows(xf, c, s, m1, half):
    w = xf.shape[-1]
    up = pltpu.roll(xf, w - half, axis=1)
    dn = pltpu.roll(xf, half, axis=1)
    return xf * c + jnp.where(m1 > 0.5, up, dn) * s


def _rope_cols(xt, c, s):
    h = xt.shape[0] // 2
    x1, x2 = xt[:h], xt[h:]
    return jnp.concatenate([x1 * c - x2 * s, x1 * s + x2 * c], axis=0)


def _inproj_kernel(x_ref, mod_ref, ng_ref, wr_ref, wkt_ref, qg_ref, kvgr_ref, kvgc_ref,
                   wuq_ref, wukt_ref, wuv_ref, tc_ref, ts_ref, m_ref, ct_ref, st_ref,
                   qa_ref, qb_ref, qc_ref, u_ref, va_ref, vb_ref, vc_ref, kb_ref, kta_ref, ktc_ref):
    x = x_ref[...]
    mod = mod_ref[...]
    hx = _rms(x, ng_ref[0:1, :]) * (1.0 + mod[1:2, :]) + mod[0:1, :]
    hb = hx.astype(BF)
    zr = _dot(hb, wr_ref[...])
    zk = _dot_nt(wkt_ref[...], hb)

    tc = tc_ref[...]
    ts = ts_ref[...]
    m1 = m_ref[...]

    cqn = _rms(zr[:, R_CQ:R_CQ + MLA_Q_RANK], qg_ref[...]).astype(BF)
    qa = _dot(cqn, wuq_ref[...])
    qa_ref[...] = _rope_rows(qa, tc[:, P_QA:P_QB], ts[:, P_QA:P_QB], m1[:, P_QA:P_QB],
                             MLA_ROPE // 2).astype(BF)
    qb_ref[...] = _rope_rows(zr[:, R_BQ:R_DQ], tc[:, P_QB:P_KB], ts[:, P_QB:P_KB], m1[:, P_QB:P_KB],
                             SWA_HEAD_DIM // 2).astype(BF)
    qc_ref[...] = _rope_rows(zr[:, R_DQ:R_DV], tc[:, P_QC:P_END], ts[:, P_QC:P_END], m1[:, P_QC:P_END],
                             DIFF_QK // 2).astype(BF)
    kb_ref[...] = _rope_rows(zr[:, R_BK:R_BV], tc[:, P_KB:P_QC], ts[:, P_KB:P_QC], m1[:, P_KB:P_QC],
                             SWA_HEAD_DIM // 2).astype(BF)
    u_ref[...] = zr[:, R_U:R_CKV]
    vb_ref[...] = zr[:, R_BV:R_END].astype(BF)
    vc_ref[...] = zr[:, R_DV:R_U].astype(BF)

    ckvn = _rms(zr[:, R_CKV:R_BK], kvgr_ref[...]).astype(BF)
    va_ref[...] = _dot(ckvn, wuv_ref[...]).astype(BF)

    ckvt = zk[T_CKV:T_KR]
    ckvtn = ckvt * lax.rsqrt(jnp.mean(ckvt * ckvt, axis=0, keepdims=True) + EPS) * kvgc_ref[...]
    knt = _dot(wukt_ref[...], ckvtn.astype(BF))
    ct = ct_ref[...]
    st = st_ref[...]
    krt = _rope_cols(zk[T_KR:T_DK], ct, st)
    pieces = []
    for h in range(MLA_HEADS):
        pieces.append(knt[h * MLA_NOPE:(h + 1) * MLA_NOPE])
        pieces.append(krt)
    kta_ref[...] = jnp.concatenate(pieces, axis=0).astype(BF)

    dk = zk[T_DK:T_END]
    ktc_ref[...] = jnp.concatenate(
        [_rope_cols(dk[j * DIFF_QK:(j + 1) * DIFF_QK], ct, st) for j in range(2 * DIFF_HEADS)],
        axis=0).astype(BF)


def _inproj(layer, x, mod_l, mod_row, tm, pw, tabs):
    B, L, _ = x.shape
    nt = L // tm
    tab_c, tab_s, msk, cos_t, sin_t = tabs
    row = (lambda b: b) if mod_row is None else (lambda b: mod_row)

    def wspec(shape):
        nd = len(shape)
        return pl.BlockSpec((None,) + shape, lambda i, b: (layer,) + (0,) * nd)

    tok = lambda w: pl.BlockSpec((None, tm, w), lambda i, b: (b, i, 0))
    tokt = lambda w: pl.BlockSpec((None, w, tm), lambda i, b: (b, 0, i))
    in_specs = [
        tok(D_MODEL),
        pl.BlockSpec((None, N_MOD, D_MODEL), lambda i, b: (row(b), 0, 0)),
        wspec((4, D_MODEL)),
        wspec((D_MODEL, R_END)),
        wspec((T_END, D_MODEL)),
        wspec((1, MLA_Q_RANK)),
        wspec((1, MLA_KV_RANK)),
        wspec((MLA_KV_RANK, 1)),
        wspec((MLA_Q_RANK, MLA_HEADS * MLA_QK)),
        wspec((MLA_HEADS * MLA_NOPE, MLA_KV_RANK)),
        wspec((MLA_KV_RANK, MLA_HEADS * MLA_V)),
        pl.BlockSpec((tm, P_END), lambda i, b: (i, 0)),
        pl.BlockSpec((tm, P_END), lambda i, b: (i, 0)),
        pl.BlockSpec((1, P_END), lambda i, b: (0, 0)),
        pl.BlockSpec((MLA_ROPE // 2, tm), lambda i, b: (0, i)),
        pl.BlockSpec((MLA_ROPE // 2, tm), lambda i, b: (0, i)),
    ]
    widths = [(MLA_HEADS * MLA_QK, BF), (256, BF), (256, BF), (MIX_D, F32),
              (256, BF), (128, BF), (256, BF), (128, BF)]
    out_shape = [jax.ShapeDtypeStruct((B, L, w), dt) for w, dt in widths]
    out_specs = [tok(w) for w, _ in widths]
    out_shape += [jax.ShapeDtypeStruct((B, MLA_HEADS * MLA_QK, L), BF),
                  jax.ShapeDtypeStruct((B, 2 * DIFF_HEADS * DIFF_QK, L), BF)]
    out_specs += [tokt(MLA_HEADS * MLA_QK), tokt(2 * DIFF_HEADS * DIFF_QK)]
    return pl.pallas_call(
        _inproj_kernel,
        out_shape=out_shape,
        grid=(nt, B),
        in_specs=in_specs,
        out_specs=out_specs,
        compiler_params=_params(("arbitrary", "arbitrary")),
        name="inproj",
    )(x, mod_l, pw["norm_g"], pw["w_r"], pw["w_kt"], pw["q_gain"], pw["kv_gain_r"], pw["kv_gain_c"],
      pw["w_uq"], pw["w_ukt"], pw["w_uv"], tab_c, tab_s, msk, cos_t, sin_t)


def _attn_heads(heads, sizes):
    chunks = [(seg, c) for seg, n in enumerate(sizes) for c in range(0, n, KEY_CHUNK)]
    outs = []
    prev = None
    for step in range(len(heads) + 1):
        cur = heads[step] if step < len(heads) else None
        cur_s, mc = [], None
        if prev is not None:
            prev_s, prev_m, prev_vs, prev_x = prev
            lacc, r = None, None
        for idx, (seg, c) in enumerate(chunks):
            if cur is not None:
                s = cur[0](seg, c)
                cur_s.append(s)
                mc = s if mc is None else jnp.maximum(mc, s)
            if prev is not None:
                e = jnp.exp2(prev_s[idx] - prev_m)
                rs = _dot(e.astype(BF), prev_vs[seg][c:c + KEY_CHUNK, :])
                lacc = e if lacc is None else lacc + e
                r = rs if r is None else r + rs
        if prev is not None:
            l = jnp.sum(lacc, axis=-1, keepdims=True)
            if prev_x is not None:
                l = l + jnp.exp2(prev_x - prev_m)
            outs.append(r / l)
        if cur is not None:
            m = jnp.max(mc, axis=-1, keepdims=True)
            if cur[2] is not None:
                m = jnp.maximum(m, cur[2])
            prev = (cur_s, m, cur[1], cur[2])
        else:
            prev = None
    return outs


def _kt_scores(qh, kt_refs, sl, seg, c):
    return _dot(qh, kt_refs[seg][sl, c:c + KEY_CHUNK])


def _mla_kernel(*refs, nseg):
    q_ref = refs[0]
    kt_refs = refs[1:1 + nseg]
    v_refs = refs[1 + nseg:1 + 2 * nseg]
    o_ref = refs[1 + 2 * nseg]
    q = q_ref[...]
    lane = lax.broadcasted_iota(jnp.int32, (1, LANES), 1)
    heads = []
    for h in range(MLA_HEADS):
        sl = slice(h * MLA_QK, (h + 1) * MLA_QK)
        vs = [v[:, (h // 2) * LANES:(h // 2 + 1) * LANES] for v in v_refs]
        heads.append((functools.partial(_kt_scores, q[:, sl], kt_refs, sl), vs, None))
    outs = _attn_heads(heads, [kt.shape[1] for kt in kt_refs])
    blocks = [jnp.where(lane < MLA_V, outs[2 * p], outs[2 * p + 1]) for p in range(MLA_HEADS // 2)]
    o_ref[...] = jnp.concatenate(blocks, axis=-1).astype(BF)


def _diff_kernel(*refs, nseg, lam_init):
    q_ref = refs[0]
    kt_refs = refs[1:1 + nseg]
    v_refs = refs[1 + nseg:1 + 2 * nseg]
    dl_ref, g_ref, o_ref = refs[1 + 2 * nseg:]
    q = q_ref[...]
    dl = dl_ref[...]
    lam = (jnp.exp(jnp.sum(dl[0:1] * dl[1:2], axis=-1, keepdims=True))
           - jnp.exp(jnp.sum(dl[2:3] * dl[3:4], axis=-1, keepdims=True)) + lam_init)
    lane = lax.broadcasted_iota(jnp.int32, (1, LANES), 1)
    lo = lane < DIFF_V
    heads = []
    for j in range(2 * DIFF_HEADS):
        sl = slice(j * DIFF_QK, (j + 1) * DIFF_QK)
        vs = [v[:, (j // 4) * LANES:(j // 4 + 1) * LANES] for v in v_refs]
        heads.append((functools.partial(_kt_scores, q[:, sl], kt_refs, sl), vs, None))
    comps = _attn_heads(heads, [kt.shape[1] for kt in kt_refs])
    blocks = []
    for p in range(DIFF_HEADS // 2):
        pair = [comps[2 * h] - lam * comps[2 * h + 1] for h in (2 * p, 2 * p + 1)]
        o = jnp.where(lo, pair[0], pair[1])
        o2 = o * o
        ms_lo = jnp.sum(jnp.where(lo, o2, 0.0), axis=-1, keepdims=True) * (1.0 / DIFF_V)
        ms_hi = jnp.sum(jnp.where(lo, 0.0, o2), axis=-1, keepdims=True) * (1.0 / DIFF_V)
        ms = jnp.where(lo, ms_lo, ms_hi)
        blocks.append(o * lax.rsqrt(ms + EPS) * g_ref[...] * (1.0 - lam_init))
    o_ref[...] = jnp.concatenate(blocks, axis=-1).astype(BF)


def _dense_attn(kind, layer, q, kts, vs, tq, extra_in=(), lam_init=0.0):
    B, L, wq = q.shape
    nseg = len(kts)
    in_specs = [pl.BlockSpec((None, tq, wq), lambda b, i: (b, i, 0))]
    for kt in kts:
        in_specs.append(pl.BlockSpec((None,) + kt.shape[1:], lambda b, i: (b, 0, 0)))
    for v in vs:
        in_specs.append(pl.BlockSpec((None,) + v.shape[1:], lambda b, i: (b, 0, 0)))
    if kind == "mla":
        body = functools.partial(_mla_kernel, nseg=nseg)
    else:
        body = functools.partial(_diff_kernel, nseg=nseg, lam_init=lam_init)
        dl, g = extra_in
        in_specs.append(pl.BlockSpec((None, 4, DIFF_QK), lambda b, i: (layer, 0, 0)))
        in_specs.append(pl.BlockSpec((None, 1, LANES), lambda b, i: (layer, 0, 0)))
    return pl.pallas_call(
        body,
        out_shape=jax.ShapeDtypeStruct((B, L, 256), BF),
        grid=(B, L // tq),
        in_specs=in_specs,
        out_specs=pl.BlockSpec((None, tq, 256), lambda b, i: (b, i, 0)),
        compiler_params=_params(("arbitrary", "arbitrary")),
        name=kind,
    )(q, *kts, *vs, *extra_in)


def _swa_kernel(*refs, layer, tq, seq_len, local):
    if local:
        sink_ref, q_ref, kl_ref, vl_ref, kc_ref, vc_ref, o_ref = refs
    else:
        sink_ref, q_ref, kc_ref, vc_ref, o_ref = refs
    i = pl.program_id(1)
    q = q_ref[...]
    lane = lax.broadcasted_iota(jnp.int32, (1, LANES), 1)
    rows = lax.broadcasted_iota(jnp.int32, (2 * tq, 1), 0)
    top = rows < tq
    if local:
        win = tq + 2 * SWA_WINDOW
        q0 = i * tq
        ws = pl.multiple_of(jnp.clip(q0 - SWA_WINDOW, 0, seq_len - win), LANES)
        kwin = kl_ref[pl.ds(ws, win), :]
        vwin = vl_ref[pl.ds(ws, win), :]
        qpos = q0 + jnp.where(top, rows, rows - tq)
        kpos = ws + lax.broadcasted_iota(jnp.int32, (1, win), 1)
        band = jnp.abs(kpos - qpos) <= SWA_WINDOW
    kc = kc_ref[...]
    vs = [vc_ref[...]] + ([vwin] if local else [])
    sizes = [kc.shape[0]] + ([win] if local else [])

    def scores(q2, sl, seg, c):
        if seg == 0:
            return _dot_nt(q2, kc[c:c + KEY_CHUNK, sl])
        return jnp.where(band[:, c:c + KEY_CHUNK], _dot_nt(q2, kwin[c:c + KEY_CHUNK, sl]), NEG)

    heads = []
    for k in range(SWA_KV_HEADS):
        q2 = jnp.concatenate([q[:, (2 * k) * SWA_HEAD_DIM:(2 * k + 1) * SWA_HEAD_DIM],
                              q[:, (2 * k + 1) * SWA_HEAD_DIM:(2 * k + 2) * SWA_HEAD_DIM]], axis=0)
        sl = slice(k * SWA_HEAD_DIM, (k + 1) * SWA_HEAD_DIM)
        sk = jnp.where(top, sink_ref[layer, 2 * k], sink_ref[layer, 2 * k + 1]) * LOG2E
        heads.append((functools.partial(scores, q2, sl), vs, sk))
    outs = _attn_heads(heads, sizes)
    blocks = []
    for k, r in enumerate(outs):
        r_top, r_bot = r[:tq], r[tq:]
        if k == 0:
            r_bot = pltpu.roll(r_bot, SWA_HEAD_DIM, axis=1)
        else:
            r_top = pltpu.roll(r_top, SWA_HEAD_DIM, axis=1)
        blocks.append(jnp.where(lane < SWA_HEAD_DIM, r_top, r_bot))
    o_ref[...] = jnp.concatenate(blocks, axis=-1).astype(BF)


def _swa(layer, sink, q, k_loc, v_loc, k_ctx, v_ctx, tq):
    B, L, _ = q.shape
    C = k_ctx.shape[1]
    local = k_loc is not None
    full = lambda n: pl.BlockSpec((None, n, 128), lambda b, i: (b, 0, 0))
    in_specs = [pl.BlockSpec(memory_space=pltpu.SMEM),
                pl.BlockSpec((None, tq, 256), lambda b, i: (b, i, 0))]
    args = [sink, q]
    if local:
        in_specs += [full(L), full(L)]
        args += [k_loc, v_loc]
    in_specs += [full(C), full(C)]
    args += [k_ctx, v_ctx]
    return pl.pallas_call(
        functools.partial(_swa_kernel, layer=layer, tq=tq, seq_len=L, local=local),
        out_shape=jax.ShapeDtypeStruct((B, L, 256), BF),
        grid=(B, L // tq),
        in_specs=in_specs,
        out_specs=pl.BlockSpec((None, tq, 256), lambda b, i: (b, i, 0)),
        compiler_params=_params(("arbitrary", "arbitrary")),
        name="swa",
    )(*args)


def _pool_diffs(u, prev, nxt, i, nt, tm, seq_len):
    n = tm + 2 * POOL_HALO
    ue = jnp.concatenate([jnp.where(i > 0, prev, 0.0), u, jnp.where(i < nt - 1, nxt, 0.0)], axis=0)

    def sh(a, k):
        return pltpu.roll(a, k % n, axis=0)

    s2 = ue + sh(ue, 1)
    s4 = sh(s2, 1) + sh(s2, -1)
    s8 = sh(s4, 2) + sh(s4, -2)
    s16 = sh(s8, 4) + sh(s8, -4)
    lane = lax.broadcasted_iota(jnp.int32, (1, MIX_D), 1)
    t = i * tm + lax.broadcasted_iota(jnp.int32, (tm, 1), 0)
    win = None
    cnt = None
    for gi, (w, s) in enumerate(zip(POOL_WINDOWS, (s2, s4, s8, s16))):
        half = w // 2
        c = (jnp.minimum(t - half + w, seq_len) - jnp.maximum(t - half, 0)).astype(F32)
        sv = s[POOL_HALO:POOL_HALO + tm]
        if win is None:
            win, cnt = sv, c
        else:
            sel = lane >= gi * POOL_GROUP
            win = jnp.where(sel, sv, win)
            cnt = jnp.where(sel, c, cnt)
    return win / cnt - u


def _outproj_kernel(oa_ref, ob_ref, oc_ref, u_ref, up_ref, un_ref, x_ref, mod_ref, ng_ref,
                    wo_ref, pw_ref, ps_ref, rw_ref, rb_ref,
                    cin_ref, xo_ref, hr_ref, rt_ref, cnt_ref, carry_ref, *, tm, seq_len, nt):
    i = pl.program_id(1)

    @pl.when((pl.program_id(0) == 0) & (i == 0))
    def _():
        carry_ref[...] = cin_ref[...]

    u = u_ref[...]
    dg = _pool_diffs(u, up_ref[...], un_ref[...], i, nt, tm, seq_len)
    od = _dot(dg.astype(BF), pw_ref[...]) * ps_ref[...]
    mix = jnp.concatenate([oa_ref[...], ob_ref[...], oc_ref[...], od.astype(BF)], axis=-1)
    ox = _dot(mix, wo_ref[...])
    mod = mod_ref[...]
    x = x_ref[...] + mod[2:3, :] * _rms(ox, ng_ref[1:2, :])
    xo_ref[...] = x
    h = _rms(x, ng_ref[2:3, :]) * (1.0 + mod[4:5, :]) + mod[3:4, :]
    hr_ref[...] = h.reshape((tm,) + TOK_TILE)
    hh = h.astype(BF)
    hl = (h - hh.astype(F32)).astype(BF)
    hw = _dot(hh, rw_ref[...])
    logits = hw[:, :LANES] + hw[:, LANES:] + _dot(hl, rw_ref[:, :LANES]) + rb_ref[...]

    lane = lax.broadcasted_iota(jnp.int32, (1, LANES), 1)
    lanef = lane.astype(F32)
    is_g = (lane >= N_EXPERTS) & (lane < N_EXPERTS + N_EXPERT_GROUPS)
    gl = jnp.where(is_g, logits, NEG)
    gmax = jnp.max(gl, axis=-1, keepdims=True)
    gsel = jnp.min(jnp.where(gl == gmax, lanef, 1e9), axis=-1, keepdims=True) - N_EXPERTS
    p_g = 1.0 / jnp.sum(jnp.where(is_g, jnp.exp(gl - gmax), 0.0), axis=-1, keepdims=True)
    grp = jnp.right_shift(lane, 2).astype(F32)
    el = jnp.where((lane < N_EXPERTS) & (grp == gsel), logits, NEG)
    v1 = jnp.max(el, axis=-1, keepdims=True)
    i1 = jnp.min(jnp.where(el == v1, lanef, 1e9), axis=-1, keepdims=True)
    el2 = jnp.where(lanef == i1, NEG, el)
    v2 = jnp.max(el2, axis=-1, keepdims=True)
    i2 = jnp.min(jnp.where(el2 == v2, lanef, 1e9), axis=-1, keepdims=True)
    e21 = jnp.exp(v2 - v1)
    w1 = p_g / (1.0 + e21)
    w2 = w1 * e21

    l1 = i1 - EXPERTS_PER_GROUP * gsel
    l2 = i2 - EXPERTS_PER_GROUP * gsel
    code = jnp.minimum(l1, l2) * EXPERTS_PER_GROUP + jnp.maximum(l1, l2)
    pidx = jnp.full_like(code, float(PAIRS_PER_GROUP - 1))
    slot_a = jnp.full_like(code, float(PAIR_A[-1]))
    for p in range(PAIRS_PER_GROUP - 1):
        lo_e, hi_e = min(PAIR_A[p], PAIR_B[p]), max(PAIR_A[p], PAIR_B[p])
        hit = code == float(lo_e * EXPERTS_PER_GROUP + hi_e)
        pidx = jnp.where(hit, float(p), pidx)
        slot_a = jnp.where(hit, float(PAIR_A[p]), slot_a)
    first_is_a = l1 == slot_a
    w_a = jnp.where(first_is_a, w1, w2)
    w_b = jnp.where(first_is_a, w2, w1)
    bucket = gsel * PAIRS_PER_GROUP + pidx

    onehot = lanef == bucket
    rr = lax.broadcasted_iota(jnp.int32, (tm, tm), 0)
    cc = lax.broadcasted_iota(jnp.int32, (tm, tm), 1)
    tri = jnp.where(rr >= cc, 1.0, 0.0).astype(BF)
    prefix = _dot(tri, jnp.where(onehot, 1.0, 0.0).astype(BF))
    carry = carry_ref[...]
    rank = jnp.sum(jnp.where(onehot, prefix - 1.0 + carry, 0.0), axis=-1, keepdims=True)
    carry = carry + prefix[tm - 1:tm, :]
    carry_ref[...] = carry
    cnt_ref[...] = carry
    rt_ref[...] = jnp.where(lane == 0, w_a, jnp.where(lane == 1, w_b, jnp.where(
        lane == 2, bucket, jnp.where(lane == 3, rank, 0.0))))


def _outproj(layer, oa, ob, oc, u, x, mod_l, mod_row, tm, pw, counts_in):
    B, L, _ = x.shape
    nt = L // tm
    hb = tm // POOL_HALO
    nhb = L // POOL_HALO
    row = (lambda b: b) if mod_row is None else (lambda b: mod_row)

    def wspec(shape):
        nd = len(shape)
        return pl.BlockSpec((None,) + shape, lambda b, i: (layer,) + (0,) * nd)

    tok = lambda w: pl.BlockSpec((None, tm, w), lambda b, i: (b, i, 0))
    in_specs = [
        tok(256), tok(256), tok(256), tok(MIX_D),
        pl.BlockSpec((None, POOL_HALO, MIX_D), lambda b, i: (b, jnp.maximum(i * hb - 1, 0), 0)),
        pl.BlockSpec((None, POOL_HALO, MIX_D), lambda b, i: (b, jnp.minimum((i + 1) * hb, nhb - 1), 0)),
        tok(D_MODEL),
        pl.BlockSpec((None, N_MOD, D_MODEL), lambda b, i: (row(b), 0, 0)),
        wspec((4, D_MODEL)),
        wspec((D_MODEL, D_MODEL)),
        wspec((MIX_D, MIX_D)),
        wspec((1, MIX_D)),
        wspec((D_MODEL, 2 * LANES)),
        wspec((1, LANES)),
        pl.BlockSpec((1, LANES), lambda b, i: (0, 0)),
    ]
    return pl.pallas_call(
        functools.partial(_outproj_kernel, tm=tm, seq_len=L, nt=nt),
        out_shape=[jax.ShapeDtypeStruct((B, L, D_MODEL), F32),
                   jax.ShapeDtypeStruct((B * L,) + TOK_TILE, F32),
                   jax.ShapeDtypeStruct((B, L, LANES), F32),
                   jax.ShapeDtypeStruct((1, LANES), F32)],
        grid=(B, nt),
        in_specs=in_specs,
        out_specs=[tok(D_MODEL),
                   pl.BlockSpec((tm,) + TOK_TILE, lambda b, i: (b * nt + i, 0, 0)),
                   tok(LANES), pl.BlockSpec((1, LANES), lambda b, i: (0, 0))],
        scratch_shapes=[pltpu.VMEM((1, LANES), F32)],
        compiler_params=_params(("arbitrary", "arbitrary")),
        name="outproj",
    )(oa, ob, oc, u, u, u, x, mod_l, pw["norm_g"], pw["w_out"], pw["pool_bd"], pw["pool_scale"],
      pw["r_w"], pw["r_b"], counts_in)


def _route_tables(routes, counts, n_tiles):
    bucket = jnp.concatenate([rt[:, 2] for rt in routes]).astype(jnp.int32)
    rank = jnp.concatenate([rt[:, 3] for rt in routes]).astype(jnp.int32)
    cnt = counts[0, :N_BUCKETS].astype(jnp.int32)
    tiles_b = (cnt + MOE_TM - 1) // MOE_TM
    ends = jnp.cumsum(tiles_b)
    starts = ends - tiles_b
    pos = jnp.take(starts * MOE_TM, bucket) + rank
    n_used = ends[-1]
    tile = jnp.arange(n_tiles, dtype=jnp.int32)
    src = jnp.minimum(tile, n_used - 1)
    tb = jnp.sum((src[:, None] >= ends[None, :]).astype(jnp.int32), axis=1)
    tb = jnp.minimum(tb, N_BUCKETS - 1)
    grp, pidx = tb // PAIRS_PER_GROUP, tb % PAIRS_PER_GROUP
    ea = grp * EXPERTS_PER_GROUP + jnp.take(jnp.array(PAIR_A, jnp.int32), pidx)
    eb = grp * EXPERTS_PER_GROUP + jnp.take(jnp.array(PAIR_B, jnp.int32), pidx)
    zlast = jnp.where(tiles_b > 0, (ends - 1) * MOE_TM, -1)
    idle = n_used + jnp.arange(N_BUCKETS, dtype=jnp.int32)
    zidle = jnp.where(idle < n_tiles, idle * MOE_TM, -1)
    pad_lo = jnp.concatenate([starts * MOE_TM + cnt, (n_used * MOE_TM)[None]])
    pad_hi = jnp.concatenate([ends * MOE_TM, jnp.full((1,), n_tiles * MOE_TM, jnp.int32)])
    zrow = jnp.concatenate([zlast, zidle, pad_lo, pad_hi]).astype(jnp.int32)
    return pos, zrow, src, ea, eb, n_used.reshape(1).astype(jnp.int32)


def _row_copy(src_ref, src_row, dst_ref, dst_row, sem):
    return pltpu.make_async_copy(src_ref.at[src_row], dst_ref.at[dst_row], sem)


def _scatter_kernel(pos_ref, zrow_ref, *refs, tm, steps):
    ng = len(steps)
    hs_ref, ws_ref, inv_ref, zbuf_ref, zw_ref, sem, zsem = refs[2 * ng:]
    i = pl.program_id(0)

    @pl.when(i == 0)
    def _():
        zbuf_ref[...] = jnp.zeros_like(zbuf_ref)
        zw_ref[...] = jnp.zeros_like(zw_ref)
        for start in (True, False):
            for b in range(2 * N_BUCKETS):
                @pl.when(zrow_ref[b] >= 0)
                def _():
                    z0 = pl.multiple_of(zrow_ref[b], MOE_TM)
                    for cp in (pltpu.make_async_copy(zbuf_ref, hs_ref.at[pl.ds(z0, MOE_TM)], zsem),
                               pltpu.make_async_copy(zw_ref, ws_ref.at[pl.ds(z0, MOE_TM)], zsem)):
                        cp.start() if start else cp.wait()

        def no_token(s, carry):
            inv_ref[s] = -1
            return carry

        for b in range(N_BUCKETS + 1):
            lax.fori_loop(zrow_ref[2 * N_BUCKETS + b], zrow_ref[3 * N_BUCKETS + 1 + b], no_token, 0)

    def invert(r):
        inv_ref[pos_ref[i * tm + r]] = i * tm + r

    first = 0
    for g in range(ng):
        hr_ref, rt_ref = refs[2 * g], refs[2 * g + 1]

        def copies(r, hr_ref=hr_ref, rt_ref=rt_ref):
            p = pos_ref[i * tm + r]
            return (_row_copy(hr_ref, r, hs_ref, p, sem),
                    pltpu.make_async_copy(rt_ref.at[pl.ds(r, 1)], ws_ref.at[pl.ds(p, 1)], sem))

        @pl.when((i >= first) & (i < first + steps[g]))
        def _():
            _row_dma_loop(tm, copies, invert)

        first += steps[g]


def _row_dma_loop(n, copies, on_issue=None):
    def issue(g, carry):
        for k in range(DMA_UNROLL):
            for cp in copies(g * DMA_UNROLL + k):
                cp.start(priority=k % 2)
            if on_issue is not None:
                on_issue(g * DMA_UNROLL + k)
        return carry

    lax.fori_loop(0, n // DMA_UNROLL, issue, 0)

    def drain(g, carry):
        for k in range(DMA_UNROLL):
            for cp in copies(g * DMA_UNROLL + k):
                cp.wait()
        return carry

    lax.fori_loop(0, n // DMA_UNROLL, drain, 0)


def _scatter_rows(hrs, routes, pos, zrow, n_tiles, tm):
    steps = [hr.shape[0] // tm for hr in hrs]
    in_specs, args, first = [], [], 0
    for hr, rt, n in zip(hrs, routes, steps):
        blk = lambda i, first=first, n=n: jnp.clip(i - first, 0, n - 1)
        in_specs += [pl.BlockSpec((tm,) + TOK_TILE, lambda i, pos, zrow, blk=blk: (blk(i), 0, 0)),
                     pl.BlockSpec((tm, LANES), lambda i, pos, zrow, blk=blk: (blk(i), 0))]
        args += [hr, rt]
        first += n
    return pl.pallas_call(
        functools.partial(_scatter_kernel, tm=tm, steps=tuple(steps)),
        out_shape=[jax.ShapeDtypeStruct((n_tiles * MOE_TM,) + TOK_TILE, F32),
                   jax.ShapeDtypeStruct((n_tiles * MOE_TM, LANES), F32),
                   jax.ShapeDtypeStruct((n_tiles * MOE_TM,), jnp.int32)],
        grid_spec=pltpu.PrefetchScalarGridSpec(
            num_scalar_prefetch=2,
            grid=(sum(steps),),
            in_specs=in_specs,
            out_specs=[pl.BlockSpec(memory_space=pl.ANY), pl.BlockSpec(memory_space=pl.ANY),
                       pl.BlockSpec(memory_space=pltpu.SMEM)],
            scratch_shapes=[pltpu.VMEM((MOE_TM,) + TOK_TILE, F32), pltpu.VMEM((MOE_TM, LANES), F32),
                            pltpu.SemaphoreType.DMA(()), pltpu.SemaphoreType.DMA(())]),
        compiler_params=_params(("arbitrary",)),
        name="moe_scatter",
    )(pos, zrow, *args)


def _tile_row_copies(j, inv_ref, obuf_ref, ys_ref, sem, n_tok, start):
    slot = j % 2
    for r in range(MOE_TM):
        tok = inv_ref[j * MOE_TM + r]
        dst = jnp.where(tok < 0, n_tok + slot * MOE_TM + r, tok)
        cp = _row_copy(obuf_ref.at[slot], r, ys_ref, dst, sem.at[slot])
        cp.start(priority=r % 2) if start else cp.wait()


def _moe_kernel(src_ref, ea_ref, eb_ref, nu_ref, inv_ref, hs_ref, ws_ref,
                wgua_ref, wgub_ref, wda_ref, wdb_ref, ys_ref, obuf_ref, sem, *, n_tok):
    j = pl.program_id(0)
    last = pl.num_programs(0) - 1
    copies = functools.partial(_tile_row_copies, inv_ref=inv_ref, obuf_ref=obuf_ref, ys_ref=ys_ref,
                               sem=sem, n_tok=n_tok)

    @pl.when(j == 0)
    def _():
        obuf_ref[...] = jnp.zeros_like(obuf_ref)
        for slot in range(2):
            cp = pltpu.make_async_copy(obuf_ref.at[slot], ys_ref.at[pl.ds(n_tok + slot * MOE_TM, MOE_TM)],
                                       sem.at[slot])
            cp.start()
            cp.wait()

    n_used = nu_ref[0]

    def experts():
        h = hs_ref[...].reshape(MOE_TM, D_MODEL).astype(BF)
        w = ws_ref[...]
        gua = _dot(h, wgua_ref[...])
        gub = _dot(h, wgub_ref[...])
        ga, gb = gua[:, :D_EXPERT], gub[:, :D_EXPERT]
        ya = _dot(((ga * jax.nn.sigmoid(ga)) * gua[:, D_EXPERT:]).astype(BF), wda_ref[...])
        yb = _dot(((gb * jax.nn.sigmoid(gb)) * gub[:, D_EXPERT:]).astype(BF), wdb_ref[...])
        obuf_ref[j % 2] = (w[:, 0:1] * ya + w[:, 1:2] * yb).reshape((MOE_TM,) + TOK_TILE)

    @pl.when((j >= 2) & (j - 2 < n_used))
    def _():
        copies(j - 2, start=False)

    @pl.when((j >= 1) & (j < n_used))
    def _():
        copies(j - 1, start=True)
        experts()

    @pl.when((j == 0) & (j < n_used))
    def _():
        experts()

    @pl.when((j >= 1) & (j == n_used))
    def _():
        copies(j - 1, start=True)

    @pl.when((j == last) & (j < n_used))
    def _():
        copies(j, start=True)

    @pl.when((j == last) & (j - 1 < n_used))
    def _():
        copies(j - 1, start=False)

    @pl.when((j == last) & (j < n_used))
    def _():
        copies(j, start=False)


def _moe_sorted(layer, n_tok, hs, ws, src, ea, eb, n_used, inv, pw):
    n_tiles = hs.shape[0] // MOE_TM
    assert n_tiles >= 2
    wgu = lambda sel: pl.BlockSpec((None, None, D_MODEL, 2 * D_EXPERT),
                                   lambda j, src, ea, eb, *_: (layer, (ea, eb)[sel][j], 0, 0))
    wd = lambda sel: pl.BlockSpec((None, None, D_EXPERT, D_MODEL),
                                  lambda j, src, ea, eb, *_: (layer, (ea, eb)[sel][j], 0, 0))
    return pl.pallas_call(
        functools.partial(_moe_kernel, n_tok=n_tok),
        out_shape=jax.ShapeDtypeStruct((n_tok + 2 * MOE_TM,) + TOK_TILE, F32),
        grid_spec=pltpu.PrefetchScalarGridSpec(
            num_scalar_prefetch=5,
            grid=(n_tiles,),
            in_specs=[pl.BlockSpec((MOE_TM,) + TOK_TILE, lambda j, src, *_: (src[j], 0, 0)),
                      pl.BlockSpec((MOE_TM, LANES), lambda j, src, *_: (src[j], 0)),
                      wgu(0), wgu(1), wd(0), wd(1)],
            out_specs=pl.BlockSpec(memory_space=pl.ANY),
            scratch_shapes=[pltpu.VMEM((2, MOE_TM) + TOK_TILE, F32), pltpu.SemaphoreType.DMA((2,))]),
        compiler_params=_params(("arbitrary",)),
        name="moe",
    )(src, ea, eb, n_used, inv, hs, ws, pw["w_gu"], pw["w_gu"], pw["w_down"], pw["w_down"])


def _residual_kernel(ys_ref, x_ref, mod_ref, ng_ref, o_ref, *, tm):
    fx = ys_ref[...].reshape(tm, D_MODEL)
    o_ref[...] = x_ref[...] + mod_ref[5:6, :] * _rms(fx, ng_ref[3:4, :])


def _moe_residual(layer, ys, first_row, x, mod_l, mod_row, tm, pw):
    B, L, _ = x.shape
    T = B * L
    tiles_per_seq = L // tm
    first_blk = first_row // tm
    row = (lambda i: i // tiles_per_seq) if mod_row is None else (lambda i: mod_row)
    out = pl.pallas_call(
        functools.partial(_residual_kernel, tm=tm),
        out_shape=jax.ShapeDtypeStruct((T, D_MODEL), F32),
        grid=(T // tm,),
        in_specs=[pl.BlockSpec((tm,) + TOK_TILE, lambda i: (first_blk + i, 0, 0)),
                  pl.BlockSpec((tm, D_MODEL), lambda i: (i, 0)),
                  pl.BlockSpec((None, N_MOD, D_MODEL), lambda i: (row(i), 0, 0)),
                  pl.BlockSpec((None, 4, D_MODEL), lambda i: (layer, 0, 0))],
        out_specs=pl.BlockSpec((tm, D_MODEL), lambda i: (i, 0)),
        compiler_params=_params(("arbitrary",)),
        name="moe_residual",
    )(ys, x.reshape(T, D_MODEL), mod_l, pw["norm_g"])
    return out.reshape(B, L, D_MODEL)


def _moe(layer, groups, counts, mod_l, tm, pw):
    hrs = [g[0] for g in groups]
    routes = [g[1].reshape(-1, LANES) for g in groups]
    T = sum(hr.shape[0] for hr in hrs)
    n_tiles = T // MOE_TM + N_BUCKETS
    pos, zrow, src, ea, eb, n_used = _route_tables(routes, counts, n_tiles)
    hs, ws, inv = _scatter_rows(hrs, routes, pos, zrow, n_tiles, tm)
    ys = _moe_sorted(layer, T, hs, ws, src, ea, eb, n_used, inv, pw)
    outs, first = [], 0
    for hr, _, x, mod_row, tm_res in groups:
        outs.append(_moe_residual(layer, ys, first, x, mod_l, mod_row, tm_res, pw))
        first += hr.shape[0]
    return outs


def _axial(length, dim):
    rows = length // GRID_W
    row = jnp.repeat(jnp.arange(rows), GRID_W).astype(F32)
    col = jnp.tile(jnp.arange(GRID_W), rows).astype(F32)
    n = dim // 4
    inv = ROPE_BASE ** (-jnp.arange(n, dtype=F32) / n)
    ang = jnp.concatenate([row[:, None] * inv, col[:, None] * inv], axis=-1)
    return jnp.cos(ang), jnp.sin(ang)


def _rope_tables(length, rotate):
    if rotate:
        cos_a, sin_a = _axial(length, MLA_ROPE)
        cos_b, sin_b = _axial(length, SWA_HEAD_DIM)
    else:
        cos_a, sin_a = jnp.ones((length, MLA_ROPE // 2), F32), jnp.zeros((length, MLA_ROPE // 2), F32)
        cos_b, sin_b = jnp.ones((length, SWA_HEAD_DIM // 2), F32), jnp.zeros((length, SWA_HEAD_DIM // 2), F32)
    one = jnp.ones((length, MLA_NOPE), F32)
    zero = jnp.zeros((length, MLA_NOPE), F32)
    sa, sb, sc = MLA_SCALE * LOG2E, SWA_SCALE * LOG2E, DIFF_SCALE * LOG2E
    c_qa = jnp.tile(jnp.concatenate([one, cos_a, cos_a], -1), (1, MLA_HEADS)) * sa
    s_qa = jnp.tile(jnp.concatenate([zero, -sin_a, sin_a], -1), (1, MLA_HEADS)) * sa
    c_b = jnp.concatenate([cos_b, cos_b], -1)
    s_b = jnp.concatenate([-sin_b, sin_b], -1)
    c_c = jnp.tile(jnp.concatenate([cos_a, cos_a], -1), (1, 2 * DIFF_HEADS)) * sc
    s_c = jnp.tile(jnp.concatenate([-sin_a, sin_a], -1), (1, 2 * DIFF_HEADS)) * sc
    tab_c = jnp.concatenate([c_qa, jnp.tile(c_b, (1, SWA_HEADS)) * sb, jnp.tile(c_b, (1, SWA_KV_HEADS)), c_c], -1)
    tab_s = jnp.concatenate([s_qa, jnp.tile(s_b, (1, SWA_HEADS)) * sb, jnp.tile(s_b, (1, SWA_KV_HEADS)), s_c], -1)
    m_qa = jnp.tile(jnp.concatenate([jnp.zeros((MLA_NOPE,)), jnp.ones((MLA_ROPE // 2,)),
                                     jnp.zeros((MLA_ROPE // 2,))]), MLA_HEADS)
    m_b = jnp.concatenate([jnp.ones((SWA_HEAD_DIM // 2,)), jnp.zeros((SWA_HEAD_DIM // 2,))])
    m_c = jnp.concatenate([jnp.ones((DIFF_QK // 2,)), jnp.zeros((DIFF_QK // 2,))])
    msk = jnp.concatenate([m_qa, jnp.tile(m_b, SWA_HEADS + SWA_KV_HEADS), jnp.tile(m_c, 2 * DIFF_HEADS)])
    return tab_c, tab_s, msk.astype(F32)[None, :], cos_a.T, sin_a.T


def _prep_weights(norm_g, w_in, mla_q_gain, mla_kv_gain, mla_w_uq, mla_w_ukv, diff_subln, pool_w,
                  pool_scale, w_out, router_gw, router_gb, router_ew, router_eb, exp_w_gu, exp_w_down):
    sizes = (MLA_Q_RANK, MLA_KV_RANK, MLA_ROPE, 256, 128, 128, 256, 256, 256, MIX_D)
    offs = [0]
    for s in sizes:
        offs.append(offs[-1] + s)
    cq, ckv, kr, bq, bk, bv, dq, dk, dv, u = [w_in[:, :, offs[j]:offs[j + 1]] for j in range(len(sizes))]
    w_r = jnp.concatenate([cq, bq, dq, dv, u, ckv, bk, bv], axis=-1).astype(BF)
    w_kt = jnp.swapaxes(jnp.concatenate([ckv, kr, dk], axis=-1), 1, 2).astype(BF)
    ukv = mla_w_ukv.reshape(DEPTH, MLA_KV_RANK, MLA_HEADS, MLA_NOPE + MLA_V)
    w_uk = ukv[..., :MLA_NOPE].reshape(DEPTH, MLA_KV_RANK, MLA_HEADS * MLA_NOPE)
    w_uv = ukv[..., MLA_NOPE:].reshape(DEPTH, MLA_KV_RANK, MLA_HEADS * MLA_V)
    eye = jnp.eye(len(POOL_WINDOWS), dtype=F32)
    pool_bd = jnp.einsum('lgcd,gh->lgchd', pool_w, eye).reshape(DEPTH, MIX_D, MIX_D)
    r_w = jnp.concatenate([router_ew, router_gw], axis=-1)
    r_w = jnp.pad(r_w, ((0, 0), (0, 0), (0, LANES - r_w.shape[-1])))
    r_hi = r_w.astype(BF)
    r_lo = (r_w - r_hi.astype(F32)).astype(BF)
    r_b = jnp.concatenate([router_eb, router_gb], axis=-1)
    r_b = jnp.pad(r_b, ((0, 0), (0, LANES - r_b.shape[-1])))[:, None, :]
    return dict(
        norm_g=norm_g, w_r=w_r, w_kt=w_kt,
        q_gain=mla_q_gain[:, None, :], kv_gain_r=mla_kv_gain[:, None, :], kv_gain_c=mla_kv_gain[:, :, None],
        w_uq=mla_w_uq.astype(BF), w_ukt=jnp.swapaxes(w_uk, 1, 2).astype(BF), w_uv=w_uv.astype(BF),
        subln=jnp.tile(diff_subln, (1, LANES // DIFF_V))[:, None, :],
        pool_bd=pool_bd.astype(BF), pool_scale=pool_scale[:, None, :], w_out=w_out.astype(BF),
        r_w=jnp.concatenate([r_hi, r_lo], axis=-1), r_b=r_b, w_gu=exp_w_gu.astype(BF), w_down=exp_w_down.astype(BF))


def kernel(x, c, ctx, c_ctx, ada_w, ada_b, norm_g, w_in, mla_q_gain, mla_kv_gain, mla_w_uq, mla_w_ukv,
           swa_sink, diff_lambda, diff_subln, pool_w, pool_scale, w_out, router_gw, router_gb,
           router_ew, router_eb, exp_w_gu, exp_w_down):
    B, L, D = x.shape
    C = ctx.shape[1]
    assert D == D_MODEL and B + 1 <= MOD_ROWS and L % 512 == 0 and C % 256 == 0
    pw = _prep_weights(norm_g, w_in, mla_q_gain, mla_kv_gain, mla_w_uq, mla_w_ukv, diff_subln, pool_w,
                       pool_scale, w_out, router_gw, router_gb, router_ew, router_eb, exp_w_gu, exp_w_down)
    cc = jnp.concatenate([c, c_ctx[None, :], jnp.zeros((MOD_ROWS - B - 1, D), F32)], axis=0)
    mod = jnp.swapaxes(_adaln(cc, ada_w, ada_b), 1, 2)
    tabs_x = _rope_tables(L, True)
    tabs_c = _rope_tables(C, False)
    tm_x, tm_c, tq, tq_x = 512, 256, 256, 512

    h_ctx = ctx
    for layer in range(DEPTH):
        update_ctx = layer < DEPTH - 1
        lam_init = 0.8 - 0.6 * math.exp(-0.3 * layer)
        mod_l = mod[layer]
        qa, qb, qc, u, va, vb, vc, kb, kta, ktc = _inproj(layer, x, mod_l, None, tm_x, pw, tabs_x)
        cqa, cqb, cqc, cu, cva, cvb, cvc, ckb, ckta, cktc = _inproj(layer, h_ctx, mod_l, B, tm_c, pw, tabs_c)
        diff_extra = (diff_lambda, pw["subln"])

        oa = _dense_attn("mla", layer, qa, [kta, ckta], [va, cva], tq_x)
        ob = _swa(layer, swa_sink, qb, kb, vb, ckb, cvb, tq)
        oc = _dense_attn("diff", layer, qc, [ktc, cktc], [vc, cvc], tq_x, diff_extra, lam_init)
        x, hrx, rtx, counts = _outproj(layer, oa, ob, oc, u, x, mod_l, None, tm_x, pw, jnp.zeros((1, LANES), F32))
        groups = [(hrx, rtx, x, None, tm_x)]
        if update_ctx:
            coa = _dense_attn("mla", layer, cqa, [ckta], [cva], tq)
            cob = _swa(layer, swa_sink, cqb, None, None, ckb, cvb, tq)
            coc = _dense_attn("diff", layer, cqc, [cktc], [cvc], tq, diff_extra, lam_init)
            h_ctx, hrc, rtc, counts = _outproj(layer, coa, cob, coc, cu, h_ctx, mod_l, B, tm_c, pw, counts)
            groups.append((hrc, rtc, h_ctx, B, tm_c))
        outs = _moe(layer, groups, counts, mod_l, tm_x, pw)
        x = outs[0]
        if update_ctx:
            h_ctx = outs[1]
    return x
```

```python
import functools
import math

import jax
import jax.numpy as jnp
from jax import lax
from jax.experimental import pallas as pl
from jax.experimental.pallas import tpu as pltpu

F32 = jnp.float32
BF = jnp.bfloat16

D_MODEL = 1024
DEPTH = 2
GRID_W = 64
EPS = 1e-6
ROPE_BASE = 10000.0
N_MOD = 6

MLA_HEADS = 4
MLA_Q_RANK = 256
MLA_KV_RANK = 128
MLA_NOPE = 64
MLA_ROPE = 32
MLA_V = 64
MLA_QK = MLA_NOPE + MLA_ROPE
MLA_SCALE = MLA_QK ** -0.5

SWA_HEADS = 4
SWA_KV_HEADS = 2
SWA_HEAD_DIM = 64
SWA_WINDOW = 128
SWA_SCALE = SWA_HEAD_DIM ** -0.5

DIFF_HEADS = 4
DIFF_QK = 32
DIFF_V = 64
DIFF_SCALE = DIFF_QK ** -0.5

POOL_WINDOWS = (2, 4, 8, 16)
POOL_GROUP = 64
POOL_HALO = 8
MIX_D = len(POOL_WINDOWS) * POOL_GROUP

N_EXPERT_GROUPS = 4
EXPERTS_PER_GROUP = 4
N_EXPERTS = 16
D_EXPERT = 512
PAIR_A = (0, 2, 2, 3, 3, 3)
PAIR_B = (1, 1, 0, 0, 1, 2)
PAIRS_PER_GROUP = len(PAIR_A)
N_BUCKETS = N_EXPERT_GROUPS * PAIRS_PER_GROUP

LOG2E = math.log2(math.e)
NEG = -1e30
LANES = 128
MOD_ROWS = 24
SUBLANES = 8
TOK_TILE = (D_MODEL // LANES, LANES)
MOE_TM = 256
DMA_UNROLL = 8
KEY_CHUNK = 256
VMEM_LIMIT = 48 * 1024 * 1024

R_CQ, R_BQ, R_DQ, R_DV, R_U, R_CKV, R_BK, R_BV, R_END = 0, 256, 512, 768, 1024, 1280, 1408, 1536, 1664
T_CKV, T_KR, T_DK, T_END = 0, 128, 160, 416
P_QA, P_QB, P_KB, P_QC, P_END = 0, 384, 640, 768, 1024


def _dot(a, b):
    return jnp.dot(a, b, preferred_element_type=F32)


def _dot_nt(a, b):
    return lax.dot_general(a, b, (((1,), (1,)), ((), ())), preferred_element_type=F32)


def _rms(x, g):
    return x * lax.rsqrt(jnp.mean(x * x, axis=-1, keepdims=True) + EPS) * g


def _params(sem):
    return pltpu.CompilerParams(dimension_semantics=sem, vmem_limit_bytes=VMEM_LIMIT)


def _adaln_kernel(c_ref, w_ref, b_ref, o_ref):
    c = c_ref[...]
    s = (c * jax.nn.sigmoid(c)).astype(BF)
    o_ref[...] = _dot(s, w_ref[...].astype(BF)) + b_ref[...]


def _adaln(cc, ada_w, ada_b):
    return pl.pallas_call(
        _adaln_kernel,
        out_shape=jax.ShapeDtypeStruct((DEPTH, N_MOD, MOD_ROWS, D_MODEL), F32),
        grid=(DEPTH, N_MOD),
        in_specs=[
            pl.BlockSpec((MOD_ROWS, D_MODEL), lambda l, j: (0, 0)),
            pl.BlockSpec((None, D_MODEL, D_MODEL), lambda l, j: (l, 0, j)),
            pl.BlockSpec((None, None, 1, D_MODEL), lambda l, j: (l, j, 0, 0)),
        ],
        out_specs=pl.BlockSpec((None, None, MOD_ROWS, D_MODEL), lambda l, j: (l, j, 0, 0)),
        compiler_params=_params(("arbitrary", "arbitrary")),
        name="adaln",
    )(cc, ada_w, ada_b.reshape(DEPTH, N_MOD, 1, D_MODEL))


def _rope_rows(xf, c, s, m1, half):
    w = xf.shape[-1]
    up = pltpu.roll(xf, w - half, axis=1)
    dn = pltpu.roll(xf, half, axis=1)
    return xf * c + jnp.where(m1 > 0.5, up, dn) * s


def _rope_cols(xt, c, s):
    h = xt.shape[0] // 2
    x1, x2 = xt[:h], xt[h:]
    return jnp.concatenate([x1 * c - x2 * s, x1 * s + x2 * c], axis=0)


def _inproj_kernel(x_ref, mod_ref, ng_ref, wr_ref, wkt_ref, qg_ref, kvgr_ref, kvgc_ref,
                   wuq_ref, wukt_ref, wuv_ref, tc_ref, ts_ref, m_ref, ct_ref, st_ref,
                   qa_ref, qb_ref, qc_ref, u_ref, va_ref, vb_ref, vc_ref, kb_ref, kta_ref, ktc_ref):
    x = x_ref[...]
    mod = mod_ref[...]
    hx = _rms(x, ng_ref[0:1, :]) * (1.0 + mod[1:2, :]) + mod[0:1, :]
    hb = hx.astype(BF)
    zr = _dot(hb, wr_ref[...])
    zk = _dot_nt(wkt_ref[...], hb)

    tc = tc_ref[...]
    ts = ts_ref[...]
    m1 = m_ref[...]

    cqn = _rms(zr[:, R_CQ:R_CQ + MLA_Q_RANK], qg_ref[...]).astype(BF)
    qa = _dot(cqn, wuq_ref[...])
    qa_ref[...] = _rope_rows(qa, tc[:, P_QA:P_QB], ts[:, P_QA:P_QB], m1[:, P_QA:P_QB],
                             MLA_ROPE // 2).astype(BF)
    qb_ref[...] = _rope_rows(zr[:, R_BQ:R_DQ], tc[:, P_QB:P_KB], ts[:, P_QB:P_KB], m1[:, P_QB:P_KB],
                             SWA_HEAD_DIM // 2).astype(BF)
    qc_ref[...] = _rope_rows(zr[:, R_DQ:R_DV], tc[:, P_QC:P_END], ts[:, P_QC:P_END], m1[:, P_QC:P_END],
                             DIFF_QK // 2).astype(BF)
    kb_ref[...] = _rope_rows(zr[:, R_BK:R_BV], tc[:, P_KB:P_QC], ts[:, P_KB:P_QC], m1[:, P_KB:P_QC],
                             SWA_HEAD_DIM // 2).astype(BF)
    u_ref[...] = zr[:, R_U:R_CKV]
    vb_ref[...] = zr[:, R_BV:R_END].astype(BF)
    vc_ref[...] = zr[:, R_DV:R_U].astype(BF)

    ckvn = _rms(zr[:, R_CKV:R_BK], kvgr_ref[...]).astype(BF)
    va_ref[...] = _dot(ckvn, wuv_ref[...]).astype(BF)

    ckvt = zk[T_CKV:T_KR]
    ckvtn = ckvt * lax.rsqrt(jnp.mean(ckvt * ckvt, axis=0, keepdims=True) + EPS) * kvgc_ref[...]
    knt = _dot(wukt_ref[...], ckvtn.astype(BF))
    ct = ct_ref[...]
    st = st_ref[...]
    krt = _rope_cols(zk[T_KR:T_DK], ct, st)
    pieces = []
    for h in range(MLA_HEADS):
        pieces.append(knt[h * MLA_NOPE:(h + 1) * MLA_NOPE])
        pieces.append(krt)
    kta_ref[...] = jnp.concatenate(pieces, axis=0).astype(BF)

    dk = zk[T_DK:T_END]
    ktc_ref[...] = jnp.concatenate(
        [_rope_cols(dk[j * DIFF_QK:(j + 1) * DIFF_QK], ct, st) for j in range(2 * DIFF_HEADS)],
        axis=0).astype(BF)


def _inproj(layer, x, mod_l, mod_row, tm, pw, tabs):
    B, L, _ = x.shape
    nt = L // tm
    tab_c, tab_s, msk, cos_t, sin_t = tabs
    row = (lambda b: b) if mod_row is None else (lambda b: mod_row)

    def wspec(shape):
        nd = len(shape)
        return pl.BlockSpec((None,) + shape, lambda i, b: (layer,) + (0,) * nd)

    tok = lambda w: pl.BlockSpec((None, tm, w), lambda i, b: (b, i, 0))
    tokt = lambda w: pl.BlockSpec((None, w, tm), lambda i, b: (b, 0, i))
    in_specs = [
        tok(D_MODEL),
        pl.BlockSpec((None, N_MOD, D_MODEL), lambda i, b: (row(b), 0, 0)),
        wspec((4, D_MODEL)),
        wspec((D_MODEL, R_END)),
        wspec((T_END, D_MODEL)),
        wspec((1, MLA_Q_RANK)),
        wspec((1, MLA_KV_RANK)),
        wspec((MLA_KV_RANK, 1)),
        wspec((MLA_Q_RANK, MLA_HEADS * MLA_QK)),
        wspec((MLA_HEADS * MLA_NOPE, MLA_KV_RANK)),
        wspec((MLA_KV_RANK, MLA_HEADS * MLA_V)),
        pl.BlockSpec((tm, P_END), lambda i, b: (i, 0)),
        pl.BlockSpec((tm, P_END), lambda i, b: (i, 0)),
        pl.BlockSpec((1, P_END), lambda i, b: (0, 0)),
        pl.BlockSpec((MLA_ROPE // 2, tm), lambda i, b: (0, i)),
        pl.BlockSpec((MLA_ROPE // 2, tm), lambda i, b: (0, i)),
    ]
    widths = [(MLA_HEADS * MLA_QK, BF), (256, BF), (256, BF), (MIX_D, F32),
              (256, BF), (128, BF), (256, BF), (128, BF)]
    out_shape = [jax.ShapeDtypeStruct((B, L, w), dt) for w, dt in widths]
    out_specs = [tok(w) for w, _ in widths]
    out_shape += [jax.ShapeDtypeStruct((B, MLA_HEADS * MLA_QK, L), BF),
                  jax.ShapeDtypeStruct((B, 2 * DIFF_HEADS * DIFF_QK, L), BF)]
    out_specs += [tokt(MLA_HEADS * MLA_QK), tokt(2 * DIFF_HEADS * DIFF_QK)]
    return pl.pallas_call(
        _inproj_kernel,
        out_shape=out_shape,
        grid=(nt, B),
        in_specs=in_specs,
        out_specs=out_specs,
        compiler_params=_params(("arbitrary", "arbitrary")),
        name="inproj",
    )(x, mod_l, pw["norm_g"], pw["w_r"], pw["w_kt"], pw["q_gain"], pw["kv_gain_r"], pw["kv_gain_c"],
      pw["w_uq"], pw["w_ukt"], pw["w_uv"], tab_c, tab_s, msk, cos_t, sin_t)


def _attn_heads(heads, sizes):
    chunks = [(seg, c) for seg, n in enumerate(sizes) for c in range(0, n, KEY_CHUNK)]
    outs = []
    prev = None
    for step in range(len(heads) + 1):
        cur = heads[step] if step < len(heads) else None
        cur_s, mc = [], None
        if prev is not None:
            prev_s, prev_m, prev_vs, prev_x = prev
            lacc, r = None, None
        for idx, (seg, c) in enumerate(chunks):
            if cur is not None:
                s = cur[0](seg, c)
                cur_s.append(s)
                mc = s if mc is None else jnp.maximum(mc, s)
            if prev is not None:
                e = jnp.exp2(prev_s[idx] - prev_m)
                rs = _dot(e.astype(BF), prev_vs[seg][c:c + KEY_CHUNK, :])
                lacc = e if lacc is None else lacc + e
                r = rs if r is None else r + rs
        if prev is not None:
            l = jnp.sum(lacc, axis=-1, keepdims=True)
            if prev_x is not None:
                l = l + jnp.exp2(prev_x - prev_m)
            outs.append(r / l)
        if cur is not None:
            m = jnp.max(mc, axis=-1, keepdims=True)
            if cur[2] is not None:
                m = jnp.maximum(m, cur[2])
            prev = (cur_s, m, cur[1], cur[2])
        else:
            prev = None
    return outs


def _kt_scores(qh, kt_refs, sl, seg, c):
    return _dot(qh, kt_refs[seg][sl, c:c + KEY_CHUNK])


def _mla_kernel(*refs, nseg):
    q_ref = refs[0]
    kt_refs = refs[1:1 + nseg]
    v_refs = refs[1 + nseg:1 + 2 * nseg]
    o_ref = refs[1 + 2 * nseg]
    q = q_ref[...]
    lane = lax.broadcasted_iota(jnp.int32, (1, LANES), 1)
    heads = []
    for h in range(MLA_HEADS):
        sl = slice(h * MLA_QK, (h + 1) * MLA_QK)
        vs = [v[:, (h // 2) * LANES:(h // 2 + 1) * LANES] for v in v_refs]
        heads.append((functools.partial(_kt_scores, q[:, sl], kt_refs, sl), vs, None))
    outs = _attn_heads(heads, [kt.shape[1] for kt in kt_refs])
    blocks = [jnp.where(lane < MLA_V, outs[2 * p], outs[2 * p + 1]) for p in range(MLA_HEADS // 2)]
    o_ref[...] = jnp.concatenate(blocks, axis=-1).astype(BF)


def _diff_kernel(*refs, nseg, lam_init):
    q_ref = refs[0]
    kt_refs = refs[1:1 + nseg]
    v_refs = refs[1 + nseg:1 + 2 * nseg]
    dl_ref, g_ref, o_ref = refs[1 + 2 * nseg:]
    q = q_ref[...]
    dl = dl_ref[...]
    lam = (jnp.exp(jnp.sum(dl[0:1] * dl[1:2], axis=-1, keepdims=True))
           - jnp.exp(jnp.sum(dl[2:3] * dl[3:4], axis=-1, keepdims=True)) + lam_init)
    lane = lax.broadcasted_iota(jnp.int32, (1, LANES), 1)
    lo = lane < DIFF_V
    heads = []
    for j in range(2 * DIFF_HEADS):
        sl = slice(j * DIFF_QK, (j + 1) * DIFF_QK)
        vs = [v[:, (j // 4) * LANES:(j // 4 + 1) * LANES] for v in v_refs]
        heads.append((functools.partial(_kt_scores, q[:, sl], kt_refs, sl), vs, None))
    comps = _attn_heads(heads, [kt.shape[1] for kt in kt_refs])
    blocks = []
    for p in range(DIFF_HEADS // 2):
        pair = [comps[2 * h] - lam * comps[2 * h + 1] for h in (2 * p, 2 * p + 1)]
        o = jnp.where(lo, pair[0], pair[1])
        o2 = o * o
        ms_lo = jnp.sum(jnp.where(lo, o2, 0.0), axis=-1, keepdims=True) * (1.0 / DIFF_V)
        ms_hi = jnp.sum(jnp.where(lo, 0.0, o2), axis=-1, keepdims=True) * (1.0 / DIFF_V)
        ms = jnp.where(lo, ms_lo, ms_hi)
        blocks.append(o * lax.rsqrt(ms + EPS) * g_ref[...] * (1.0 - lam_init))
    o_ref[...] = jnp.concatenate(blocks, axis=-1).astype(BF)


def _dense_attn(kind, layer, q, kts, vs, tq, extra_in=(), lam_init=0.0):
    B, L, wq = q.shape
    nseg = len(kts)
    in_specs = [pl.BlockSpec((None, tq, wq), lambda b, i: (b, i, 0))]
    for kt in kts:
        in_specs.append(pl.BlockSpec((None,) + kt.shape[1:], lambda b, i: (b, 0, 0)))
    for v in vs:
        in_specs.append(pl.BlockSpec((None,) + v.shape[1:], lambda b, i: (b, 0, 0)))
    if kind == "mla":
        body = functools.partial(_mla_kernel, nseg=nseg)
    else:
        body = functools.partial(_diff_kernel, nseg=nseg, lam_init=lam_init)
        dl, g = extra_in
        in_specs.append(pl.BlockSpec((None, 4, DIFF_QK), lambda b, i: (layer, 0, 0)))
        in_specs.append(pl.BlockSpec((None, 1, LANES), lambda b, i: (layer, 0, 0)))
    return pl.pallas_call(
        body,
        out_shape=jax.ShapeDtypeStruct((B, L, 256), BF),
        grid=(B, L // tq),
        in_specs=in_specs,
        out_specs=pl.BlockSpec((None, tq, 256), lambda b, i: (b, i, 0)),
        compiler_params=_params(("arbitrary", "arbitrary")),
        name=kind,
    )(q, *kts, *vs, *extra_in)


def _swa_kernel(*refs, layer, tq, seq_len, local):
    if local:
        sink_ref, q_ref, kl_ref, vl_ref, kc_ref, vc_ref, o_ref = refs
    else:
        sink_ref, q_ref, kc_ref, vc_ref, o_ref = refs
    i = pl.program_id(1)
    q = q_ref[...]
    lane = lax.broadcasted_iota(jnp.int32, (1, LANES), 1)
    rows = lax.broadcasted_iota(jnp.int32, (2 * tq, 1), 0)
    top = rows < tq
    if local:
        win = tq + 2 * SWA_WINDOW
        q0 = i * tq
        ws = pl.multiple_of(jnp.clip(q0 - SWA_WINDOW, 0, seq_len - win), LANES)
        kwin = kl_ref[pl.ds(ws, win), :]
        vwin = vl_ref[pl.ds(ws, win), :]
        qpos = q0 + jnp.where(top, rows, rows - tq)
        kpos = ws + lax.broadcasted_iota(jnp.int32, (1, win), 1)
        band = jnp.abs(kpos - qpos) <= SWA_WINDOW
    kc = kc_ref[...]
    vs = [vc_ref[...]] + ([vwin] if local else [])
    sizes = [kc.shape[0]] + ([win] if local else [])

    def scores(q2, sl, seg, c):
        if seg == 0:
            return _dot_nt(q2, kc[c:c + KEY_CHUNK, sl])
        return jnp.where(band[:, c:c + KEY_CHUNK], _dot_nt(q2, kwin[c:c + KEY_CHUNK, sl]), NEG)

    heads = []
    for k in range(SWA_KV_HEADS):
        q2 = jnp.concatenate([q[:, (2 * k) * SWA_HEAD_DIM:(2 * k + 1) * SWA_HEAD_DIM],
                              q[:, (2 * k + 1) * SWA_HEAD_DIM:(2 * k + 2) * SWA_HEAD_DIM]], axis=0)
        sl = slice(k * SWA_HEAD_DIM, (k + 1) * SWA_HEAD_DIM)
        sk = jnp.where(top, sink_ref[layer, 2 * k], sink_ref[layer, 2 * k + 1]) * LOG2E
        heads.append((functools.partial(scores, q2, sl), vs, sk))
    outs = _attn_heads(heads, sizes)
    blocks = []
    for k, r in enumerate(outs):
        r_top, r_bot = r[:tq], r[tq:]
        if k == 0:
            r_bot = pltpu.roll(r_bot, SWA_HEAD_DIM, axis=1)
        else:
            r_top = pltpu.roll(r_top, SWA_HEAD_DIM, axis=1)
        blocks.append(jnp.where(lane < SWA_HEAD_DIM, r_top, r_bot))
    o_ref[...] = jnp.concatenate(blocks, axis=-1).astype(BF)


def _swa(layer, sink, q, k_loc, v_loc, k_ctx, v_ctx, tq):
    B, L, _ = q.shape
    C = k_ctx.shape[1]
    local = k_loc is not None
    full = lambda n: pl.BlockSpec((None, n, 128), lambda b, i: (b, 0, 0))
    in_specs = [pl.BlockSpec(memory_space=pltpu.SMEM),
                pl.BlockSpec((None, tq, 256), lambda b, i: (b, i, 0))]
    args = [sink, q]
    if local:
        in_specs += [full(L), full(L)]
        args += [k_loc, v_loc]
    in_specs += [full(C), full(C)]
    args += [k_ctx, v_ctx]
    return pl.pallas_call(
        functools.partial(_swa_kernel, layer=layer, tq=tq, seq_len=L, local=local),
        out_shape=jax.ShapeDtypeStruct((B, L, 256), BF),
        grid=(B, L // tq),
        in_specs=in_specs,
        out_specs=pl.BlockSpec((None, tq, 256), lambda b, i: (b, i, 0)),
        compiler_params=_params(("arbitrary", "arbitrary")),
        name="swa",
    )(*args)


def _pool_diffs(u, prev, nxt, i, nt, tm, seq_len):
    n = tm + 2 * POOL_HALO
    ue = jnp.concatenate([jnp.where(i > 0, prev, 0.0), u, jnp.where(i < nt - 1, nxt, 0.0)], axis=0)

    def sh(a, k):
        return pltpu.roll(a, k % n, axis=0)

    s2 = ue + sh(ue, 1)
    s4 = sh(s2, 1) + sh(s2, -1)
    s8 = sh(s4, 2) + sh(s4, -2)
    s16 = sh(s8, 4) + sh(s8, -4)
    lane = lax.broadcasted_iota(jnp.int32, (1, MIX_D), 1)
    t = i * tm + lax.broadcasted_iota(jnp.int32, (tm, 1), 0)
    win = None
    cnt = None
    for gi, (w, s) in enumerate(zip(POOL_WINDOWS, (s2, s4, s8, s16))):
        half = w // 2
        c = (jnp.minimum(t - half + w, seq_len) - jnp.maximum(t - half, 0)).astype(F32)
        sv = s[POOL_HALO:POOL_HALO + tm]
        if win is None:
            win, cnt = sv, c
        else:
            sel = lane >= gi * POOL_GROUP
            win = jnp.where(sel, sv, win)
            cnt = jnp.where(sel, c, cnt)
    return win / cnt - u


def _outproj_kernel(oa_ref, ob_ref, oc_ref, u_ref, up_ref, un_ref, x_ref, mod_ref, ng_ref,
                    wo_ref, pw_ref, ps_ref, rw_ref, rb_ref,
                    cin_ref, xo_ref, hr_ref, rt_ref, cnt_ref, carry_ref, *, tm, seq_len, nt):
    i = pl.program_id(1)

    @pl.when((pl.program_id(0) == 0) & (i == 0))
    def _():
        carry_ref[...] = cin_ref[...]

    u = u_ref[...]
    dg = _pool_diffs(u, up_ref[...], un_ref[...], i, nt, tm, seq_len)
    od = _dot(dg.astype(BF), pw_ref[...]) * ps_ref[...]
    mix = jnp.concatenate([oa_ref[...], ob_ref[...], oc_ref[...], od.astype(BF)], axis=-1)
    ox = _dot(mix, wo_ref[...])
    mod = mod_ref[...]
    x = x_ref[...] + mod[2:3, :] * _rms(ox, ng_ref[1:2, :])
    xo_ref[...] = x
    h = _rms(x, ng_ref[2:3, :]) * (1.0 + mod[4:5, :]) + mod[3:4, :]
    hr_ref[...] = h.reshape((tm,) + TOK_TILE)
    hh = h.astype(BF)
    hl = (h - hh.astype(F32)).astype(BF)
    hw = _dot(hh, rw_ref[...])
    logits = hw[:, :LANES] + hw[:, LANES:] + _dot(hl, rw_ref[:, :LANES]) + rb_ref[...]

    lane = lax.broadcasted_iota(jnp.int32, (1, LANES), 1)
    lanef = lane.astype(F32)
    is_g = (lane >= N_EXPERTS) & (lane < N_EXPERTS + N_EXPERT_GROUPS)
    gl = jnp.where(is_g, logits, NEG)
    gmax = jnp.max(gl, axis=-1, keepdims=True)
    gsel = jnp.min(jnp.where(gl == gmax, lanef, 1e9), axis=-1, keepdims=True) - N_EXPERTS
    p_g = 1.0 / jnp.sum(jnp.where(is_g, jnp.exp(gl - gmax), 0.0), axis=-1, keepdims=True)
    grp = jnp.right_shift(lane, EXPERTS_PER_GROUP.bit_length() - 1).astype(F32)
    el = jnp.where((lane < N_EXPERTS) & (grp == gsel), logits, NEG)
    v1 = jnp.max(el, axis=-1, keepdims=True)
    i1 = jnp.min(jnp.where(el == v1, lanef, 1e9), axis=-1, keepdims=True)
    el2 = jnp.where(lanef == i1, NEG, el)
    v2 = jnp.max(el2, axis=-1, keepdims=True)
    i2 = jnp.min(jnp.where(el2 == v2, lanef, 1e9), axis=-1, keepdims=True)
    e21 = jnp.exp(v2 - v1)
    w1 = p_g / (1.0 + e21)
    w2 = w1 * e21

    l1 = i1 - EXPERTS_PER_GROUP * gsel
    l2 = i2 - EXPERTS_PER_GROUP * gsel
    code = jnp.minimum(l1, l2) * EXPERTS_PER_GROUP + jnp.maximum(l1, l2)
    pidx = jnp.full_like(code, float(PAIRS_PER_GROUP - 1))
    slot_a = jnp.full_like(code, float(PAIR_A[-1]))
    for p in range(PAIRS_PER_GROUP - 1):
        lo_e, hi_e = min(PAIR_A[p], PAIR_B[p]), max(PAIR_A[p], PAIR_B[p])
        hit = code == float(lo_e * EXPERTS_PER_GROUP + hi_e)
        pidx = jnp.where(hit, float(p), pidx)
        slot_a = jnp.where(hit, float(PAIR_A[p]), slot_a)
    first_is_a = l1 == slot_a
    w_a = jnp.where(first_is_a, w1, w2)
    w_b = jnp.where(first_is_a, w2, w1)
    bucket = gsel * PAIRS_PER_GROUP + pidx

    onehot = lanef == bucket
    rr = lax.broadcasted_iota(jnp.int32, (tm, tm), 0)
    cc = lax.broadcasted_iota(jnp.int32, (tm, tm), 1)
    tri = jnp.where(rr >= cc, 1.0, 0.0).astype(BF)
    prefix = _dot(tri, jnp.where(onehot, 1.0, 0.0).astype(BF))
    carry = carry_ref[...]
    rank = jnp.sum(jnp.where(onehot, prefix - 1.0 + carry, 0.0), axis=-1, keepdims=True)
    carry = carry + prefix[tm - 1:tm, :]
    carry_ref[...] = carry
    cnt_ref[...] = carry
    rt_ref[...] = jnp.where(lane == 0, w_a, jnp.where(lane == 1, w_b, jnp.where(
        lane == 2, bucket, jnp.where(lane == 3, rank, 0.0))))


def _outproj(layer, oa, ob, oc, u, x, mod_l, mod_row, tm, pw, counts_in):
    B, L, _ = x.shape
    nt = L // tm
    hb = tm // POOL_HALO
    nhb = L // POOL_HALO
    row = (lambda b: b) if mod_row is None else (lambda b: mod_row)

    def wspec(shape):
        nd = len(shape)
        return pl.BlockSpec((None,) + shape, lambda b, i: (layer,) + (0,) * nd)

    tok = lambda w: pl.BlockSpec((None, tm, w), lambda b, i: (b, i, 0))
    in_specs = [
        tok(256), tok(256), tok(256), tok(MIX_D),
        pl.BlockSpec((None, POOL_HALO, MIX_D), lambda b, i: (b, jnp.maximum(i * hb - 1, 0), 0)),
        pl.BlockSpec((None, POOL_HALO, MIX_D), lambda b, i: (b, jnp.minimum((i + 1) * hb, nhb - 1), 0)),
        tok(D_MODEL),
        pl.BlockSpec((None, N_MOD, D_MODEL), lambda b, i: (row(b), 0, 0)),
        wspec((4, D_MODEL)),
        wspec((D_MODEL, D_MODEL)),
        wspec((MIX_D, MIX_D)),
        wspec((1, MIX_D)),
        wspec((D_MODEL, 2 * LANES)),
        wspec((1, LANES)),
        pl.BlockSpec((1, LANES), lambda b, i: (0, 0)),
    ]
    return pl.pallas_call(
        functools.partial(_outproj_kernel, tm=tm, seq_len=L, nt=nt),
        out_shape=[jax.ShapeDtypeStruct((B, L, D_MODEL), F32),
                   jax.ShapeDtypeStruct((B * L,) + TOK_TILE, F32),
                   jax.ShapeDtypeStruct((B, L, LANES), F32),
                   jax.ShapeDtypeStruct((1, LANES), F32)],
        grid=(B, nt),
        in_specs=in_specs,
        out_specs=[tok(D_MODEL),
                   pl.BlockSpec((tm,) + TOK_TILE, lambda b, i: (b * nt + i, 0, 0)),
                   tok(LANES), pl.BlockSpec((1, LANES), lambda b, i: (0, 0))],
        scratch_shapes=[pltpu.VMEM((1, LANES), F32)],
        compiler_params=_params(("arbitrary", "arbitrary")),
        name="outproj",
    )(oa, ob, oc, u, u, u, x, mod_l, pw["norm_g"], pw["w_out"], pw["pool_bd"], pw["pool_scale"],
      pw["r_w"], pw["r_b"], counts_in)


def _route_tables(routes, counts, n_tiles):
    cnt = counts[0, :N_BUCKETS].astype(jnp.int32)
    tiles_b = (cnt + MOE_TM - 1) // MOE_TM
    ends = jnp.cumsum(tiles_b)
    starts = ends - tiles_b
    pos = jnp.concatenate([jnp.take(starts * MOE_TM, rt[:, 2].astype(jnp.int32)) + rt[:, 3].astype(jnp.int32)
                           for rt in routes])
    n_used = ends[-1]
    tile = jnp.arange(n_tiles, dtype=jnp.int32)
    src = jnp.minimum(tile, n_used - 1)
    tb = jnp.sum((src[:, None] >= ends[None, :]).astype(jnp.int32), axis=1)
    tb = jnp.minimum(tb, N_BUCKETS - 1)
    grp, pidx = tb // PAIRS_PER_GROUP, tb % PAIRS_PER_GROUP
    ea = grp * EXPERTS_PER_GROUP + jnp.take(jnp.array(PAIR_A, jnp.int32), pidx)
    eb = grp * EXPERTS_PER_GROUP + jnp.take(jnp.array(PAIR_B, jnp.int32), pidx)
    zlast = jnp.where(tiles_b > 0, (ends - 1) * MOE_TM, -1)
    idle = n_used + jnp.arange(N_BUCKETS, dtype=jnp.int32)
    zidle = jnp.where(idle < n_tiles, idle * MOE_TM, -1)
    pad_lo = jnp.concatenate([starts * MOE_TM + cnt, (n_used * MOE_TM)[None]])
    pad_hi = jnp.concatenate([ends * MOE_TM, jnp.full((1,), n_tiles * MOE_TM, jnp.int32)])
    zrow = jnp.concatenate([zlast, zidle, pad_lo, pad_hi]).astype(jnp.int32)
    return pos, zrow, src, ea, eb, n_used.reshape(1).astype(jnp.int32)


def _row_copy(src_ref, src_row, dst_ref, dst_row, sem):
    return pltpu.make_async_copy(src_ref.at[src_row], dst_ref.at[dst_row], sem)


def _scatter_kernel(pos_ref, zrow_ref, *refs, tm, steps):
    ng = len(steps)
    hs_ref, ws_ref, inv_ref, zbuf_ref, zw_ref, sem, zsem = refs[2 * ng:]
    i = pl.program_id(0)

    @pl.when(i == 0)
    def _():
        zbuf_ref[...] = jnp.zeros_like(zbuf_ref)
        zw_ref[...] = jnp.zeros_like(zw_ref)
        for start in (True, False):
            for b in range(2 * N_BUCKETS):
                @pl.when(zrow_ref[b] >= 0)
                def _():
                    z0 = pl.multiple_of(zrow_ref[b], MOE_TM)
                    for cp in (pltpu.make_async_copy(zbuf_ref, hs_ref.at[pl.ds(z0, MOE_TM)], zsem),
                               pltpu.make_async_copy(zw_ref, ws_ref.at[pl.ds(z0, MOE_TM)], zsem)):
                        cp.start() if start else cp.wait()

        def no_token(s, carry):
            inv_ref[s] = -1
            return carry

        for b in range(N_BUCKETS + 1):
            lax.fori_loop(zrow_ref[2 * N_BUCKETS + b], zrow_ref[3 * N_BUCKETS + 1 + b], no_token, 0)

    def invert(r):
        inv_ref[pos_ref[i * tm + r]] = i * tm + r

    first = 0
    for g in range(ng):
        hr_ref, rt_ref = refs[2 * g], refs[2 * g + 1]

        def copies(r, hr_ref=hr_ref, rt_ref=rt_ref):
            p = pos_ref[i * tm + r]
            return (_row_copy(hr_ref, r, hs_ref, p, sem),
                    pltpu.make_async_copy(rt_ref.at[pl.ds(r, 1)], ws_ref.at[pl.ds(p, 1)], sem))

        @pl.when((i >= first) & (i < first + steps[g]))
        def _():
            _row_dma_loop(tm, copies, invert)

        first += steps[g]


def _row_dma_loop(n, copies, on_issue=None):
    def issue(g, carry):
        for k in range(DMA_UNROLL):
            for cp in copies(g * DMA_UNROLL + k):
                cp.start(priority=k % 2)
            if on_issue is not None:
                on_issue(g * DMA_UNROLL + k)
        return carry

    lax.fori_loop(0, n // DMA_UNROLL, issue, 0)

    def drain(g, carry):
        for k in range(DMA_UNROLL):
            for cp in copies(g * DMA_UNROLL + k):
                cp.wait()
        return carry

    lax.fori_loop(0, n // DMA_UNROLL, drain, 0)


def _scatter_rows(hrs, routes, pos, zrow, n_tiles, tm):
    steps = [hr.shape[0] // tm for hr in hrs]
    in_specs, args, first = [], [], 0
    for hr, rt, n in zip(hrs, routes, steps):
        blk = lambda i, first=first, n=n: jnp.clip(i - first, 0, n - 1)
        in_specs += [pl.BlockSpec((tm,) + TOK_TILE, lambda i, pos, zrow, blk=blk: (blk(i), 0, 0)),
                     pl.BlockSpec((tm, LANES), lambda i, pos, zrow, blk=blk: (blk(i), 0))]
        args += [hr, rt]
        first += n
    return pl.pallas_call(
        functools.partial(_scatter_kernel, tm=tm, steps=tuple(steps)),
        out_shape=[jax.ShapeDtypeStruct((n_tiles * MOE_TM,) + TOK_TILE, F32),
                   jax.ShapeDtypeStruct((n_tiles * MOE_TM, LANES), F32),
                   jax.ShapeDtypeStruct((n_tiles * MOE_TM,), jnp.int32)],
        grid_spec=pltpu.PrefetchScalarGridSpec(
            num_scalar_prefetch=2,
            grid=(sum(steps),),
            in_specs=in_specs,
            out_specs=[pl.BlockSpec(memory_space=pl.ANY), pl.BlockSpec(memory_space=pl.ANY),
                       pl.BlockSpec(memory_space=pltpu.SMEM)],
            scratch_shapes=[pltpu.VMEM((MOE_TM,) + TOK_TILE, F32), pltpu.VMEM((MOE_TM, LANES), F32),
                            pltpu.SemaphoreType.DMA(()), pltpu.SemaphoreType.DMA(())]),
        compiler_params=_params(("arbitrary",)),
        name="moe_scatter",
    )(pos, zrow, *args)


def _tile_row_copies(j, inv_ref, obuf_ref, ys_ref, sem, n_tok, start):
    slot = j % 2
    for r in range(MOE_TM):
        tok = inv_ref[j * MOE_TM + r]
        dst = jnp.where(tok < 0, n_tok + slot * MOE_TM + r, tok)
        cp = _row_copy(obuf_ref.at[slot], r, ys_ref, dst, sem.at[slot])
        cp.start(priority=r % 2) if start else cp.wait()


def _moe_kernel(src_ref, ea_ref, eb_ref, nu_ref, inv_ref, hs_ref, ws_ref,
                wgua_ref, wgub_ref, wda_ref, wdb_ref, ys_ref, obuf_ref, sem, *, n_tok):
    j = pl.program_id(0)
    last = pl.num_programs(0) - 1
    copies = functools.partial(_tile_row_copies, inv_ref=inv_ref, obuf_ref=obuf_ref, ys_ref=ys_ref,
                               sem=sem, n_tok=n_tok)

    @pl.when(j == 0)
    def _():
        obuf_ref[...] = jnp.zeros_like(obuf_ref)
        for slot in range(2):
            cp = pltpu.make_async_copy(obuf_ref.at[slot], ys_ref.at[pl.ds(n_tok + slot * MOE_TM, MOE_TM)],
                                       sem.at[slot])
            cp.start()
            cp.wait()

    n_used = nu_ref[0]

    def experts():
        h = hs_ref[...].reshape(MOE_TM, D_MODEL).astype(BF)
        w = ws_ref[...]
        gua = _dot(h, wgua_ref[...])
        gub = _dot(h, wgub_ref[...])
        ga, gb = gua[:, :D_EXPERT], gub[:, :D_EXPERT]
        ya = _dot(((ga * jax.nn.sigmoid(ga)) * gua[:, D_EXPERT:]).astype(BF), wda_ref[...])
        yb = _dot(((gb * jax.nn.sigmoid(gb)) * gub[:, D_EXPERT:]).astype(BF), wdb_ref[...])
        obuf_ref[j % 2] = (w[:, 0:1] * ya + w[:, 1:2] * yb).reshape((MOE_TM,) + TOK_TILE)

    @pl.when((j >= 2) & (j - 2 < n_used))
    def _():
        copies(j - 2, start=False)

    @pl.when((j >= 1) & (j < n_used))
    def _():
        copies(j - 1, start=True)
        experts()

    @pl.when((j == 0) & (j < n_used))
    def _():
        experts()

    @pl.when((j >= 1) & (j == n_used))
    def _():
        copies(j - 1, start=True)

    @pl.when((j == last) & (j < n_used))
    def _():
        copies(j, start=True)

    @pl.when((j == last) & (j - 1 < n_used))
    def _():
        copies(j - 1, start=False)

    @pl.when((j == last) & (j < n_used))
    def _():
        copies(j, start=False)


def _moe_sorted(layer, n_tok, hs, ws, src, ea, eb, n_used, inv, pw):
    n_tiles = hs.shape[0] // MOE_TM
    assert n_tiles >= 2
    wgu = lambda sel: pl.BlockSpec((None, None, D_MODEL, 2 * D_EXPERT),
                                   lambda j, src, ea, eb, *_: (layer, (ea, eb)[sel][j], 0, 0))
    wd = lambda sel: pl.BlockSpec((None, None, D_EXPERT, D_MODEL),
                                  lambda j, src, ea, eb, *_: (layer, (ea, eb)[sel][j], 0, 0))
    return pl.pallas_call(
        functools.partial(_moe_kernel, n_tok=n_tok),
        out_shape=jax.ShapeDtypeStruct((n_tok + 2 * MOE_TM,) + TOK_TILE, F32),
        grid_spec=pltpu.PrefetchScalarGridSpec(
            num_scalar_prefetch=5,
            grid=(n_tiles,),
            in_specs=[pl.BlockSpec((MOE_TM,) + TOK_TILE, lambda j, src, *_: (src[j], 0, 0)),
                      pl.BlockSpec((MOE_TM, LANES), lambda j, src, *_: (src[j], 0)),
                      wgu(0), wgu(1), wd(0), wd(1)],
            out_specs=pl.BlockSpec(memory_space=pl.ANY),
            scratch_shapes=[pltpu.VMEM((2, MOE_TM) + TOK_TILE, F32), pltpu.SemaphoreType.DMA((2,))]),
        compiler_params=_params(("arbitrary",)),
        name="moe",
    )(src, ea, eb, n_used, inv, hs, ws, pw["w_gu"], pw["w_gu"], pw["w_down"], pw["w_down"])


def _residual_kernel(ys_ref, x_ref, mod_ref, ng_ref, o_ref, *, tm):
    fx = ys_ref[...].reshape(tm, D_MODEL)
    o_ref[...] = x_ref[...] + mod_ref[5:6, :] * _rms(fx, ng_ref[3:4, :])


def _moe_residual(layer, ys, first_row, x, mod_l, mod_row, tm, pw):
    B, L, _ = x.shape
    T = B * L
    tiles_per_seq = L // tm
    first_blk = first_row // tm
    row = (lambda i: i // tiles_per_seq) if mod_row is None else (lambda i: mod_row)
    out = pl.pallas_call(
        functools.partial(_residual_kernel, tm=tm),
        out_shape=jax.ShapeDtypeStruct((T, D_MODEL), F32),
        grid=(T // tm,),
        in_specs=[pl.BlockSpec((tm,) + TOK_TILE, lambda i: (first_blk + i, 0, 0)),
                  pl.BlockSpec((tm, D_MODEL), lambda i: (i, 0)),
                  pl.BlockSpec((None, N_MOD, D_MODEL), lambda i: (row(i), 0, 0)),
                  pl.BlockSpec((None, 4, D_MODEL), lambda i: (layer, 0, 0))],
        out_specs=pl.BlockSpec((tm, D_MODEL), lambda i: (i, 0)),
        compiler_params=_params(("arbitrary",)),
        name="moe_residual",
    )(ys, x.reshape(T, D_MODEL), mod_l, pw["norm_g"])
    return out.reshape(B, L, D_MODEL)


def _moe(layer, groups, counts, mod_l, tm, pw):
    hrs = [g[0] for g in groups]
    routes = [g[1].reshape(-1, LANES) for g in groups]
    T = sum(hr.shape[0] for hr in hrs)
    n_tiles = T // MOE_TM + N_BUCKETS
    pos, zrow, src, ea, eb, n_used = _route_tables(routes, counts, n_tiles)
    hs, ws, inv = _scatter_rows(hrs, routes, pos, zrow, n_tiles, tm)
    ys = _moe_sorted(layer, T, hs, ws, src, ea, eb, n_used, inv, pw)
    outs, first = [], 0
    for hr, _, x, mod_row, tm_res in groups:
        outs.append(_moe_residual(layer, ys, first, x, mod_l, mod_row, tm_res, pw))
        first += hr.shape[0]
    return outs


def _axial(length, dim):
    rows = length // GRID_W
    row = jnp.repeat(jnp.arange(rows), GRID_W).astype(F32)
    col = jnp.tile(jnp.arange(GRID_W), rows).astype(F32)
    n = dim // 4
    inv = ROPE_BASE ** (-jnp.arange(n, dtype=F32) / n)
    ang = jnp.concatenate([row[:, None] * inv, col[:, None] * inv], axis=-1)
    return jnp.cos(ang), jnp.sin(ang)


def _rope_tables(length, rotate):
    if rotate:
        cos_a, sin_a = _axial(length, MLA_ROPE)
        cos_b, sin_b = _axial(length, SWA_HEAD_DIM)
    else:
        cos_a, sin_a = jnp.ones((length, MLA_ROPE // 2), F32), jnp.zeros((length, MLA_ROPE // 2), F32)
        cos_b, sin_b = jnp.ones((length, SWA_HEAD_DIM // 2), F32), jnp.zeros((length, SWA_HEAD_DIM // 2), F32)
    one = jnp.ones((length, MLA_NOPE), F32)
    zero = jnp.zeros((length, MLA_NOPE), F32)
    sa, sb, sc = MLA_SCALE * LOG2E, SWA_SCALE * LOG2E, DIFF_SCALE * LOG2E
    c_qa = jnp.tile(jnp.concatenate([one, cos_a, cos_a], -1), (1, MLA_HEADS)) * sa
    s_qa = jnp.tile(jnp.concatenate([zero, -sin_a, sin_a], -1), (1, MLA_HEADS)) * sa
    c_b = jnp.concatenate([cos_b, cos_b], -1)
    s_b = jnp.concatenate([-sin_b, sin_b], -1)
    c_c = jnp.tile(jnp.concatenate([cos_a, cos_a], -1), (1, 2 * DIFF_HEADS)) * sc
    s_c = jnp.tile(jnp.concatenate([-sin_a, sin_a], -1), (1, 2 * DIFF_HEADS)) * sc
    tab_c = jnp.concatenate([c_qa, jnp.tile(c_b, (1, SWA_HEADS)) * sb, jnp.tile(c_b, (1, SWA_KV_HEADS)), c_c], -1)
    tab_s = jnp.concatenate([s_qa, jnp.tile(s_b, (1, SWA_HEADS)) * sb, jnp.tile(s_b, (1, SWA_KV_HEADS)), s_c], -1)
    m_qa = jnp.tile(jnp.concatenate([jnp.zeros((MLA_NOPE,)), jnp.ones((MLA_ROPE // 2,)),
                                     jnp.zeros((MLA_ROPE // 2,))]), MLA_HEADS)
    m_b = jnp.concatenate([jnp.ones((SWA_HEAD_DIM // 2,)), jnp.zeros((SWA_HEAD_DIM // 2,))])
    m_c = jnp.concatenate([jnp.ones((DIFF_QK // 2,)), jnp.zeros((DIFF_QK // 2,))])
    msk = jnp.concatenate([m_qa, jnp.tile(m_b, SWA_HEADS + SWA_KV_HEADS), jnp.tile(m_c, 2 * DIFF_HEADS)])
    return tab_c, tab_s, msk.astype(F32)[None, :], cos_a.T, sin_a.T


def _prep_weights(norm_g, w_in, mla_q_gain, mla_kv_gain, mla_w_uq, mla_w_ukv, diff_subln, pool_w,
                  pool_scale, w_out, router_gw, router_gb, router_ew, router_eb, exp_w_gu, exp_w_down):
    sizes = (MLA_Q_RANK, MLA_KV_RANK, MLA_ROPE, 256, 128, 128, 256, 256, 256, MIX_D)
    offs = [0]
    for s in sizes:
        offs.append(offs[-1] + s)
    cq, ckv, kr, bq, bk, bv, dq, dk, dv, u = [w_in[:, :, offs[j]:offs[j + 1]] for j in range(len(sizes))]
    w_r = jnp.concatenate([cq, bq, dq, dv, u, ckv, bk, bv], axis=-1).astype(BF)
    w_kt = jnp.swapaxes(jnp.concatenate([ckv, kr, dk], axis=-1), 1, 2).astype(BF)
    ukv = mla_w_ukv.reshape(DEPTH, MLA_KV_RANK, MLA_HEADS, MLA_NOPE + MLA_V)
    w_uk = ukv[..., :MLA_NOPE].reshape(DEPTH, MLA_KV_RANK, MLA_HEADS * MLA_NOPE)
    w_uv = ukv[..., MLA_NOPE:].reshape(DEPTH, MLA_KV_RANK, MLA_HEADS * MLA_V)
    eye = jnp.eye(len(POOL_WINDOWS), dtype=F32)
    pool_bd = jnp.einsum('lgcd,gh->lgchd', pool_w, eye).reshape(DEPTH, MIX_D, MIX_D)
    r_w = jnp.concatenate([router_ew, router_gw], axis=-1)
    r_w = jnp.pad(r_w, ((0, 0), (0, 0), (0, LANES - r_w.shape[-1])))
    r_hi = r_w.astype(BF)
    r_lo = (r_w - r_hi.astype(F32)).astype(BF)
    r_b = jnp.concatenate([router_eb, router_gb], axis=-1)
    r_b = jnp.pad(r_b, ((0, 0), (0, LANES - r_b.shape[-1])))[:, None, :]
    return dict(
        norm_g=norm_g, w_r=w_r, w_kt=w_kt,
        q_gain=mla_q_gain[:, None, :], kv_gain_r=mla_kv_gain[:, None, :], kv_gain_c=mla_kv_gain[:, :, None],
        w_uq=mla_w_uq.astype(BF), w_ukt=jnp.swapaxes(w_uk, 1, 2).astype(BF), w_uv=w_uv.astype(BF),
        subln=jnp.tile(diff_subln, (1, LANES // DIFF_V))[:, None, :],
        pool_bd=pool_bd.astype(BF), pool_scale=pool_scale[:, None, :], w_out=w_out.astype(BF),
        r_w=jnp.concatenate([r_hi, r_lo], axis=-1), r_b=r_b, w_gu=exp_w_gu.astype(BF), w_down=exp_w_down.astype(BF))


def kernel(x, c, ctx, c_ctx, ada_w, ada_b, norm_g, w_in, mla_q_gain, mla_kv_gain, mla_w_uq, mla_w_ukv,
           swa_sink, diff_lambda, diff_subln, pool_w, pool_scale, w_out, router_gw, router_gb,
           router_ew, router_eb, exp_w_gu, exp_w_down):
    B, L, D = x.shape
    C = ctx.shape[1]
    assert D == D_MODEL and B + 1 <= MOD_ROWS and L % 512 == 0 and C % 256 == 0
    pw = _prep_weights(norm_g, w_in, mla_q_gain, mla_kv_gain, mla_w_uq, mla_w_ukv, diff_subln, pool_w,
                       pool_scale, w_out, router_gw, router_gb, router_ew, router_eb, exp_w_gu, exp_w_down)
    cc = jnp.concatenate([c, c_ctx[None, :], jnp.zeros((MOD_ROWS - B - 1, D), F32)], axis=0)
    mod = jnp.swapaxes(_adaln(cc, ada_w, ada_b), 1, 2)
    tabs_x = _rope_tables(L, True)
    tabs_c = _rope_tables(C, False)
    tm_x, tm_c, tq, tq_x = 512, 256, 256, 512

    h_ctx = ctx
    for layer in range(DEPTH):
        update_ctx = layer < DEPTH - 1
        lam_init = 0.8 - 0.6 * math.exp(-0.3 * layer)
        mod_l = mod[layer]
        qa, qb, qc, u, va, vb, vc, kb, kta, ktc = _inproj(layer, x, mod_l, None, tm_x, pw, tabs_x)
        cqa, cqb, cqc, cu, cva, cvb, cvc, ckb, ckta, cktc = _inproj(layer, h_ctx, mod_l, B, tm_c, pw, tabs_c)
        diff_extra = (diff_lambda, pw["subln"])

        oa = _dense_attn("mla", layer, qa, [kta, ckta], [va, cva], tq_x)
        ob = _swa(layer, swa_sink, qb, kb, vb, ckb, cvb, tq)
        oc = _dense_attn("diff", layer, qc, [ktc, cktc], [vc, cvc], tq_x, diff_extra, lam_init)
        x, hrx, rtx, counts = _outproj(layer, oa, ob, oc, u, x, mod_l, None, tm_x, pw, jnp.zeros((1, LANES), F32))
        groups = [(hrx, rtx, x, None, tm_x)]
        if update_ctx:
            coa = _dense_attn("mla", layer, cqa, [ckta], [cva], tq)
            cob = _swa(layer, swa_sink, cqb, None, None, ckb, cvb, tq)
            coc = _dense_attn("diff", layer, cqc, [cktc], [cvc], tq, diff_extra, lam_init)
            h_ctx, hrc, rtc, counts = _outproj(layer, coa, cob, coc, cu, h_ctx, mod_l, B, tm_c, pw, counts)
            groups.append((hrc, rtc, h_ctx, B, tm_c))
        outs = _moe(layer, groups, counts, mod_l, tm_x, pw)
        x = outs[0]
        if update_ctx:
            h_ctx = outs[1]
    return x
```

```python
import functools
import math

import jax
import jax.numpy as jnp
from jax import lax
from jax.experimental import pallas as pl
from jax.experimental.pallas import tpu as pltpu

F32 = jnp.float32
BF = jnp.bfloat16

D_MODEL = 1024
DEPTH = 2
GRID_W = 64
EPS = 1e-6
ROPE_BASE = 10000.0
N_MOD = 6

MLA_HEADS = 4
MLA_Q_RANK = 256
MLA_KV_RANK = 128
MLA_NOPE = 64
MLA_ROPE = 32
MLA_V = 64
MLA_QK = MLA_NOPE + MLA_ROPE
MLA_SCALE = MLA_QK ** -0.5

SWA_HEADS = 4
SWA_KV_HEADS = 2
SWA_HEAD_DIM = 64
SWA_WINDOW = 128
SWA_SCALE = SWA_HEAD_DIM ** -0.5

DIFF_HEADS = 4
DIFF_QK = 32
DIFF_V = 64
DIFF_SCALE = DIFF_QK ** -0.5

POOL_WINDOWS = (2, 4, 8, 16)
POOL_GROUP = 64
POOL_HALO = 8
MIX_D = len(POOL_WINDOWS) * POOL_GROUP

N_EXPERT_GROUPS = 4
EXPERTS_PER_GROUP = 4
N_EXPERTS = 16
D_EXPERT = 512
PAIR_A = (0, 2, 2, 3, 3, 3)
PAIR_B = (1, 1, 0, 0, 1, 2)
PAIRS_PER_GROUP = len(PAIR_A)
N_BUCKETS = N_EXPERT_GROUPS * PAIRS_PER_GROUP

LOG2E = math.log2(math.e)
NEG = -1e30
LANES = 128
MOD_ROWS = 24
SUBLANES = 8
TOK_TILE = (D_MODEL // LANES, LANES)
MOE_TM = 256
DMA_UNROLL = 16
SCATTER_TM = 1024
KEY_CHUNK = 256
VMEM_LIMIT = 48 * 1024 * 1024

R_CQ, R_BQ, R_DQ, R_DV, R_U, R_CKV, R_BK, R_BV, R_END = 0, 256, 512, 768, 1024, 1280, 1408, 1536, 1664
T_CKV, T_KR, T_DK, T_END = 0, 128, 160, 416
P_QA, P_QB, P_KB, P_QC, P_END = 0, 384, 640, 768, 1024


def _dot(a, b):
    return jnp.dot(a, b, preferred_element_type=F32)


def _dot_nt(a, b):
    return lax.dot_general(a, b, (((1,), (1,)), ((), ())), preferred_element_type=F32)


def _rms(x, g):
    return x * lax.rsqrt(jnp.mean(x * x, axis=-1, keepdims=True) + EPS) * g


def _params(sem):
    return pltpu.CompilerParams(dimension_semantics=sem, vmem_limit_bytes=VMEM_LIMIT)


def _adaln_kernel(c_ref, w_ref, b_ref, o_ref):
    c = c_ref[...]
    s = (c * jax.nn.sigmoid(c)).astype(BF)
    o_ref[...] = _dot(s, w_ref[...].astype(BF)) + b_ref[...]


def _adaln(cc, ada_w, ada_b):
    return pl.pallas_call(
        _adaln_kernel,
        out_shape=jax.ShapeDtypeStruct((DEPTH, N_MOD, MOD_ROWS, D_MODEL), F32),
        grid=(DEPTH, N_MOD),
        in_specs=[
            pl.BlockSpec((MOD_ROWS, D_MODEL), lambda l, j: (0, 0)),
            pl.BlockSpec((None, D_MODEL, D_MODEL), lambda l, j: (l, 0, j)),
            pl.BlockSpec((None, None, 1, D_MODEL), lambda l, j: (l, j, 0, 0)),
        ],
        out_specs=pl.BlockSpec((None, None, MOD_ROWS, D_MODEL), lambda l, j: (l, j, 0, 0)),
        compiler_params=_params(("arbitrary", "arbitrary")),
        name="adaln",
    )(cc, ada_w, ada_b.reshape(DEPTH, N_MOD, 1, D_MODEL))


def _rope_rows(xf, c, s, m1, half):
    w = xf.shape[-1]
    up = pltpu.roll(xf, w - half, axis=1)
    dn = pltpu.roll(xf, half, axis=1)
    return xf * c + jnp.where(m1 > 0.5, up, dn) * s


def _rope_cols(xt, c, s):
    h = xt.shape[0] // 2
    x1, x2 = xt[:h], xt[h:]
    return jnp.concatenate([x1 * c - x2 * s, x1 * s + x2 * c], axis=0)


def _inproj_kernel(x_ref, mod_ref, ng_ref, wr_ref, wkt_ref, qg_ref, kvgr_ref, kvgc_ref,
                   wuq_ref, wukt_ref, wuv_ref, tc_ref, ts_ref, m_ref, ct_ref, st_ref,
                   qa_ref, qb_ref, qc_ref, u_ref, va_ref, vb_ref, vc_ref, kb_ref, kta_ref, ktc_ref):
    x = x_ref[...]
    mod = mod_ref[...]
    hx = _rms(x, ng_ref[0:1, :]) * (1.0 + mod[1:2, :]) + mod[0:1, :]
    hb = hx.astype(BF)
    zr = _dot(hb, wr_ref[...])
    zk = _dot_nt(wkt_ref[...], hb)

    tc = tc_ref[...]
    ts = ts_ref[...]
    m1 = m_ref[...]

    cqn = _rms(zr[:, R_CQ:R_CQ + MLA_Q_RANK], qg_ref[...]).astype(BF)
    qa = _dot(cqn, wuq_ref[...])
    qa_ref[...] = _rope_rows(qa, tc[:, P_QA:P_QB], ts[:, P_QA:P_QB], m1[:, P_QA:P_QB],
                             MLA_ROPE // 2).astype(BF)
    qb_ref[...] = _rope_rows(zr[:, R_BQ:R_DQ], tc[:, P_QB:P_KB], ts[:, P_QB:P_KB], m1[:, P_QB:P_KB],
                             SWA_HEAD_DIM // 2).astype(BF)
    qc_ref[...] = _rope_rows(zr[:, R_DQ:R_DV], tc[:, P_QC:P_END], ts[:, P_QC:P_END], m1[:, P_QC:P_END],
                             DIFF_QK // 2).astype(BF)
    kb_ref[...] = _rope_rows(zr[:, R_BK:R_BV], tc[:, P_KB:P_QC], ts[:, P_KB:P_QC], m1[:, P_KB:P_QC],
                             SWA_HEAD_DIM // 2).astype(BF)
    u_ref[...] = zr[:, R_U:R_CKV]
    vb_ref[...] = zr[:, R_BV:R_END].astype(BF)
    vc_ref[...] = zr[:, R_DV:R_U].astype(BF)

    ckvn = _rms(zr[:, R_CKV:R_BK], kvgr_ref[...]).astype(BF)
    va_ref[...] = _dot(ckvn, wuv_ref[...]).astype(BF)

    ckvt = zk[T_CKV:T_KR]
    ckvtn = ckvt * lax.rsqrt(jnp.mean(ckvt * ckvt, axis=0, keepdims=True) + EPS) * kvgc_ref[...]
    knt = _dot(wukt_ref[...], ckvtn.astype(BF))
    ct = ct_ref[...]
    st = st_ref[...]
    krt = _rope_cols(zk[T_KR:T_DK], ct, st)
    pieces = []
    for h in range(MLA_HEADS):
        pieces.append(knt[h * MLA_NOPE:(h + 1) * MLA_NOPE])
        pieces.append(krt)
    kta_ref[...] = jnp.concatenate(pieces, axis=0).astype(BF)

    dk = zk[T_DK:T_END]
    ktc_ref[...] = jnp.concatenate(
        [_rope_cols(dk[j * DIFF_QK:(j + 1) * DIFF_QK], ct, st) for j in range(2 * DIFF_HEADS)],
        axis=0).astype(BF)


def _inproj(layer, x, mod_l, mod_row, tm, pw, tabs):
    B, L, _ = x.shape
    nt = L // tm
    tab_c, tab_s, msk, cos_t, sin_t = tabs
    row = (lambda b: b) if mod_row is None else (lambda b: mod_row)

    def wspec(shape):
        nd = len(shape)
        return pl.BlockSpec((None,) + shape, lambda i, b: (layer,) + (0,) * nd)

    tok = lambda w: pl.BlockSpec((None, tm, w), lambda i, b: (b, i, 0))
    tokt = lambda w: pl.BlockSpec((None, w, tm), lambda i, b: (b, 0, i))
    in_specs = [
        tok(D_MODEL),
        pl.BlockSpec((None, N_MOD, D_MODEL), lambda i, b: (row(b), 0, 0)),
        wspec((4, D_MODEL)),
        wspec((D_MODEL, R_END)),
        wspec((T_END, D_MODEL)),
        wspec((1, MLA_Q_RANK)),
        wspec((1, MLA_KV_RANK)),
        wspec((MLA_KV_RANK, 1)),
        wspec((MLA_Q_RANK, MLA_HEADS * MLA_QK)),
        wspec((MLA_HEADS * MLA_NOPE, MLA_KV_RANK)),
        wspec((MLA_KV_RANK, MLA_HEADS * MLA_V)),
        pl.BlockSpec((tm, P_END), lambda i, b: (i, 0)),
        pl.BlockSpec((tm, P_END), lambda i, b: (i, 0)),
        pl.BlockSpec((1, P_END), lambda i, b: (0, 0)),
        pl.BlockSpec((MLA_ROPE // 2, tm), lambda i, b: (0, i)),
        pl.BlockSpec((MLA_ROPE // 2, tm), lambda i, b: (0, i)),
    ]
    widths = [(MLA_HEADS * MLA_QK, BF), (256, BF), (256, BF), (MIX_D, F32),
              (256, BF), (128, BF), (256, BF), (128, BF)]
    out_shape = [jax.ShapeDtypeStruct((B, L, w), dt) for w, dt in widths]
    out_specs = [tok(w) for w, _ in widths]
    out_shape += [jax.ShapeDtypeStruct((B, MLA_HEADS * MLA_QK, L), BF),
                  jax.ShapeDtypeStruct((B, 2 * DIFF_HEADS * DIFF_QK, L), BF)]
    out_specs += [tokt(MLA_HEADS * MLA_QK), tokt(2 * DIFF_HEADS * DIFF_QK)]
    return pl.pallas_call(
        _inproj_kernel,
        out_shape=out_shape,
        grid=(nt, B),
        in_specs=in_specs,
        out_specs=out_specs,
        compiler_params=_params(("arbitrary", "arbitrary")),
        name="inproj",
    )(x, mod_l, pw["norm_g"], pw["w_r"], pw["w_kt"], pw["q_gain"], pw["kv_gain_r"], pw["kv_gain_c"],
      pw["w_uq"], pw["w_ukt"], pw["w_uv"], tab_c, tab_s, msk, cos_t, sin_t)


def _attn_heads(heads, sizes):
    chunks = [(seg, c) for seg, n in enumerate(sizes) for c in range(0, n, KEY_CHUNK)]
    outs = []
    prev = None
    for step in range(len(heads) + 1):
        cur = heads[step] if step < len(heads) else None
        cur_s, mc = [], None
        if prev is not None:
            prev_s, prev_m, prev_vs, prev_x = prev
            lacc, r = None, None
        for idx, (seg, c) in enumerate(chunks):
            if cur is not None:
                s = cur[0](seg, c)
                cur_s.append(s)
                mc = s if mc is None else jnp.maximum(mc, s)
            if prev is not None:
                e = jnp.exp2(prev_s[idx] - prev_m)
                rs = _dot(e.astype(BF), prev_vs[seg][c:c + KEY_CHUNK, :])
                lacc = e if lacc is None else lacc + e
                r = rs if r is None else r + rs
        if prev is not None:
            l = jnp.sum(lacc, axis=-1, keepdims=True)
            if prev_x is not None:
                l = l + jnp.exp2(prev_x - prev_m)
            outs.append(r / l)
        if cur is not None:
            m = jnp.max(mc, axis=-1, keepdims=True)
            if cur[2] is not None:
                m = jnp.maximum(m, cur[2])
            prev = (cur_s, m, cur[1], cur[2])
        else:
            prev = None
    return outs


def _kt_scores(qh, kt_refs, sl, seg, c):
    return _dot(qh, kt_refs[seg][sl, c:c + KEY_CHUNK])


def _mla_kernel(*refs, nseg):
    q_ref = refs[0]
    kt_refs = refs[1:1 + nseg]
    v_refs = refs[1 + nseg:1 + 2 * nseg]
    o_ref = refs[1 + 2 * nseg]
    q = q_ref[...]
    lane = lax.broadcasted_iota(jnp.int32, (1, LANES), 1)
    heads = []
    for h in range(MLA_HEADS):
        sl = slice(h * MLA_QK, (h + 1) * MLA_QK)
        vs = [v[:, (h // 2) * LANES:(h // 2 + 1) * LANES] for v in v_refs]
        heads.append((functools.partial(_kt_scores, q[:, sl], kt_refs, sl), vs, None))
    outs = _attn_heads(heads, [kt.shape[1] for kt in kt_refs])
    blocks = [jnp.where(lane < MLA_V, outs[2 * p], outs[2 * p + 1]) for p in range(MLA_HEADS // 2)]
    o_ref[...] = jnp.concatenate(blocks, axis=-1).astype(BF)


def _diff_kernel(*refs, nseg, lam_init):
    q_ref = refs[0]
    kt_refs = refs[1:1 + nseg]
    v_refs = refs[1 + nseg:1 + 2 * nseg]
    dl_ref, g_ref, o_ref = refs[1 + 2 * nseg:]
    q = q_ref[...]
    dl = dl_ref[...]
    lam = (jnp.exp(jnp.sum(dl[0:1] * dl[1:2], axis=-1, keepdims=True))
           - jnp.exp(jnp.sum(dl[2:3] * dl[3:4], axis=-1, keepdims=True)) + lam_init)
    lane = lax.broadcasted_iota(jnp.int32, (1, LANES), 1)
    lo = lane < DIFF_V
    heads = []
    for j in range(2 * DIFF_HEADS):
        sl = slice(j * DIFF_QK, (j + 1) * DIFF_QK)
        vs = [v[:, (j // 4) * LANES:(j // 4 + 1) * LANES] for v in v_refs]
        heads.append((functools.partial(_kt_scores, q[:, sl], kt_refs, sl), vs, None))
    comps = _attn_heads(heads, [kt.shape[1] for kt in kt_refs])
    blocks = []
    for p in range(DIFF_HEADS // 2):
        pair = [comps[2 * h] - lam * comps[2 * h + 1] for h in (2 * p, 2 * p + 1)]
        o = jnp.where(lo, pair[0], pair[1])
        o2 = o * o
        ms_lo = jnp.sum(jnp.where(lo, o2, 0.0), axis=-1, keepdims=True) * (1.0 / DIFF_V)
        ms_hi = jnp.sum(jnp.where(lo, 0.0, o2), axis=-1, keepdims=True) * (1.0 / DIFF_V)
        ms = jnp.where(lo, ms_lo, ms_hi)
        blocks.append(o * lax.rsqrt(ms + EPS) * g_ref[...] * (1.0 - lam_init))
    o_ref[...] = jnp.concatenate(blocks, axis=-1).astype(BF)


def _dense_attn(kind, layer, q, kts, vs, tq, extra_in=(), lam_init=0.0):
    B, L, wq = q.shape
    nseg = len(kts)
    in_specs = [pl.BlockSpec((None, tq, wq), lambda b, i: (b, i, 0))]
    for kt in kts:
        in_specs.append(pl.BlockSpec((None,) + kt.shape[1:], lambda b, i: (b, 0, 0)))
    for v in vs:
        in_specs.append(pl.BlockSpec((None,) + v.shape[1:], lambda b, i: (b, 0, 0)))
    if kind == "mla":
        body = functools.partial(_mla_kernel, nseg=nseg)
    else:
        body = functools.partial(_diff_kernel, nseg=nseg, lam_init=lam_init)
        dl, g = extra_in
        in_specs.append(pl.BlockSpec((None, 4, DIFF_QK), lambda b, i: (layer, 0, 0)))
        in_specs.append(pl.BlockSpec((None, 1, LANES), lambda b, i: (layer, 0, 0)))
    return pl.pallas_call(
        body,
        out_shape=jax.ShapeDtypeStruct((B, L, 256), BF),
        grid=(B, L // tq),
        in_specs=in_specs,
        out_specs=pl.BlockSpec((None, tq, 256), lambda b, i: (b, i, 0)),
        compiler_params=_params(("arbitrary", "arbitrary")),
        name=kind,
    )(q, *kts, *vs, *extra_in)


def _swa_kernel(*refs, layer, tq, seq_len, local):
    if local:
        sink_ref, q_ref, kl_ref, vl_ref, kc_ref, vc_ref, o_ref = refs
    else:
        sink_ref, q_ref, kc_ref, vc_ref, o_ref = refs
    i = pl.program_id(1)
    q = q_ref[...]
    lane = lax.broadcasted_iota(jnp.int32, (1, LANES), 1)
    rows = lax.broadcasted_iota(jnp.int32, (2 * tq, 1), 0)
    top = rows < tq
    if local:
        win = tq + 2 * SWA_WINDOW
        q0 = i * tq
        ws = pl.multiple_of(jnp.clip(q0 - SWA_WINDOW, 0, seq_len - win), LANES)
        kwin = kl_ref[pl.ds(ws, win), :]
        vwin = vl_ref[pl.ds(ws, win), :]
        qpos = q0 + jnp.where(top, rows, rows - tq)
        kpos = ws + lax.broadcasted_iota(jnp.int32, (1, win), 1)
        band = jnp.abs(kpos - qpos) <= SWA_WINDOW
    kc = kc_ref[...]
    vs = [vc_ref[...]] + ([vwin] if local else [])
    sizes = [kc.shape[0]] + ([win] if local else [])

    def scores(q2, sl, seg, c):
        if seg == 0:
            return _dot_nt(q2, kc[c:c + KEY_CHUNK, sl])
        return jnp.where(band[:, c:c + KEY_CHUNK], _dot_nt(q2, kwin[c:c + KEY_CHUNK, sl]), NEG)

    heads = []
    for k in range(SWA_KV_HEADS):
        q2 = jnp.concatenate([q[:, (2 * k) * SWA_HEAD_DIM:(2 * k + 1) * SWA_HEAD_DIM],
                              q[:, (2 * k + 1) * SWA_HEAD_DIM:(2 * k + 2) * SWA_HEAD_DIM]], axis=0)
        sl = slice(k * SWA_HEAD_DIM, (k + 1) * SWA_HEAD_DIM)
        sk = jnp.where(top, sink_ref[layer, 2 * k], sink_ref[layer, 2 * k + 1]) * LOG2E
        heads.append((functools.partial(scores, q2, sl), vs, sk))
    outs = _attn_heads(heads, sizes)
    blocks = []
    for k, r in enumerate(outs):
        r_top, r_bot = r[:tq], r[tq:]
        if k == 0:
            r_bot = pltpu.roll(r_bot, SWA_HEAD_DIM, axis=1)
        else:
            r_top = pltpu.roll(r_top, SWA_HEAD_DIM, axis=1)
        blocks.append(jnp.where(lane < SWA_HEAD_DIM, r_top, r_bot))
    o_ref[...] = jnp.concatenate(blocks, axis=-1).astype(BF)


def _swa(layer, sink, q, k_loc, v_loc, k_ctx, v_ctx, tq):
    B, L, _ = q.shape
    C = k_ctx.shape[1]
    local = k_loc is not None
    full = lambda n: pl.BlockSpec((None, n, 128), lambda b, i: (b, 0, 0))
    in_specs = [pl.BlockSpec(memory_space=pltpu.SMEM),
                pl.BlockSpec((None, tq, 256), lambda b, i: (b, i, 0))]
    args = [sink, q]
    if local:
        in_specs += [full(L), full(L)]
        args += [k_loc, v_loc]
    in_specs += [full(C), full(C)]
    args += [k_ctx, v_ctx]
    return pl.pallas_call(
        functools.partial(_swa_kernel, layer=layer, tq=tq, seq_len=L, local=local),
        out_shape=jax.ShapeDtypeStruct((B, L, 256), BF),
        grid=(B, L // tq),
        in_specs=in_specs,
        out_specs=pl.BlockSpec((None, tq, 256), lambda b, i: (b, i, 0)),
        compiler_params=_params(("arbitrary", "arbitrary")),
        name="swa",
    )(*args)


def _pool_diffs(u, prev, nxt, i, nt, tm, seq_len):
    n = tm + 2 * POOL_HALO
    ue = jnp.concatenate([jnp.where(i > 0, prev, 0.0), u, jnp.where(i < nt - 1, nxt, 0.0)], axis=0)

    def sh(a, k):
        return pltpu.roll(a, k % n, axis=0)

    s2 = ue + sh(ue, 1)
    s4 = sh(s2, 1) + sh(s2, -1)
    s8 = sh(s4, 2) + sh(s4, -2)
    s16 = sh(s8, 4) + sh(s8, -4)
    lane = lax.broadcasted_iota(jnp.int32, (1, MIX_D), 1)
    t = i * tm + lax.broadcasted_iota(jnp.int32, (tm, 1), 0)
    win = None
    cnt = None
    for gi, (w, s) in enumerate(zip(POOL_WINDOWS, (s2, s4, s8, s16))):
        half = w // 2
        c = (jnp.minimum(t - half + w, seq_len) - jnp.maximum(t - half, 0)).astype(F32)
        sv = s[POOL_HALO:POOL_HALO + tm]
        if win is None:
            win, cnt = sv, c
        else:
            sel = lane >= gi * POOL_GROUP
            win = jnp.where(sel, sv, win)
            cnt = jnp.where(sel, c, cnt)
    return win / cnt - u


def _outproj_kernel(oa_ref, ob_ref, oc_ref, u_ref, up_ref, un_ref, x_ref, mod_ref, ng_ref,
                    wo_ref, pw_ref, ps_ref, rw_ref, rb_ref,
                    cin_ref, xo_ref, hr_ref, rt_ref, cnt_ref, carry_ref, *, tm, seq_len, nt):
    i = pl.program_id(1)

    @pl.when((pl.program_id(0) == 0) & (i == 0))
    def _():
        carry_ref[...] = cin_ref[...]

    u = u_ref[...]
    dg = _pool_diffs(u, up_ref[...], un_ref[...], i, nt, tm, seq_len)
    od = _dot(dg.astype(BF), pw_ref[...]) * ps_ref[...]
    mix = jnp.concatenate([oa_ref[...], ob_ref[...], oc_ref[...], od.astype(BF)], axis=-1)
    ox = _dot(mix, wo_ref[...])
    mod = mod_ref[...]
    x = x_ref[...] + mod[2:3, :] * _rms(ox, ng_ref[1:2, :])
    xo_ref[...] = x
    h = _rms(x, ng_ref[2:3, :]) * (1.0 + mod[4:5, :]) + mod[3:4, :]
    hr_ref[...] = h.reshape((tm,) + TOK_TILE)
    hh = h.astype(BF)
    hl = (h - hh.astype(F32)).astype(BF)
    hw = _dot(hh, rw_ref[...])
    logits = hw[:, :LANES] + hw[:, LANES:] + _dot(hl, rw_ref[:, :LANES]) + rb_ref[...]

    lane = lax.broadcasted_iota(jnp.int32, (1, LANES), 1)
    lanef = lane.astype(F32)
    is_g = (lane >= N_EXPERTS) & (lane < N_EXPERTS + N_EXPERT_GROUPS)
    gl = jnp.where(is_g, logits, NEG)
    gmax = jnp.max(gl, axis=-1, keepdims=True)
    gsel = jnp.min(jnp.where(gl == gmax, lanef, 1e9), axis=-1, keepdims=True) - N_EXPERTS
    p_g = 1.0 / jnp.sum(jnp.where(is_g, jnp.exp(gl - gmax), 0.0), axis=-1, keepdims=True)
    grp = jnp.right_shift(lane, EXPERTS_PER_GROUP.bit_length() - 1).astype(F32)
    el = jnp.where((lane < N_EXPERTS) & (grp == gsel), logits, NEG)
    v1 = jnp.max(el, axis=-1, keepdims=True)
    i1 = jnp.min(jnp.where(el == v1, lanef, 1e9), axis=-1, keepdims=True)
    el2 = jnp.where(lanef == i1, NEG, el)
    v2 = jnp.max(el2, axis=-1, keepdims=True)
    i2 = jnp.min(jnp.where(el2 == v2, lanef, 1e9), axis=-1, keepdims=True)
    e21 = jnp.exp(v2 - v1)
    w1 = p_g / (1.0 + e21)
    w2 = w1 * e21

    l1 = i1 - EXPERTS_PER_GROUP * gsel
    l2 = i2 - EXPERTS_PER_GROUP * gsel
    code = jnp.minimum(l1, l2) * EXPERTS_PER_GROUP + jnp.maximum(l1, l2)
    pidx = jnp.full_like(code, float(PAIRS_PER_GROUP - 1))
    slot_a = jnp.full_like(code, float(PAIR_A[-1]))
    for p in range(PAIRS_PER_GROUP - 1):
        lo_e, hi_e = min(PAIR_A[p], PAIR_B[p]), max(PAIR_A[p], PAIR_B[p])
        hit = code == float(lo_e * EXPERTS_PER_GROUP + hi_e)
        pidx = jnp.where(hit, float(p), pidx)
        slot_a = jnp.where(hit, float(PAIR_A[p]), slot_a)
    first_is_a = l1 == slot_a
    w_a = jnp.where(first_is_a, w1, w2)
    w_b = jnp.where(first_is_a, w2, w1)
    bucket = gsel * PAIRS_PER_GROUP + pidx

    onehot = lanef == bucket
    rr = lax.broadcasted_iota(jnp.int32, (tm, tm), 0)
    cc = lax.broadcasted_iota(jnp.int32, (tm, tm), 1)
    tri = jnp.where(rr >= cc, 1.0, 0.0).astype(BF)
    prefix = _dot(tri, jnp.where(onehot, 1.0, 0.0).astype(BF))
    carry = carry_ref[...]
    rank = jnp.sum(jnp.where(onehot, prefix - 1.0 + carry, 0.0), axis=-1, keepdims=True)
    carry = carry + prefix[tm - 1:tm, :]
    carry_ref[...] = carry
    cnt_ref[...] = carry
    rt_ref[...] = jnp.where(lane == 0, w_a, jnp.where(lane == 1, w_b, jnp.where(
        lane == 2, bucket, jnp.where(lane == 3, rank, 0.0))))


def _outproj(layer, oa, ob, oc, u, x, mod_l, mod_row, tm, pw, counts_in):
    B, L, _ = x.shape
    nt = L // tm
    hb = tm // POOL_HALO
    nhb = L // POOL_HALO
    row = (lambda b: b) if mod_row is None else (lambda b: mod_row)

    def wspec(shape):
        nd = len(shape)
        return pl.BlockSpec((None,) + shape, lambda b, i: (layer,) + (0,) * nd)

    tok = lambda w: pl.BlockSpec((None, tm, w), lambda b, i: (b, i, 0))
    in_specs = [
        tok(256), tok(256), tok(256), tok(MIX_D),
        pl.BlockSpec((None, POOL_HALO, MIX_D), lambda b, i: (b, jnp.maximum(i * hb - 1, 0), 0)),
        pl.BlockSpec((None, POOL_HALO, MIX_D), lambda b, i: (b, jnp.minimum((i + 1) * hb, nhb - 1), 0)),
        tok(D_MODEL),
        pl.BlockSpec((None, N_MOD, D_MODEL), lambda b, i: (row(b), 0, 0)),
        wspec((4, D_MODEL)),
        wspec((D_MODEL, D_MODEL)),
        wspec((MIX_D, MIX_D)),
        wspec((1, MIX_D)),
        wspec((D_MODEL, 2 * LANES)),
        wspec((1, LANES)),
        pl.BlockSpec((1, LANES), lambda b, i: (0, 0)),
    ]
    return pl.pallas_call(
        functools.partial(_outproj_kernel, tm=tm, seq_len=L, nt=nt),
        out_shape=[jax.ShapeDtypeStruct((B, L, D_MODEL), F32),
                   jax.ShapeDtypeStruct((B * L,) + TOK_TILE, F32),
                   jax.ShapeDtypeStruct((B, L, LANES), F32),
                   jax.ShapeDtypeStruct((1, LANES), F32)],
        grid=(B, nt),
        in_specs=in_specs,
        out_specs=[tok(D_MODEL),
                   pl.BlockSpec((tm,) + TOK_TILE, lambda b, i: (b * nt + i, 0, 0)),
                   tok(LANES), pl.BlockSpec((1, LANES), lambda b, i: (0, 0))],
        scratch_shapes=[pltpu.VMEM((1, LANES), F32)],
        compiler_params=_params(("arbitrary", "arbitrary")),
        name="outproj",
    )(oa, ob, oc, u, u, u, x, mod_l, pw["norm_g"], pw["w_out"], pw["pool_bd"], pw["pool_scale"],
      pw["r_w"], pw["r_b"], counts_in)


def _route_tables(routes, counts, n_tiles):
    cnt = counts[0, :N_BUCKETS].astype(jnp.int32)
    tiles_b = (cnt + MOE_TM - 1) // MOE_TM
    ends = jnp.cumsum(tiles_b)
    starts = ends - tiles_b
    pos = jnp.concatenate([jnp.take(starts * MOE_TM, rt[:, 2].astype(jnp.int32)) + rt[:, 3].astype(jnp.int32)
                           for rt in routes])
    n_used = ends[-1]
    tile = jnp.arange(n_tiles, dtype=jnp.int32)
    src = jnp.minimum(tile, n_used - 1)
    tb = jnp.sum((src[:, None] >= ends[None, :]).astype(jnp.int32), axis=1)
    tb = jnp.minimum(tb, N_BUCKETS - 1)
    grp, pidx = tb // PAIRS_PER_GROUP, tb % PAIRS_PER_GROUP
    ea = grp * EXPERTS_PER_GROUP + jnp.take(jnp.array(PAIR_A, jnp.int32), pidx)
    eb = grp * EXPERTS_PER_GROUP + jnp.take(jnp.array(PAIR_B, jnp.int32), pidx)
    zlast = jnp.where(tiles_b > 0, (ends - 1) * MOE_TM, -1)
    idle = n_used + jnp.arange(N_BUCKETS, dtype=jnp.int32)
    zidle = jnp.where(idle < n_tiles, idle * MOE_TM, -1)
    pad_lo = jnp.concatenate([starts * MOE_TM + cnt, (n_used * MOE_TM)[None]])
    pad_hi = jnp.concatenate([ends * MOE_TM, jnp.full((1,), n_tiles * MOE_TM, jnp.int32)])
    zrow = jnp.concatenate([zlast, zidle, pad_lo, pad_hi]).astype(jnp.int32)
    return pos, zrow, src, ea, eb, n_used.reshape(1).astype(jnp.int32)


def _row_copy(src_ref, src_row, dst_ref, dst_row, sem):
    return pltpu.make_async_copy(src_ref.at[src_row], dst_ref.at[dst_row], sem)


def _scatter_kernel(pos_ref, zrow_ref, *refs, tm, steps):
    ng = len(steps)
    hs_ref, ws_ref, inv_ref, zbuf_ref, zw_ref, sem, zsem = refs[2 * ng:]
    i = pl.program_id(0)

    @pl.when(i == 0)
    def _():
        zbuf_ref[...] = jnp.zeros_like(zbuf_ref)
        zw_ref[...] = jnp.zeros_like(zw_ref)
        for start in (True, False):
            for b in range(2 * N_BUCKETS):
                @pl.when(zrow_ref[b] >= 0)
                def _():
                    z0 = pl.multiple_of(zrow_ref[b], MOE_TM)
                    for cp in (pltpu.make_async_copy(zbuf_ref, hs_ref.at[pl.ds(z0, MOE_TM)], zsem),
                               pltpu.make_async_copy(zw_ref, ws_ref.at[pl.ds(z0, MOE_TM)], zsem)):
                        cp.start() if start else cp.wait()

        def no_token(s, carry):
            inv_ref[s] = -1
            return carry

        for b in range(N_BUCKETS + 1):
            lax.fori_loop(zrow_ref[2 * N_BUCKETS + b], zrow_ref[3 * N_BUCKETS + 1 + b], no_token, 0)

    def invert(r):
        inv_ref[pos_ref[i * tm + r]] = i * tm + r

    first = 0
    for g in range(ng):
        hr_ref, rt_ref = refs[2 * g], refs[2 * g + 1]

        def copies(r, hr_ref=hr_ref, rt_ref=rt_ref):
            p = pos_ref[i * tm + r]
            return (_row_copy(hr_ref, r, hs_ref, p, sem),
                    pltpu.make_async_copy(rt_ref.at[pl.ds(r, 1)], ws_ref.at[pl.ds(p, 1)], sem))

        @pl.when((i >= first) & (i < first + steps[g]))
        def _():
            _row_dma_loop(tm, copies, invert)

        first += steps[g]


def _row_dma_loop(n, copies, on_issue=None):
    def issue(g, carry):
        for k in range(DMA_UNROLL):
            for cp in copies(g * DMA_UNROLL + k):
                cp.start(priority=k % 2)
            if on_issue is not None:
                on_issue(g * DMA_UNROLL + k)
        return carry

    lax.fori_loop(0, n // DMA_UNROLL, issue, 0)

    def drain(g, carry):
        for k in range(DMA_UNROLL):
            for cp in copies(g * DMA_UNROLL + k):
                cp.wait()
        return carry

    lax.fori_loop(0, n // DMA_UNROLL, drain, 0)


def _scatter_rows(hrs, routes, pos, zrow, n_tiles, tm):
    steps = [hr.shape[0] // tm for hr in hrs]
    in_specs, args, first = [], [], 0
    for hr, rt, n in zip(hrs, routes, steps):
        blk = lambda i, first=first, n=n: jnp.clip(i - first, 0, n - 1)
        in_specs += [pl.BlockSpec((tm,) + TOK_TILE, lambda i, pos, zrow, blk=blk: (blk(i), 0, 0)),
                     pl.BlockSpec((tm, LANES), lambda i, pos, zrow, blk=blk: (blk(i), 0))]
        args += [hr, rt]
        first += n
    return pl.pallas_call(
        functools.partial(_scatter_kernel, tm=tm, steps=tuple(steps)),
        out_shape=[jax.ShapeDtypeStruct((n_tiles * MOE_TM,) + TOK_TILE, F32),
                   jax.ShapeDtypeStruct((n_tiles * MOE_TM, LANES), F32),
                   jax.ShapeDtypeStruct((n_tiles * MOE_TM,), jnp.int32)],
        grid_spec=pltpu.PrefetchScalarGridSpec(
            num_scalar_prefetch=2,
            grid=(sum(steps),),
            in_specs=in_specs,
            out_specs=[pl.BlockSpec(memory_space=pl.ANY), pl.BlockSpec(memory_space=pl.ANY),
                       pl.BlockSpec(memory_space=pltpu.SMEM)],
            scratch_shapes=[pltpu.VMEM((MOE_TM,) + TOK_TILE, F32), pltpu.VMEM((MOE_TM, LANES), F32),
                            pltpu.SemaphoreType.DMA(()), pltpu.SemaphoreType.DMA(())]),
        compiler_params=_params(("arbitrary",)),
        name="moe_scatter",
    )(pos, zrow, *args)


def _tile_row_copies(j, inv_ref, obuf_ref, ys_ref, sem, n_tok, start):
    slot = j % 2
    for r in range(MOE_TM):
        tok = inv_ref[j * MOE_TM + r]
        dst = jnp.where(tok < 0, n_tok + slot * MOE_TM + r, tok)
        cp = _row_copy(obuf_ref.at[slot], r, ys_ref, dst, sem.at[slot])
        cp.start(priority=r % 2) if start else cp.wait()


def _moe_kernel(src_ref, ea_ref, eb_ref, nu_ref, inv_ref, hs_ref, ws_ref,
                wgua_ref, wgub_ref, wda_ref, wdb_ref, ys_ref, obuf_ref, sem, *, n_tok):
    j = pl.program_id(0)
    last = pl.num_programs(0) - 1
    copies = functools.partial(_tile_row_copies, inv_ref=inv_ref, obuf_ref=obuf_ref, ys_ref=ys_ref,
                               sem=sem, n_tok=n_tok)

    @pl.when(j == 0)
    def _():
        obuf_ref[...] = jnp.zeros_like(obuf_ref)
        for slot in range(2):
            cp = pltpu.make_async_copy(obuf_ref.at[slot], ys_ref.at[pl.ds(n_tok + slot * MOE_TM, MOE_TM)],
                                       sem.at[slot])
            cp.start()
            cp.wait()

    n_used = nu_ref[0]

    def experts():
        h = hs_ref[...].reshape(MOE_TM, D_MODEL).astype(BF)
        w = ws_ref[...]
        gua = _dot(h, wgua_ref[...])
        gub = _dot(h, wgub_ref[...])
        ga, gb = gua[:, :D_EXPERT], gub[:, :D_EXPERT]
        ya = _dot(((ga * jax.nn.sigmoid(ga)) * gua[:, D_EXPERT:]).astype(BF), wda_ref[...])
        yb = _dot(((gb * jax.nn.sigmoid(gb)) * gub[:, D_EXPERT:]).astype(BF), wdb_ref[...])
        obuf_ref[j % 2] = (w[:, 0:1] * ya + w[:, 1:2] * yb).reshape((MOE_TM,) + TOK_TILE)

    @pl.when((j >= 2) & (j - 2 < n_used))
    def _():
        copies(j - 2, start=False)

    @pl.when((j >= 1) & (j < n_used))
    def _():
        copies(j - 1, start=True)
        experts()

    @pl.when((j == 0) & (j < n_used))
    def _():
        experts()

    @pl.when((j >= 1) & (j == n_used))
    def _():
        copies(j - 1, start=True)

    @pl.when((j == last) & (j < n_used))
    def _():
        copies(j, start=True)

    @pl.when((j == last) & (j - 1 < n_used))
    def _():
        copies(j - 1, start=False)

    @pl.when((j == last) & (j < n_used))
    def _():
        copies(j, start=False)


def _moe_sorted(layer, n_tok, hs, ws, src, ea, eb, n_used, inv, pw):
    n_tiles = hs.shape[0] // MOE_TM
    assert n_tiles >= 2
    wgu = lambda sel: pl.BlockSpec((None, None, D_MODEL, 2 * D_EXPERT),
                                   lambda j, src, ea, eb, *_: (layer, (ea, eb)[sel][j], 0, 0))
    wd = lambda sel: pl.BlockSpec((None, None, D_EXPERT, D_MODEL),
                                  lambda j, src, ea, eb, *_: (layer, (ea, eb)[sel][j], 0, 0))
    return pl.pallas_call(
        functools.partial(_moe_kernel, n_tok=n_tok),
        out_shape=jax.ShapeDtypeStruct((n_tok + 2 * MOE_TM,) + TOK_TILE, F32),
        grid_spec=pltpu.PrefetchScalarGridSpec(
            num_scalar_prefetch=5,
            grid=(n_tiles,),
            in_specs=[pl.BlockSpec((MOE_TM,) + TOK_TILE, lambda j, src, *_: (src[j], 0, 0)),
                      pl.BlockSpec((MOE_TM, LANES), lambda j, src, *_: (src[j], 0)),
                      wgu(0), wgu(1), wd(0), wd(1)],
            out_specs=pl.BlockSpec(memory_space=pl.ANY),
            scratch_shapes=[pltpu.VMEM((2, MOE_TM) + TOK_TILE, F32), pltpu.SemaphoreType.DMA((2,))]),
        compiler_params=_params(("arbitrary",)),
        name="moe",
    )(src, ea, eb, n_used, inv, hs, ws, pw["w_gu"], pw["w_gu"], pw["w_down"], pw["w_down"])


def _residual_kernel(ys_ref, x_ref, mod_ref, ng_ref, o_ref, *, tm):
    fx = ys_ref[...].reshape(tm, D_MODEL)
    o_ref[...] = x_ref[...] + mod_ref[5:6, :] * _rms(fx, ng_ref[3:4, :])


def _moe_residual(layer, ys, first_row, x, mod_l, mod_row, tm, pw):
    B, L, _ = x.shape
    T = B * L
    tiles_per_seq = L // tm
    first_blk = first_row // tm
    row = (lambda i: i // tiles_per_seq) if mod_row is None else (lambda i: mod_row)
    out = pl.pallas_call(
        functools.partial(_residual_kernel, tm=tm),
        out_shape=jax.ShapeDtypeStruct((T, D_MODEL), F32),
        grid=(T // tm,),
        in_specs=[pl.BlockSpec((tm,) + TOK_TILE, lambda i: (first_blk + i, 0, 0)),
                  pl.BlockSpec((tm, D_MODEL), lambda i: (i, 0)),
                  pl.BlockSpec((None, N_MOD, D_MODEL), lambda i: (row(i), 0, 0)),
                  pl.BlockSpec((None, 4, D_MODEL), lambda i: (layer, 0, 0))],
        out_specs=pl.BlockSpec((tm, D_MODEL), lambda i: (i, 0)),
        compiler_params=_params(("arbitrary",)),
        name="moe_residual",
    )(ys, x.reshape(T, D_MODEL), mod_l, pw["norm_g"])
    return out.reshape(B, L, D_MODEL)


def _moe(layer, groups, counts, mod_l, tm, pw):
    hrs = [g[0] for g in groups]
    routes = [g[1].reshape(-1, LANES) for g in groups]
    T = sum(hr.shape[0] for hr in hrs)
    n_tiles = T // MOE_TM + N_BUCKETS
    pos, zrow, src, ea, eb, n_used = _route_tables(routes, counts, n_tiles)
    hs, ws, inv = _scatter_rows(hrs, routes, pos, zrow, n_tiles, tm)
    ys = _moe_sorted(layer, T, hs, ws, src, ea, eb, n_used, inv, pw)
    outs, first = [], 0
    for hr, _, x, mod_row, tm_res in groups:
        outs.append(_moe_residual(layer, ys, first, x, mod_l, mod_row, tm_res, pw))
        first += hr.shape[0]
    return outs


def _axial(length, dim):
    rows = length // GRID_W
    row = jnp.repeat(jnp.arange(rows), GRID_W).astype(F32)
    col = jnp.tile(jnp.arange(GRID_W), rows).astype(F32)
    n = dim // 4
    inv = ROPE_BASE ** (-jnp.arange(n, dtype=F32) / n)
    ang = jnp.concatenate([row[:, None] * inv, col[:, None] * inv], axis=-1)
    return jnp.cos(ang), jnp.sin(ang)


def _rope_tables(length, rotate):
    if rotate:
        cos_a, sin_a = _axial(length, MLA_ROPE)
        cos_b, sin_b = _axial(length, SWA_HEAD_DIM)
    else:
        cos_a, sin_a = jnp.ones((length, MLA_ROPE // 2), F32), jnp.zeros((length, MLA_ROPE // 2), F32)
        cos_b, sin_b = jnp.ones((length, SWA_HEAD_DIM // 2), F32), jnp.zeros((length, SWA_HEAD_DIM // 2), F32)
    one = jnp.ones((length, MLA_NOPE), F32)
    zero = jnp.zeros((length, MLA_NOPE), F32)
    sa, sb, sc = MLA_SCALE * LOG2E, SWA_SCALE * LOG2E, DIFF_SCALE * LOG2E
    c_qa = jnp.tile(jnp.concatenate([one, cos_a, cos_a], -1), (1, MLA_HEADS)) * sa
    s_qa = jnp.tile(jnp.concatenate([zero, -sin_a, sin_a], -1), (1, MLA_HEADS)) * sa
    c_b = jnp.concatenate([cos_b, cos_b], -1)
    s_b = jnp.concatenate([-sin_b, sin_b], -1)
    c_c = jnp.tile(jnp.concatenate([cos_a, cos_a], -1), (1, 2 * DIFF_HEADS)) * sc
    s_c = jnp.tile(jnp.concatenate([-sin_a, sin_a], -1), (1, 2 * DIFF_HEADS)) * sc
    tab_c = jnp.concatenate([c_qa, jnp.tile(c_b, (1, SWA_HEADS)) * sb, jnp.tile(c_b, (1, SWA_KV_HEADS)), c_c], -1)
    tab_s = jnp.concatenate([s_qa, jnp.tile(s_b, (1, SWA_HEADS)) * sb, jnp.tile(s_b, (1, SWA_KV_HEADS)), s_c], -1)
    m_qa = jnp.tile(jnp.concatenate([jnp.zeros((MLA_NOPE,)), jnp.ones((MLA_ROPE // 2,)),
                                     jnp.zeros((MLA_ROPE // 2,))]), MLA_HEADS)
    m_b = jnp.concatenate([jnp.ones((SWA_HEAD_DIM // 2,)), jnp.zeros((SWA_HEAD_DIM // 2,))])
    m_c = jnp.concatenate([jnp.ones((DIFF_QK // 2,)), jnp.zeros((DIFF_QK // 2,))])
    msk = jnp.concatenate([m_qa, jnp.tile(m_b, SWA_HEADS + SWA_KV_HEADS), jnp.tile(m_c, 2 * DIFF_HEADS)])
    return tab_c, tab_s, msk.astype(F32)[None, :], cos_a.T, sin_a.T


def _prep_weights(norm_g, w_in, mla_q_gain, mla_kv_gain, mla_w_uq, mla_w_ukv, diff_subln, pool_w,
                  pool_scale, w_out, router_gw, router_gb, router_ew, router_eb, exp_w_gu, exp_w_down):
    sizes = (MLA_Q_RANK, MLA_KV_RANK, MLA_ROPE, 256, 128, 128, 256, 256, 256, MIX_D)
    offs = [0]
    for s in sizes:
        offs.append(offs[-1] + s)
    cq, ckv, kr, bq, bk, bv, dq, dk, dv, u = [w_in[:, :, offs[j]:offs[j + 1]] for j in range(len(sizes))]
    w_r = jnp.concatenate([cq, bq, dq, dv, u, ckv, bk, bv], axis=-1).astype(BF)
    w_kt = jnp.swapaxes(jnp.concatenate([ckv, kr, dk], axis=-1), 1, 2).astype(BF)
    ukv = mla_w_ukv.reshape(DEPTH, MLA_KV_RANK, MLA_HEADS, MLA_NOPE + MLA_V)
    w_uk = ukv[..., :MLA_NOPE].reshape(DEPTH, MLA_KV_RANK, MLA_HEADS * MLA_NOPE)
    w_uv = ukv[..., MLA_NOPE:].reshape(DEPTH, MLA_KV_RANK, MLA_HEADS * MLA_V)
    eye = jnp.eye(len(POOL_WINDOWS), dtype=F32)
    pool_bd = jnp.einsum('lgcd,gh->lgchd', pool_w, eye).reshape(DEPTH, MIX_D, MIX_D)
    r_w = jnp.concatenate([router_ew, router_gw], axis=-1)
    r_w = jnp.pad(r_w, ((0, 0), (0, 0), (0, LANES - r_w.shape[-1])))
    r_hi = r_w.astype(BF)
    r_lo = (r_w - r_hi.astype(F32)).astype(BF)
    r_b = jnp.concatenate([router_eb, router_gb], axis=-1)
    r_b = jnp.pad(r_b, ((0, 0), (0, LANES - r_b.shape[-1])))[:, None, :]
    return dict(
        norm_g=norm_g, w_r=w_r, w_kt=w_kt,
        q_gain=mla_q_gain[:, None, :], kv_gain_r=mla_kv_gain[:, None, :], kv_gain_c=mla_kv_gain[:, :, None],
        w_uq=mla_w_uq.astype(BF), w_ukt=jnp.swapaxes(w_uk, 1, 2).astype(BF), w_uv=w_uv.astype(BF),
        subln=jnp.tile(diff_subln, (1, LANES // DIFF_V))[:, None, :],
        pool_bd=pool_bd.astype(BF), pool_scale=pool_scale[:, None, :], w_out=w_out.astype(BF),
        r_w=jnp.concatenate([r_hi, r_lo], axis=-1), r_b=r_b, w_gu=exp_w_gu.astype(BF), w_down=exp_w_down.astype(BF))


def kernel(x, c, ctx, c_ctx, ada_w, ada_b, norm_g, w_in, mla_q_gain, mla_kv_gain, mla_w_uq, mla_w_ukv,
           swa_sink, diff_lambda, diff_subln, pool_w, pool_scale, w_out, router_gw, router_gb,
           router_ew, router_eb, exp_w_gu, exp_w_down):
    B, L, D = x.shape
    C = ctx.shape[1]
    assert D == D_MODEL and B + 1 <= MOD_ROWS and L % 512 == 0 and C % 256 == 0
    assert (B * L) % SCATTER_TM == 0 and (B * C) % SCATTER_TM == 0
    pw = _prep_weights(norm_g, w_in, mla_q_gain, mla_kv_gain, mla_w_uq, mla_w_ukv, diff_subln, pool_w,
                       pool_scale, w_out, router_gw, router_gb, router_ew, router_eb, exp_w_gu, exp_w_down)
    cc = jnp.concatenate([c, c_ctx[None, :], jnp.zeros((MOD_ROWS - B - 1, D), F32)], axis=0)
    mod = jnp.swapaxes(_adaln(cc, ada_w, ada_b), 1, 2)
    tabs_x = _rope_tables(L, True)
    tabs_c = _rope_tables(C, False)
    tm_x, tm_c, tq, tq_x = 512, 256, 256, 512

    h_ctx = ctx
    for layer in range(DEPTH):
        update_ctx = layer < DEPTH - 1
        lam_init = 0.8 - 0.6 * math.exp(-0.3 * layer)
        mod_l = mod[layer]
        qa, qb, qc, u, va, vb, vc, kb, kta, ktc = _inproj(layer, x, mod_l, None, tm_x, pw, tabs_x)
        cqa, cqb, cqc, cu, cva, cvb, cvc, ckb, ckta, cktc = _inproj(layer, h_ctx, mod_l, B, tm_c, pw, tabs_c)
        diff_extra = (diff_lambda, pw["subln"])

        oa = _dense_attn("mla", layer, qa, [kta, ckta], [va, cva], tq_x)
        ob = _swa(layer, swa_sink, qb, kb, vb, ckb, cvb, tq)
        oc = _dense_attn("diff", layer, qc, [ktc, cktc], [vc, cvc], tq_x, diff_extra, lam_init)
        x, hrx, rtx, counts = _outproj(layer, oa, ob, oc, u, x, mod_l, None, tm_x, pw, jnp.zeros((1, LANES), F32))
        groups = [(hrx, rtx, x, None, tm_x)]
        if update_ctx:
            coa = _dense_attn("mla", layer, cqa, [ckta], [cva], tq)
            cob = _swa(layer, swa_sink, cqb, None, None, ckb, cvb, tq)
            coc = _dense_attn("diff", layer, cqc, [cktc], [cvc], tq, diff_extra, lam_init)
            h_ctx, hrc, rtc, counts = _outproj(layer, coa, cob, coc, cu, h_ctx, mod_l, B, tm_c, pw, counts)
            groups.append((hrc, rtc, h_ctx, B, tm_c))
        outs = _moe(layer, groups, counts, mod_l, SCATTER_TM, pw)
        x = outs[0]
        if update_ctx:
            h_ctx = outs[1]
    return x
```
